```python
import jax, jax.numpy as jnp
from jax import lax
import numpy as np

D_MODEL = 1024
BATCH = 16
SEQ = 2048
DEPTH = 2
DEC_BATCH = 32
DEC_SEQ = 64
PAST_LEN = 2048

CHUNK = 64
N_A_LAYERS = DEPTH // 2
N_B_LAYERS = DEPTH - N_A_LAYERS
D_RNN = D_MODEL
GATE_BLOCK = 256
N_GATE_BLOCKS = D_RNN // GATE_BLOCK
CONV_W = 4
LRU_C = 8.0
HEAD_DIM = 64
N_HEADS = D_MODEL // HEAD_DIM
N_KV = 2
GROUP = N_HEADS // N_KV
WINDOW = 128
WIN_CHUNKS = WINDOW // CHUNK
EPS = 1e-6
NEG = -1e30

kernel_name = "yoco_rglru_swa_sink_stream_step"


def _rmsnorm(x, g):
    xf = x.astype(jnp.float32)
    y = xf * lax.rsqrt(jnp.mean(xf * xf, axis=-1, keepdims=True) + EPS)
    return (y * g.astype(jnp.float32)).astype(x.dtype)


def _causal_conv(u, buf, w, b):
    t = u.shape[1]
    up = jnp.concatenate([buf.astype(u.dtype), u], axis=1)
    out = b
    for j in range(CONV_W):
        out = out + w[j] * up[:, j:j + t]
    return out, up[:, -(CONV_W - 1):]


def _block_diag(u, w, b):
    bsz, t, _ = u.shape
    ub = u.reshape(bsz, t, N_GATE_BLOCKS, GATE_BLOCK)
    return jnp.einsum('btnc,ncd->btnd', ub, w).reshape(bsz, t, D_RNN) + b


def _rglru(u, h0, w_gx, b_gx, w_ga, b_ga, lam):
    uf = u.astype(jnp.float32)
    i_t = jax.nn.sigmoid(_block_diag(u, w_gx, b_gx).astype(jnp.float32))
    r_t = jax.nn.sigmoid(_block_diag(u, w_ga, b_ga).astype(jnp.float32))
    log_a = LRU_C * r_t * jax.nn.log_sigmoid(lam.astype(jnp.float32))
    a = jnp.exp(log_a)
    mult = jnp.sqrt(-jnp.expm1(2.0 * log_a))
    bx = mult * i_t * uf
    bx = bx.at[:, 0].add(a[:, 0] * h0.astype(jnp.float32))

    def combine(l, r):
        return (l[0] * r[0], r[0] * l[1] + r[1])

    _, h = lax.associative_scan(combine, (a, bx), axis=1)
    return h, h[:, -1]


def _layer_a(x, conv_buf, h0, norm_g, w_in, cw, cb, w_gx, b_gx, w_ga, b_ga, lam, w_out):
    u = _rmsnorm(x, norm_g) @ w_in
    xb, gate = jnp.split(u, 2, axis=-1)
    xc, new_buf = _causal_conv(xb, conv_buf, cw, cb)
    h, h_last = _rglru(xc, h0, w_gx, b_gx, w_ga, b_ga, lam)
    y = (h * jax.nn.silu(gate.astype(jnp.float32))).astype(x.dtype) @ w_out
    return x + y, new_buf.astype(conv_buf.dtype), h_last.astype(h0.dtype)


def _shared_kv(x, norm_g, w_kv, k_norm):
    bsz, t, _ = x.shape
    kv = _rmsnorm(x, norm_g) @ w_kv
    k, v = jnp.split(kv, 2, axis=-1)
    k = _rmsnorm(k.reshape(bsz, t, N_KV, HEAD_DIM), k_norm)
    v = v.reshape(bsz, t, N_KV, HEAD_DIM)
    return k, v


def _sink_attention(q, k, v, sink, valid):
    s = jnp.einsum('bnqhgd,bnkhd->bnhgqk', q.astype(jnp.float32), k.astype(jnp.float32))
    s = s * (HEAD_DIM ** -0.5)
    s = jnp.where(valid[None, :, None, None, None, :], s, NEG)
    sk = sink.astype(jnp.float32).reshape(1, 1, N_KV, GROUP, 1, 1)
    m = jnp.maximum(jnp.max(s, axis=-1, keepdims=True), sk)
    p = jnp.exp(s - m)
    denom = jnp.sum(p, axis=-1, keepdims=True) + jnp.exp(sk - m)
    return jnp.einsum('bnhgqk,bnkhd->bnqhgd', p / denom, v.astype(jnp.float32))


def _band_attend(q, k, v, sink):
    bsz, s = q.shape[:2]
    nc = s // CHUNK
    qc = q.reshape(bsz, nc, CHUNK, N_KV, GROUP, HEAD_DIM)
    pad = ((0, 0), (WINDOW, 0), (0, 0), (0, 0))
    kp = jnp.pad(k, pad).reshape(bsz, nc + WIN_CHUNKS, CHUNK, N_KV, HEAD_DIM)
    vp = jnp.pad(v, pad).reshape(bsz, nc + WIN_CHUNKS, CHUNK, N_KV, HEAD_DIM)
    kb = jnp.concatenate([kp[:, j:j + nc] for j in range(WIN_CHUNKS + 1)], axis=2)
    vb = jnp.concatenate([vp[:, j:j + nc] for j in range(WIN_CHUNKS + 1)], axis=2)
    key_pos = (jnp.arange(nc)[:, None] * CHUNK
               + jnp.arange((WIN_CHUNKS + 1) * CHUNK)[None, :] - WINDOW)
    o = _sink_attention(qc, kb, vb, sink, key_pos >= 0)
    return o.reshape(bsz, s, N_HEADS * HEAD_DIM)


def _step_attend(q, k_all, v_all, sink):
    bsz, t = q.shape[:2]
    qc = q.reshape(bsz, 1, t, N_KV, GROUP, HEAD_DIM)
    valid = jnp.ones((1, k_all.shape[1]), dtype=bool)
    o = _sink_attention(qc, k_all[:, None], v_all[:, None], sink, valid)
    return o.reshape(bsz, t, N_HEADS * HEAD_DIM)


def _layer_b(x, k_att, v_att, is_prompt, norm_g, w_in, q_norm, sink, w_out):
    bsz, t, _ = x.shape
    u = _rmsnorm(x, norm_g) @ w_in
    q, gate = jnp.split(u, 2, axis=-1)
    q = _rmsnorm(q.reshape(bsz, t, N_HEADS, HEAD_DIM), q_norm)
    if is_prompt:
        o = _band_attend(q, k_att, v_att, sink)
    else:
        o = _step_attend(q, k_att, v_att, sink)
    y = (o * jax.nn.silu(gate.astype(jnp.float32))).astype(x.dtype) @ w_out
    return x + y


def _trunk(x, conv_state, h_state, k_cache, v_cache, is_prompt, params):
    (norm_a, w_in_a, conv_w, conv_b, w_gate_x, b_gate_x, w_gate_a, b_gate_a, lru_lambda,
     w_out_a, norm_kv, w_kv, k_norm, norm_b, w_in_b, q_norm, sinks, w_out_b) = params
    new_conv, new_h = [], []
    k_att = v_att = k_buf = v_buf = None
    for layer in range(DEPTH):
        if layer < N_A_LAYERS:
            i = layer
            x, cbuf, hl = _layer_a(x, conv_state[i], h_state[i], norm_a[i], w_in_a[i], conv_w[i],
                                   conv_b[i], w_gate_x[i], b_gate_x[i], w_gate_a[i], b_gate_a[i],
                                   lru_lambda[i], w_out_a[i])
            new_conv.append(cbuf)
            new_h.append(hl)
        else:
            if layer == N_A_LAYERS:
                k_new, v_new = _shared_kv(x, norm_kv, w_kv, k_norm)
                if is_prompt:
                    k_att, v_att = k_new, v_new
                else:
                    k_att = jnp.concatenate([k_cache.astype(k_new.dtype), k_new], axis=1)
                    v_att = jnp.concatenate([v_cache.astype(v_new.dtype), v_new], axis=1)
                k_buf, v_buf = k_att[:, -WINDOW:], v_att[:, -WINDOW:]
            j = layer - N_A_LAYERS
            x = _layer_b(x, k_att, v_att, is_prompt, norm_b[j], w_in_b[j], q_norm[j], sinks[j],
                         w_out_b[j])
    return x, jnp.stack(new_conv), jnp.stack(new_h), k_buf, v_buf


def setup_inputs(seed: int = 0) -> dict:
    key = jax.random.key(seed)
    ks = jax.random.split(key, 32)
    f32 = jnp.float32
    nrm = lambda k, shape, scale: jax.random.normal(k, shape, f32) * scale
    u = jax.random.uniform(ks[14], (N_A_LAYERS, D_RNN), f32, 0.9, 0.999)
    return {
        "x_prompt": nrm(ks[0], (BATCH, SEQ, D_MODEL), 1.0),
        "x_sample": nrm(ks[1], (DEC_BATCH, DEC_SEQ, D_MODEL), 1.0),
        "state_conv": nrm(ks[2], (N_A_LAYERS, DEC_BATCH, CONV_W - 1, D_RNN), 1.0),
        "state_rglru": nrm(ks[3], (N_A_LAYERS, DEC_BATCH, D_RNN), 0.5),
        "cache_k_win": nrm(ks[4], (DEC_BATCH, WINDOW, N_KV, HEAD_DIM), 1.0),
        "cache_v_win": nrm(ks[5], (DEC_BATCH, WINDOW, N_KV, HEAD_DIM), 1.0),
        "norm_a": 1.0 + nrm(ks[6], (N_A_LAYERS, D_MODEL), 0.02),
        "w_in_a": nrm(ks[7], (N_A_LAYERS, D_MODEL, 2 * D_RNN), D_MODEL ** -0.5),
        "conv_w": nrm(ks[8], (N_A_LAYERS, CONV_W, D_RNN), CONV_W ** -0.5),
        "conv_b": nrm(ks[9], (N_A_LAYERS, D_RNN), 0.02),
        "w_gate_x": nrm(ks[10], (N_A_LAYERS, N_GATE_BLOCKS, GATE_BLOCK, GATE_BLOCK), GATE_BLOCK ** -0.5),
        "b_gate_x": nrm(ks[11], (N_A_LAYERS, D_RNN), 0.02),
        "w_gate_a": nrm(ks[12], (N_A_LAYERS, N_GATE_BLOCKS, GATE_BLOCK, GATE_BLOCK), GATE_BLOCK ** -0.5),
        "b_gate_a": nrm(ks[13], (N_A_LAYERS, D_RNN), 0.02),
        "lru_lambda": jnp.log(u) - jnp.log1p(-u),
        "w_out_a": nrm(ks[15], (N_A_LAYERS, D_RNN, D_MODEL), D_RNN ** -0.5),
        "norm_kv": 1.0 + nrm(ks[16], (D_MODEL,), 0.02),
        "w_kv": nrm(ks[17], (D_MODEL, 2 * N_KV * HEAD_DIM), D_MODEL ** -0.5),
        "k_norm": 1.0 + nrm(ks[18], (HEAD_DIM,), 0.02),
        "norm_b": 1.0 + nrm(ks[19], (N_B_LAYERS, D_MODEL), 0.02),
        "w_in_b": nrm(ks[20], (N_B_LAYERS, D_MODEL, 2 * N_HEADS * HEAD_DIM), D_MODEL ** -0.5),
        "q_norm": 1.0 + nrm(ks[21], (N_B_LAYERS, HEAD_DIM), 0.02),
        "sinks": nrm(ks[22], (N_B_LAYERS, N_HEADS), 0.5),
        "w_out_b": nrm(ks[23], (N_B_LAYERS, N_HEADS * HEAD_DIM, D_MODEL), (N_HEADS * HEAD_DIM) ** -0.5),
    }


def reference(x_prompt, x_sample, state_conv, state_rglru, cache_k_win, cache_v_win,
              norm_a, w_in_a, conv_w, conv_b, w_gate_x, b_gate_x, w_gate_a, b_gate_a, lru_lambda,
              w_out_a, norm_kv, w_kv, k_norm, norm_b, w_in_b, q_norm, sinks, w_out_b):
    params = (norm_a, w_in_a, conv_w, conv_b, w_gate_x, b_gate_x, w_gate_a, b_gate_a, lru_lambda,
              w_out_a, norm_kv, w_kv, k_norm, norm_b, w_in_b, q_norm, sinks, w_out_b)
    bp = x_prompt.shape[0]
    conv0 = jnp.zeros((N_A_LAYERS, bp, CONV_W - 1, D_RNN), x_prompt.dtype)
    h0 = jnp.zeros((N_A_LAYERS, bp, D_RNN), x_prompt.dtype)
    y_prompt, conv_p, h_p, k_p, v_p = _trunk(x_prompt, conv0, h0, None, None, True, params)
    y_sample, conv_s, h_s, k_s, v_s = _trunk(x_sample, state_conv, state_rglru, cache_k_win,
                                             cache_v_win, False, params)
    return (y_prompt, y_sample, conv_p, h_p, k_p, v_p, conv_s, h_s, k_s, v_s)
```

```python
import functools

import jax
import jax.numpy as jnp
from jax import lax
from jax.experimental import pallas as pl
from jax.experimental.pallas import tpu as pltpu

D_MODEL = 1024
D_RNN = 1024
GATE_BLOCK = 256
N_GATE_BLOCKS = D_RNN // GATE_BLOCK
CONV_W = 4
LRU_C = 8.0
HEAD_DIM = 64
N_HEADS = 16
N_KV = 2
GROUP = N_HEADS // N_KV
CHUNK = 64
WINDOW = 128
EPS = 1e-6
NEG = -1e30

SUBLANES = 8
LANES = 128
BF16_SUBLANES = 16
MXU_N = 256
VMEM_LIMIT_BYTES = 56 * 1024 * 1024

F32 = jnp.float32
BF16 = jnp.bfloat16

PAD = SUBLANES
KV_W = N_KV * HEAD_DIM
PAIR_W = 2 * HEAD_DIM
N_PAIRS = GROUP // 2


def _rmsnorm(x, g):
    ms = jnp.mean(x * x, axis=-1, keepdims=True)
    return x * lax.rsqrt(ms + EPS) * g


def _sigmoid(x):
    return 1.0 / (1.0 + jnp.exp(-x))


def _norm_rows_to(x_ref, g_ref, dst_ref, n_rows, blk):
    g = g_ref[...]

    def body(i, c):
        r = pl.multiple_of(i * blk, blk)
        dst_ref[pl.ds(r, blk), :] = _rmsnorm(x_ref[pl.ds(r, blk), :], g).astype(BF16)
        return c

    lax.fori_loop(0, n_rows // blk, body, 0)


def _layer_a_kernel(*refs, nseg, seg, chained):
    m = nseg * seg
    it = iter(refs)
    x_ref = next(it)
    if not chained:
        conv_in_ref = next(it)
        h_in_ref = next(it)
    (norm_a_ref, w_in_ref, conv_w_ref, conv_b_ref, w_gate_ref, b_gx_ref, b_ga_ref,
     lam_ref, w_out_ref, norm_kv_ref, w_kv_ref, k_norm_ref) = [next(it) for _ in range(12)]
    x1_ref, k_ref, v_ref, conv_out_ref, h_out_ref = [next(it) for _ in range(5)]
    (xn_scr, xbp_scr, gate_scr, xc_scr, xcb_scr, gx_scr, ga_scr, hy_scr,
     h_scr) = [next(it) for _ in range(9)]

    if chained:
        @pl.when(pl.program_id(1) == 0)
        def _():
            xbp_scr[0, 0:PAD, :] = jnp.zeros((PAD, D_RNN), F32)
            h_scr[...] = jnp.zeros_like(h_scr)
    else:
        for s in range(nseg):
            xbp_scr[s, 0:PAD, :] = conv_in_ref[s]

    _norm_rows_to(x_ref, norm_a_ref, xn_scr, m, 32)
    for n in range(D_RNN // MXU_N):
        cols = slice(n * MXU_N, (n + 1) * MXU_N)
        ux = jnp.dot(xn_scr[...], w_in_ref[:, cols], preferred_element_type=F32)
        for s in range(nseg):
            xbp_scr[s, PAD:PAD + seg, cols] = ux[s * seg:(s + 1) * seg]
        gate_scr[:, cols] = jnp.dot(
            xn_scr[...], w_in_ref[:, D_RNN + n * MXU_N:D_RNN + (n + 1) * MXU_N],
            preferred_element_type=F32)

    cw = conv_w_ref[...]
    cb = conv_b_ref[...]
    blk = 32
    for s in range(nseg):
        def conv_body(i, c, s=s):
            r = pl.multiple_of(i * blk, blk)
            ext = xbp_scr[s, pl.ds(r, PAD + blk), :]
            acc = cb
            for j in range(CONV_W):
                back = CONV_W - 1 - j
                tap = ext if back == 0 else pltpu.roll(ext, back, 0)
                acc = acc + cw[j:j + 1, :] * tap[PAD:, :]
            xc_scr[pl.ds(s * seg + r, blk), :] = acc
            xcb_scr[pl.ds(s * seg + r, blk), :] = acc.astype(BF16)
            return c

        lax.fori_loop(0, seg // blk, conv_body, 0)
        tail = xbp_scr[s, seg:seg + PAD, :]
        conv_out_ref[s] = tail
        if chained:
            xbp_scr[0, 0:PAD, :] = tail

    for n in range(N_GATE_BLOCKS):
        cols = slice(n * GATE_BLOCK, (n + 1) * GATE_BLOCK)
        res = jnp.dot(xcb_scr[:, cols], w_gate_ref[n], preferred_element_type=F32)
        gx_scr[:, cols] = res[:, :GATE_BLOCK]
        ga_scr[:, cols] = res[:, GATE_BLOCK:]

    lam = lam_ref[...]
    log_sig_lam = jnp.minimum(lam, 0.0) - jnp.log1p(jnp.exp(-jnp.abs(lam)))
    c_a = LRU_C * log_sig_lam
    b_gx = b_gx_ref[...]
    b_ga = b_ga_ref[...]
    row = lax.broadcasted_iota(jnp.int32, (SUBLANES, D_RNN), 0)

    def scan_group(r, carry):
        rows = pl.ds(r, SUBLANES)
        i_t = _sigmoid(gx_scr[rows, :] + b_gx)
        r_t = _sigmoid(ga_scr[rows, :] + b_ga)
        log_a = r_t * c_a
        a = jnp.exp(log_a)
        mult = jnp.sqrt(-jnp.tanh(log_a) * (a * a + 1.0))
        b = mult * i_t * xc_scr[rows, :]
        for sh in (1, 2, 4):
            keep = row >= sh
            a_sh = pltpu.roll(a, sh, 0)
            b_sh = pltpu.roll(b, sh, 0)
            b = jnp.where(keep, a * b_sh + b, b)
            a = jnp.where(keep, a * a_sh, a)
        h = a * carry + b
        gate = gate_scr[rows, :]
        hy = h * (gate * _sigmoid(gate))
        return hy, h[SUBLANES - 1:SUBLANES, :]

    for s in range(nseg):
        carry0 = h_scr[0:1, :] if chained else h_in_ref[s]

        def scan_body(i, carry, s=s):
            r = pl.multiple_of(s * seg + i * BF16_SUBLANES, BF16_SUBLANES)
            hy0, carry = scan_group(r, carry)
            hy1, carry = scan_group(r + SUBLANES, carry)
            hy_scr[pl.ds(r, BF16_SUBLANES), :] = jnp.concatenate([hy0, hy1], axis=0).astype(BF16)
            return carry

        carry = lax.fori_loop(0, seg // BF16_SUBLANES, scan_body, carry0)
        h_out_ref[s] = carry
        if chained:
            h_scr[0:1, :] = carry

    for n in range(D_MODEL // MXU_N):
        cols = slice(n * MXU_N, (n + 1) * MXU_N)
        x1_ref[:, cols] = x_ref[:, cols] + jnp.dot(
            hy_scr[...], w_out_ref[:, cols], preferred_element_type=F32)

    _norm_rows_to(x1_ref, norm_kv_ref, xn_scr, m, 32)
    kv = jnp.dot(xn_scr[...], w_kv_ref[...], preferred_element_type=F32)
    k = kv[:, :KV_W]
    lo = lax.broadcasted_iota(jnp.int32, (m, KV_W), 1) < HEAD_DIM
    k2 = k * k
    ms_lo = jnp.sum(jnp.where(lo, k2, 0.0), axis=-1, keepdims=True) * (1.0 / HEAD_DIM)
    ms_hi = jnp.sum(jnp.where(lo, 0.0, k2), axis=-1, keepdims=True) * (1.0 / HEAD_DIM)
    rs = jnp.where(lo, lax.rsqrt(ms_lo + EPS), lax.rsqrt(ms_hi + EPS))
    k_ref[...] = k * rs * k_norm_ref[...]
    v_ref[...] = kv[:, KV_W:]


def _layer_a(x2d, conv_in, h_in, p, *, n_batch, t_len, nseg, seg, chained):
    m = nseg * seg
    rows = n_batch * t_len
    const2 = lambda *_: (0, 0)
    const3 = lambda *_: (0, 0, 0)
    if chained:
        steps = t_len // m
        grid = (n_batch, steps)
        row_map = lambda b, t: (b * steps + t, 0)
        st_map = lambda b, t: (b, 0, 0)
        sems = ("arbitrary", "arbitrary")
        state_in, state_specs = [], []
    else:
        grid = (n_batch // nseg,)
        row_map = lambda i: (i, 0)
        st_map = lambda i: (i, 0, 0)
        sems = ("arbitrary",)
        state_in = [conv_in, h_in]
        state_specs = [pl.BlockSpec((nseg, PAD, D_RNN), st_map),
                       pl.BlockSpec((nseg, 1, D_RNN), st_map)]

    weights = [p["norm_a"], p["w_in_a"], p["conv_w"], p["conv_b"], p["w_gate"], p["b_gx"],
               p["b_ga"], p["lam"], p["w_out_a"], p["norm_kv"], p["w_kv"], p["k_norm2"]]
    w_specs = [pl.BlockSpec(w.shape, const3 if w.ndim == 3 else const2) for w in weights]

    out_shape = [
        jax.ShapeDtypeStruct((rows, D_MODEL), F32),
        jax.ShapeDtypeStruct((rows, KV_W), F32),
        jax.ShapeDtypeStruct((rows, KV_W), F32),
        jax.ShapeDtypeStruct((n_batch, PAD, D_RNN), F32),
        jax.ShapeDtypeStruct((n_batch, 1, D_RNN), F32),
    ]
    out_specs = [
        pl.BlockSpec((m, D_MODEL), row_map),
        pl.BlockSpec((m, KV_W), row_map),
        pl.BlockSpec((m, KV_W), row_map),
        pl.BlockSpec((nseg, PAD, D_RNN), st_map),
        pl.BlockSpec((nseg, 1, D_RNN), st_map),
    ]
    scratch = [
        pltpu.VMEM((m, D_MODEL), BF16),
        pltpu.VMEM((nseg, PAD + seg, D_RNN), F32),
        pltpu.VMEM((m, D_RNN), F32),
        pltpu.VMEM((m, D_RNN), F32),
        pltpu.VMEM((m, D_RNN), BF16),
        pltpu.VMEM((m, D_RNN), F32),
        pltpu.VMEM((m, D_RNN), F32),
        pltpu.VMEM((m, D_RNN), BF16),
        pltpu.VMEM((SUBLANES, D_RNN), F32),
    ]
    return pl.pallas_call(
        functools.partial(_layer_a_kernel, nseg=nseg, seg=seg, chained=chained),
        grid=grid,
        in_specs=[pl.BlockSpec((m, D_MODEL), row_map)] + state_specs + w_specs,
        out_specs=out_specs,
        out_shape=out_shape,
        scratch_shapes=scratch,
        compiler_params=pltpu.CompilerParams(
            dimension_semantics=sems, vmem_limit_bytes=VMEM_LIMIT_BYTES),
        name="layer_a_prompt" if chained else "layer_a_sample",
    )(x2d, *state_in, *weights)


def _layer_b_kernel(x1_ref, kh_ref, kc_ref, vh_ref, vc_ref, norm_b_ref, w_in_ref, hsel_ref,
                    hexp_ref, qg_ref, sinks_ref, w_out_ref, y_ref,
                    xn_scr, q_scr, gate_scr, qh_scr, ke_scr, ko_scr, ve_scr, vo_scr, o_scr,
                    *, nseg, seg, mask_first_halo):
    m = nseg * seg
    n_chunks = seg // CHUNK
    kv_rows = WINDOW + seg

    _norm_rows_to(x1_ref, norm_b_ref, xn_scr, m, 32)
    for n in range(D_MODEL // MXU_N):
        cols = slice(n * MXU_N, (n + 1) * MXU_N)
        q_scr[:, cols] = jnp.dot(xn_scr[...], w_in_ref[:, cols], preferred_element_type=F32)
        gate_scr[:, cols] = jnp.dot(
            xn_scr[...], w_in_ref[:, D_MODEL + n * MXU_N:D_MODEL + (n + 1) * MXU_N],
            preferred_element_type=F32)

    q = q_scr[...]
    ms = jnp.dot((q * q).astype(BF16), hsel_ref[...], preferred_element_type=F32)
    rs = lax.rsqrt(ms + EPS)
    rs_hi = rs.astype(BF16)
    rs_lo = (rs - rs_hi.astype(F32)).astype(BF16)
    rs_full = jnp.dot(jnp.concatenate([rs_hi, rs_lo], axis=1), hexp_ref[...],
                      preferred_element_type=F32)
    qh_scr[...] = (q * rs_full * qg_ref[...]).astype(BF16)

    lane = lax.broadcasted_iota(jnp.int32, (CHUNK, KV_W), 1)
    lo = lane < HEAD_DIM
    first = pl.program_id(1) == 0 if mask_first_halo else None

    for g in range(N_KV):
        for s in range(nseg):
            for src, e_scr, o_scr2 in ((0, ke_scr, ko_scr), (1, ve_scr, vo_scr)):
                h_ref, c_ref = (kh_ref, kc_ref) if src == 0 else (vh_ref, vc_ref)
                for r0, nrows, val in ((0, WINDOW, h_ref[s]),
                                       (WINDOW, seg, c_ref[s * seg:(s + 1) * seg, :])):
                    lane_kv = lax.broadcasted_iota(jnp.int32, val.shape, 1)
                    is_lo = lane_kv < HEAD_DIM
                    swapped = pltpu.roll(val, HEAD_DIM, 1)
                    low = val if g == 0 else swapped
                    high = swapped if g == 0 else val
                    e_scr[s, r0:r0 + nrows, :] = jnp.where(is_lo, low, 0.0).astype(BF16)
                    o_scr2[s, r0:r0 + nrows, :] = jnp.where(is_lo, 0.0, high).astype(BF16)

        for s in range(nseg):
            for c in range(n_chunks):
                q0 = s * seg + c * CHUNK
                k0 = c * CHUNK
                qst = jnp.concatenate(
                    [qh_scr[q0:q0 + CHUNK, g * GROUP * HEAD_DIM + j * PAIR_W:
                            g * GROUP * HEAD_DIM + (j + 1) * PAIR_W] for j in range(N_PAIRS)],
                    axis=0)
                kmat = jnp.concatenate(
                    [ke_scr[s, k0:k0 + WINDOW, :], ko_scr[s, k0:k0 + WINDOW, :],
                     ke_scr[s, k0 + WINDOW:k0 + WINDOW + CHUNK, :],
                     ko_scr[s, k0 + WINDOW:k0 + WINDOW + CHUNK, :]], axis=0)
                vmat = jnp.concatenate(
                    [ve_scr[s, k0:k0 + WINDOW, :], vo_scr[s, k0:k0 + WINDOW, :],
                     ve_scr[s, k0 + WINDOW:k0 + WINDOW + CHUNK, :],
                     vo_scr[s, k0 + WINDOW:k0 + WINDOW + CHUNK, :]], axis=0)
                sc = lax.dot_general(qst, kmat, (((1,), (1,)), ((), ())),
                                     preferred_element_type=F32)
                p_rows, scale_rows = [], []
                for j in range(N_PAIRS):
                    sj = sc[j * CHUNK:(j + 1) * CHUNK]
                    c0 = sj[:, 0:KV_W]
                    c1 = sj[:, KV_W:2 * KV_W]
                    c2 = sj[:, 2 * KV_W:]
                    if mask_first_halo and c * CHUNK < WINDOW:
                        n_bad = jnp.where(first, WINDOW - c * CHUNK, 0)
                        bad = lane < n_bad
                        c0 = jnp.where(bad, NEG, c0)
                        c1 = jnp.where(bad, NEG, c1)
                    sink_e = sinks_ref[g * GROUP + 2 * j]
                    sink_o = sinks_ref[g * GROUP + 2 * j + 1]
                    m_e = jnp.maximum(jnp.max(jnp.maximum(c0, jnp.where(lo, c2, NEG)),
                                              axis=-1, keepdims=True), sink_e)
                    m_o = jnp.maximum(jnp.max(jnp.maximum(c1, jnp.where(lo, NEG, c2)),
                                              axis=-1, keepdims=True), sink_o)
                    p0 = jnp.exp(c0 - m_e)
                    p1 = jnp.exp(c1 - m_o)
                    p2 = jnp.exp(c2 - jnp.where(lo, m_e, m_o))
                    d_e = jnp.sum(p0 + jnp.where(lo, p2, 0.0), axis=-1, keepdims=True) \
                        + jnp.exp(sink_e - m_e)
                    d_o = jnp.sum(p1 + jnp.where(lo, 0.0, p2), axis=-1, keepdims=True) \
                        + jnp.exp(sink_o - m_o)
                    p_rows.append(jnp.concatenate([p0, p1, p2], axis=1).astype(BF16))
                    scale_rows.append(jnp.where(lo, 1.0 / d_e, 1.0 / d_o))
                pmat = jnp.concatenate(p_rows, axis=0)
                o = jnp.dot(pmat, vmat, preferred_element_type=F32)
                for j in range(N_PAIRS):
                    c_lo = g * GROUP * HEAD_DIM + j * PAIR_W
                    o_scr[q0:q0 + CHUNK, c_lo:c_lo + PAIR_W] = (
                        o[j * CHUNK:(j + 1) * CHUNK] * scale_rows[j])

    blk = 32

    def gate_body(i, c):
        r = pl.multiple_of(i * blk, blk)
        gate = gate_scr[pl.ds(r, blk), :]
        xn_scr[pl.ds(r, blk), :] = (o_scr[pl.ds(r, blk), :] * (gate * _sigmoid(gate))).astype(BF16)
        return c

    lax.fori_loop(0, m // blk, gate_body, 0)
    for n in range(D_MODEL // MXU_N):
        cols = slice(n * MXU_N, (n + 1) * MXU_N)
        y_ref[:, cols] = x1_ref[:, cols] + jnp.dot(
            xn_scr[...], w_out_ref[:, cols], preferred_element_type=F32)


def _layer_b(x1, k_halo, v_halo, k_cur, v_cur, p, *, n_batch, t_len, nseg, seg, prompt):
    m = nseg * seg
    rows = n_batch * t_len
    const2 = lambda *_: (0, 0)
    if prompt:
        steps = t_len // m
        per_b = t_len // WINDOW
        grid = (n_batch, steps)
        row_map = lambda b, t: (b * steps + t, 0)
        halo_map = lambda b, t: (b * per_b + jnp.maximum(t * (m // WINDOW) - 1, 0), 0, 0)
        sems = ("arbitrary", "arbitrary")
    else:
        grid = (n_batch // nseg, 1)
        row_map = lambda i, t: (i, 0)
        halo_map = lambda i, t: (i, 0, 0)
        sems = ("arbitrary", "arbitrary")

    weights = [p["norm_b"], p["w_in_b"], p["hsel"], p["hexp"], p["qg"]]
    in_specs = [
        pl.BlockSpec((m, D_MODEL), row_map),
        pl.BlockSpec((nseg, WINDOW, KV_W), halo_map),
        pl.BlockSpec((m, KV_W), row_map),
        pl.BlockSpec((nseg, WINDOW, KV_W), halo_map),
        pl.BlockSpec((m, KV_W), row_map),
    ] + [pl.BlockSpec(w.shape, const2) for w in weights] + [
        pl.BlockSpec(memory_space=pltpu.SMEM),
        pl.BlockSpec(p["w_out_b"].shape, const2),
    ]
    kv_rows = WINDOW + seg
    scratch = [
        pltpu.VMEM((m, D_MODEL), BF16),
        pltpu.VMEM((m, D_MODEL), F32),
        pltpu.VMEM((m, D_MODEL), F32),
        pltpu.VMEM((m, D_MODEL), BF16),
        pltpu.VMEM((nseg, kv_rows, KV_W), BF16),
        pltpu.VMEM((nseg, kv_rows, KV_W), BF16),
        pltpu.VMEM((nseg, kv_rows, KV_W), BF16),
        pltpu.VMEM((nseg, kv_rows, KV_W), BF16),
        pltpu.VMEM((m, D_MODEL), F32),
    ]
    return pl.pallas_call(
        functools.partial(_layer_b_kernel, nseg=nseg, seg=seg, mask_first_halo=prompt),
        grid=grid,
        in_specs=in_specs,
        out_specs=pl.BlockSpec((m, D_MODEL), row_map),
        out_shape=jax.ShapeDtypeStruct((rows, D_MODEL), F32),
        scratch_shapes=scratch,
        compiler_params=pltpu.CompilerParams(
            dimension_semantics=sems, vmem_limit_bytes=VMEM_LIMIT_BYTES),
        name="layer_b_prompt" if prompt else "layer_b_sample",
    )(x1, k_halo, k_cur, v_halo, v_cur, *weights, p["sinks"], p["w_out_b"])


def _prep_params(norm_a, w_in_a, conv_w, conv_b, w_gate_x, b_gate_x, w_gate_a, b_gate_a,
                 lru_lambda, w_out_a, norm_kv, w_kv, k_norm, norm_b, w_in_b, q_norm, sinks,
                 w_out_b):
    row = lambda v: v.reshape(1, -1).astype(F32)
    head_of_col = jnp.arange(D_MODEL) // HEAD_DIM
    hsel = (head_of_col[:, None] == jnp.arange(LANES)[None, :]).astype(F32) * (1.0 / HEAD_DIM)
    hexp = (jnp.arange(2 * LANES)[:, None] % LANES == head_of_col[None, :]).astype(BF16)
    return {
        "norm_a": row(norm_a[0]),
        "w_in_a": w_in_a[0].astype(BF16),
        "conv_w": conv_w[0].astype(F32),
        "conv_b": row(conv_b[0]),
        "w_gate": jnp.concatenate([w_gate_x[0], w_gate_a[0]], axis=-1).astype(BF16),
        "b_gx": row(b_gate_x[0]),
        "b_ga": row(b_gate_a[0]),
        "lam": row(lru_lambda[0]),
        "w_out_a": w_out_a[0].astype(BF16),
        "norm_kv": row(norm_kv),
        "w_kv": w_kv.astype(BF16),
        "k_norm2": row(jnp.tile(k_norm, N_KV)),
        "norm_b": row(norm_b[0]),
        "w_in_b": w_in_b[0].astype(BF16),
        "hsel": hsel.astype(BF16),
        "hexp": hexp,
        "qg": row(jnp.tile(q_norm[0], N_HEADS) * (HEAD_DIM ** -0.5)),
        "sinks": sinks[0].astype(F32),
        "w_out_b": w_out_b[0].astype(BF16),
    }


def kernel(x_prompt, x_sample, state_conv, state_rglru, cache_k_win, cache_v_win, norm_a, w_in_a, conv_w, conv_b, w_gate_x, b_gate_x, w_gate_a, b_gate_a, lru_lambda, w_out_a, norm_kv, w_kv, k_norm, norm_b, w_in_b, q_norm, sinks, w_out_b):
    assert norm_a.shape[0] == 1 and norm_b.shape[0] == 1, "one recurrent + one attention layer"
    p = _prep_params(norm_a, w_in_a, conv_w, conv_b, w_gate_x, b_gate_x, w_gate_a, b_gate_a,
                     lru_lambda, w_out_a, norm_kv, w_kv, k_norm, norm_b, w_in_b, q_norm, sinks,
                     w_out_b)
    bp, tp, _ = x_prompt.shape
    bs, ts, _ = x_sample.shape

    tile_a = 512
    x1p, kp, vp, conv_p, h_p = _layer_a(
        x_prompt.reshape(bp * tp, D_MODEL), None, None, p,
        n_batch=bp, t_len=tp, nseg=1, seg=tile_a, chained=True)
    tile_b = 256
    kp3 = kp.reshape(bp * tp // WINDOW, WINDOW, KV_W)
    vp3 = vp.reshape(bp * tp // WINDOW, WINDOW, KV_W)
    y_p = _layer_b(x1p, kp3, vp3, kp, vp, p, n_batch=bp, t_len=tp, nseg=1, seg=tile_b,
                   prompt=True)

    conv_in = jnp.pad(state_conv[0], ((0, 0), (PAD - (CONV_W - 1), 0), (0, 0)))
    x1s, ks, vs, conv_s, h_s = _layer_a(
        x_sample.reshape(bs * ts, D_MODEL), conv_in, state_rglru[0].reshape(bs, 1, D_RNN),
        p, n_batch=bs, t_len=ts, nseg=8, seg=ts, chained=False)
    ck = cache_k_win.reshape(bs, WINDOW, KV_W)
    cv = cache_v_win.reshape(bs, WINDOW, KV_W)
    y_s = _layer_b(x1s, ck, cv, ks, vs, p, n_batch=bs, t_len=ts, nseg=4, seg=ts, prompt=False)

    kp4 = kp.reshape(bp, tp, N_KV, HEAD_DIM)[:, -WINDOW:]
    vp4 = vp.reshape(bp, tp, N_KV, HEAD_DIM)[:, -WINDOW:]
    ks4 = jnp.concatenate([cache_k_win, ks.reshape(bs, ts, N_KV, HEAD_DIM)], axis=1)[:, -WINDOW:]
    vs4 = jnp.concatenate([cache_v_win, vs.reshape(bs, ts, N_KV, HEAD_DIM)], axis=1)[:, -WINDOW:]
    return (y_p.reshape(bp, tp, D_MODEL), y_s.reshape(bs, ts, D_MODEL),
            conv_p[None, :, PAD - (CONV_W - 1):], h_p.reshape(1, bp, D_RNN), kp4, vp4,
            conv_s[None, :, PAD - (CONV_W - 1):], h_s.reshape(1, bs, D_RNN), ks4, vs4)
```

```python
import functools

import jax
import jax.numpy as jnp
from jax import lax
from jax.experimental import pallas as pl
from jax.experimental.pallas import tpu as pltpu

D_MODEL = 1024
D_RNN = 1024
GATE_BLOCK = 256
N_GATE_BLOCKS = D_RNN // GATE_BLOCK
CONV_W = 4
LRU_C = 8.0
HEAD_DIM = 64
N_HEADS = 16
N_KV = 2
GROUP = N_HEADS // N_KV
CHUNK = 64
WINDOW = 128
EPS = 1e-6
NEG = -1e30

SUBLANES = 8
LANES = 128
BF16_SUBLANES = 16
MXU_N = 256
VMEM_LIMIT_BYTES = 56 * 1024 * 1024

F32 = jnp.float32
BF16 = jnp.bfloat16

KV_W = N_KV * HEAD_DIM
PAIR_W = 2 * HEAD_DIM
N_PAIRS = GROUP // 2

NSEQ = SUBLANES
A_STEPS = 64
A_ROWS = A_STEPS * NSEQ
CONV_PRE = (CONV_W - 1) * NSEQ


def _rmsnorm(x, g):
    ms = jnp.mean(x * x, axis=-1, keepdims=True)
    return x * lax.rsqrt(ms + EPS) * g


def _sigmoid(x):
    return 0.5 * jnp.tanh(0.5 * x) + 0.5


def _silu(x):
    h = 0.5 * x
    return h * jnp.tanh(h) + h


def _norm_rows_to(x_ref, g_ref, dst_ref, n_rows, blk):
    g = g_ref[...]

    def body(i, c):
        r = pl.multiple_of(i * blk, blk)
        dst_ref[pl.ds(r, blk), :] = _rmsnorm(x_ref[pl.ds(r, blk), :], g).astype(BF16)
        return c

    lax.fori_loop(0, n_rows // blk, body, 0)


def _layer_a_kernel(x_hbm, conv_in_ref, h_in_ref,
                    norm_a_ref, w_in_ref, conv_w_ref, conv_b_ref, w_gate_ref, b_gx_ref, b_ga_ref,
                    lam_ref, w_out_ref, norm_kv_ref, w_kv_ref, k_norm_ref,
                    x1_hbm, k_hbm, v_hbm, conv_out_ref, h_out_ref,
                    xin, x1o, ko, vo, sem_in, sem_out,
                    xn_scr, xbp_scr, gate_scr, xc_scr, xcb_scr, gx_scr, ga_scr, hy_scr, h_scr):
    g = pl.program_id(0)
    t = pl.program_id(1)
    nt = pl.num_programs(1)
    total = pl.num_programs(0) * nt
    i = g * nt + t
    slot = lax.rem(i, 2)

    def in_copies(step, sl):
        gg = step // nt
        tt = lax.rem(step, nt)
        return [pltpu.make_async_copy(x_hbm.at[gg * NSEQ + j, pl.ds(tt * A_STEPS, A_STEPS), :],
                                      xin.at[sl, :, j, :], sem_in.at[sl]) for j in range(NSEQ)]

    def out_copies(step, sl):
        gg = step // nt
        tt = lax.rem(step, nt)
        cps = []
        for buf, dst in ((x1o, x1_hbm), (ko, k_hbm), (vo, v_hbm)):
            cps += [pltpu.make_async_copy(buf.at[sl, :, j, :],
                                          dst.at[gg * NSEQ + j, pl.ds(tt * A_STEPS, A_STEPS), :],
                                          sem_out.at[sl]) for j in range(NSEQ)]
        return cps

    @pl.when(i == 0)
    def _():
        for c in in_copies(i, slot):
            c.start()

    @pl.when(i + 1 < total)
    def _():
        for c in in_copies(i + 1, 1 - slot):
            c.start()

    @pl.when(t == 0)
    def _():
        xbp_scr[0:CONV_PRE, :] = conv_in_ref[...].reshape(CONV_PRE, D_RNN)
        h_scr[...] = h_in_ref[...]

    for c in in_copies(i, slot):
        c.wait()

    x_ref = xin.at[slot].reshape(A_ROWS, D_MODEL)
    x1_ref = x1o.at[slot].reshape(A_ROWS, D_MODEL)
    k_ref = ko.at[slot].reshape(A_ROWS, KV_W)
    v_ref = vo.at[slot].reshape(A_ROWS, KV_W)

    _norm_rows_to(x_ref, norm_a_ref, xn_scr, A_ROWS, 64)
    for n in range(D_RNN // MXU_N):
        cols = slice(n * MXU_N, (n + 1) * MXU_N)
        xbp_scr[CONV_PRE:, cols] = jnp.dot(xn_scr[...], w_in_ref[:, cols],
                                           preferred_element_type=F32)
        gate_scr[:, cols] = jnp.dot(
            xn_scr[...], w_in_ref[:, D_RNN + n * MXU_N:D_RNN + (n + 1) * MXU_N],
            preferred_element_type=F32)

    cw = conv_w_ref[...]
    cb = conv_b_ref[...]
    blk = 32

    def conv_body(b, c):
        r = pl.multiple_of(b * blk, blk)
        acc = cb
        for j in range(CONV_W):
            acc = acc + cw[j:j + 1, :] * xbp_scr[pl.ds(r + j * NSEQ, blk), :]
        xc_scr[pl.ds(r, blk), :] = acc
        xcb_scr[pl.ds(r, blk), :] = acc.astype(BF16)
        return c

    lax.fori_loop(0, A_ROWS // blk, conv_body, 0)
    tail = xbp_scr[A_ROWS:A_ROWS + CONV_PRE, :]
    conv_out_ref[...] = tail.reshape(CONV_W - 1, NSEQ, D_RNN)
    xbp_scr[0:CONV_PRE, :] = tail

    for n in range(N_GATE_BLOCKS):
        cols = slice(n * GATE_BLOCK, (n + 1) * GATE_BLOCK)
        res = jnp.dot(xcb_scr[:, cols], w_gate_ref[n], preferred_element_type=F32)
        gx_scr[:, cols] = res[:, :GATE_BLOCK]
        ga_scr[:, cols] = res[:, GATE_BLOCK:]

    lam = lam_ref[...]
    log_sig_lam = jnp.minimum(lam, 0.0) - jnp.log1p(jnp.exp(-jnp.abs(lam)))
    c_a = LRU_C * log_sig_lam
    b_gx = b_gx_ref[...]
    b_ga = b_ga_ref[...]

    def scan_step(r, h):
        rows = pl.ds(r, NSEQ)
        i_t = _sigmoid(gx_scr[rows, :] + b_gx)
        r_t = _sigmoid(ga_scr[rows, :] + b_ga)
        log_a = r_t * c_a
        a = jnp.exp(log_a)
        mult = jnp.sqrt(-jnp.tanh(log_a) * (a * a + 1.0))
        h = a * h + mult * i_t * xc_scr[rows, :]
        return h * _silu(gate_scr[rows, :]), h

    def scan_body(b, h):
        r = pl.multiple_of(b * BF16_SUBLANES, BF16_SUBLANES)
        hy0, h = scan_step(r, h)
        hy1, h = scan_step(r + NSEQ, h)
        hy_scr[pl.ds(r, BF16_SUBLANES), :] = jnp.concatenate([hy0, hy1], axis=0).astype(BF16)
        return h

    h_last = lax.fori_loop(0, A_ROWS // BF16_SUBLANES, scan_body, h_scr[...])
    h_scr[...] = h_last
    h_out_ref[...] = h_last

    @pl.when(i >= 2)
    def _():
        for c in out_copies(i - 2, slot):
            c.wait()

    for n in range(D_MODEL // MXU_N):
        cols = slice(n * MXU_N, (n + 1) * MXU_N)
        x1_ref[:, cols] = x_ref[:, cols] + jnp.dot(
            hy_scr[...], w_out_ref[:, cols], preferred_element_type=F32)

    _norm_rows_to(x1_ref, norm_kv_ref, xn_scr, A_ROWS, 64)
    kv = jnp.dot(xn_scr[...], w_kv_ref[...], preferred_element_type=F32)
    k = kv[:, :KV_W]
    lo = lax.broadcasted_iota(jnp.int32, (A_ROWS, KV_W), 1) < HEAD_DIM
    k2 = k * k
    ms_lo = jnp.sum(jnp.where(lo, k2, 0.0), axis=-1, keepdims=True) * (1.0 / HEAD_DIM)
    ms_hi = jnp.sum(jnp.where(lo, 0.0, k2), axis=-1, keepdims=True) * (1.0 / HEAD_DIM)
    rs = jnp.where(lo, lax.rsqrt(ms_lo + EPS), lax.rsqrt(ms_hi + EPS))
    k_ref[...] = k * rs * k_norm_ref[...]
    v_ref[...] = kv[:, KV_W:]

    for c in out_copies(i, slot):
        c.start()

    @pl.when(i == total - 1)
    def _():
        @pl.when(i >= 1)
        def _():
            for c in out_copies(i - 1, 1 - slot):
                c.wait()
        for c in out_copies(i, slot):
            c.wait()


def _layer_a(x, conv_in, h_in, p, name):
    n_seq, t_len, _ = x.shape
    assert n_seq % NSEQ == 0 and t_len % A_STEPS == 0
    grid = (n_seq // NSEQ, t_len // A_STEPS)
    const2 = lambda g, t: (0, 0)
    const3 = lambda g, t: (0, 0, 0)
    any_spec = pl.BlockSpec(memory_space=pl.ANY)

    weights = [p["norm_a"], p["w_in_a"], p["conv_w"], p["conv_b"], p["w_gate"], p["b_gx"],
               p["b_ga"], p["lam"], p["w_out_a"], p["norm_kv"], p["w_kv"], p["k_norm2"]]
    w_specs = [pl.BlockSpec(w.shape, const3 if w.ndim == 3 else const2) for w in weights]
    conv_spec = pl.BlockSpec((CONV_W - 1, NSEQ, D_RNN), lambda g, t: (0, g, 0))
    h_spec = pl.BlockSpec((NSEQ, D_RNN), lambda g, t: (g, 0))

    out_shape = [
        jax.ShapeDtypeStruct((n_seq, t_len, D_MODEL), F32),
        jax.ShapeDtypeStruct((n_seq, t_len, KV_W), F32),
        jax.ShapeDtypeStruct((n_seq, t_len, KV_W), F32),
        jax.ShapeDtypeStruct((CONV_W - 1, n_seq, D_RNN), F32),
        jax.ShapeDtypeStruct((n_seq, D_RNN), F32),
    ]
    scratch = [
        pltpu.VMEM((2, A_STEPS, NSEQ, D_MODEL), F32),
        pltpu.VMEM((2, A_STEPS, NSEQ, D_MODEL), F32),
        pltpu.VMEM((2, A_STEPS, NSEQ, KV_W), F32),
        pltpu.VMEM((2, A_STEPS, NSEQ, KV_W), F32),
        pltpu.SemaphoreType.DMA((2,)),
        pltpu.SemaphoreType.DMA((2,)),
        pltpu.VMEM((A_ROWS, D_MODEL), BF16),
        pltpu.VMEM((CONV_PRE + A_ROWS, D_RNN), F32),
        pltpu.VMEM((A_ROWS, D_RNN), F32),
        pltpu.VMEM((A_ROWS, D_RNN), F32),
        pltpu.VMEM((A_ROWS, D_RNN), BF16),
        pltpu.VMEM((A_ROWS, D_RNN), F32),
        pltpu.VMEM((A_ROWS, D_RNN), F32),
        pltpu.VMEM((A_ROWS, D_RNN), BF16),
        pltpu.VMEM((NSEQ, D_RNN), F32),
    ]
    return pl.pallas_call(
        _layer_a_kernel,
        grid=grid,
        in_specs=[any_spec, conv_spec, h_spec] + w_specs,
        out_specs=[any_spec, any_spec, any_spec, conv_spec, h_spec],
        out_shape=out_shape,
        scratch_shapes=scratch,
        compiler_params=pltpu.CompilerParams(
            dimension_semantics=("arbitrary", "arbitrary"), vmem_limit_bytes=VMEM_LIMIT_BYTES),
        name=name,
    )(x, conv_in, h_in, *weights)


def _layer_b_kernel(x1_ref, kh_ref, kc_ref, vh_ref, vc_ref, norm_b_ref, w_in_ref, hsel_ref,
                    hexp_ref, qg_ref, sinks_ref, w_out_ref, y_ref,
                    xn_scr, q_scr, gate_scr, qh_scr, ke_scr, ko_scr, ve_scr, vo_scr, o_scr,
                    *, nseg, seg, mask_first_halo):
    m = nseg * seg
    n_chunks = seg // CHUNK

    _norm_rows_to(x1_ref, norm_b_ref, xn_scr, m, 32)
    for n in range(D_MODEL // MXU_N):
        cols = slice(n * MXU_N, (n + 1) * MXU_N)
        q_scr[:, cols] = jnp.dot(xn_scr[...], w_in_ref[:, cols], preferred_element_type=F32)
        gate_scr[:, cols] = jnp.dot(
            xn_scr[...], w_in_ref[:, D_MODEL + n * MXU_N:D_MODEL + (n + 1) * MXU_N],
            preferred_element_type=F32)

    q = q_scr[...]
    ms = jnp.dot((q * q).astype(BF16), hsel_ref[...], preferred_element_type=F32)
    rs = lax.rsqrt(ms + EPS)
    rs_hi = rs.astype(BF16)
    rs_lo = (rs - rs_hi.astype(F32)).astype(BF16)
    rs_full = jnp.dot(jnp.concatenate([rs_hi, rs_lo], axis=1), hexp_ref[...],
                      preferred_element_type=F32)
    qh_scr[...] = (q * rs_full * qg_ref[...]).astype(BF16)

    lane = lax.broadcasted_iota(jnp.int32, (CHUNK, KV_W), 1)
    lo = lane < HEAD_DIM
    first = pl.program_id(1) == 0 if mask_first_halo else None

    for g in range(N_KV):
        for s in range(nseg):
            for src, e_scr, o_scr2 in ((0, ke_scr, ko_scr), (1, ve_scr, vo_scr)):
                h_ref, c_ref = (kh_ref, kc_ref) if src == 0 else (vh_ref, vc_ref)
                for r0, nrows, val in ((0, WINDOW, h_ref[s]),
                                       (WINDOW, seg, c_ref[s * seg:(s + 1) * seg, :])):
                    lane_kv = lax.broadcasted_iota(jnp.int32, val.shape, 1)
                    is_lo = lane_kv < HEAD_DIM
                    swapped = pltpu.roll(val, HEAD_DIM, 1)
                    low = val if g == 0 else swapped
                    high = swapped if g == 0 else val
                    e_scr[s, r0:r0 + nrows, :] = jnp.where(is_lo, low, 0.0).astype(BF16)
                    o_scr2[s, r0:r0 + nrows, :] = jnp.where(is_lo, 0.0, high).astype(BF16)

        for s in range(nseg):
            for c in range(n_chunks):
                q0 = s * seg + c * CHUNK
                k0 = c * CHUNK
                qst = jnp.concatenate(
                    [qh_scr[q0:q0 + CHUNK, g * GROUP * HEAD_DIM + j * PAIR_W:
                            g * GROUP * HEAD_DIM + (j + 1) * PAIR_W] for j in range(N_PAIRS)],
                    axis=0)
                kmat = jnp.concatenate(
                    [ke_scr[s, k0:k0 + WINDOW, :], ko_scr[s, k0:k0 + WINDOW, :],
                     ke_scr[s, k0 + WINDOW:k0 + WINDOW + CHUNK, :],
                     ko_scr[s, k0 + WINDOW:k0 + WINDOW + CHUNK, :]], axis=0)
                vmat = jnp.concatenate(
                    [ve_scr[s, k0:k0 + WINDOW, :], vo_scr[s, k0:k0 + WINDOW, :],
                     ve_scr[s, k0 + WINDOW:k0 + WINDOW + CHUNK, :],
                     vo_scr[s, k0 + WINDOW:k0 + WINDOW + CHUNK, :]], axis=0)
                sc = lax.dot_general(qst, kmat, (((1,), (1,)), ((), ())),
                                     preferred_element_type=F32)
                p_rows, scale_rows = [], []
                for j in range(N_PAIRS):
                    sj = sc[j * CHUNK:(j + 1) * CHUNK]
                    c0 = sj[:, 0:KV_W]
                    c1 = sj[:, KV_W:2 * KV_W]
                    c2 = sj[:, 2 * KV_W:]
                    if mask_first_halo and c * CHUNK < WINDOW:
                        n_bad = jnp.where(first, WINDOW - c * CHUNK, 0)
                        bad = lane < n_bad
                        c0 = jnp.where(bad, NEG, c0)
                        c1 = jnp.where(bad, NEG, c1)
                    sink_e = sinks_ref[g * GROUP + 2 * j]
                    sink_o = sinks_ref[g * GROUP + 2 * j + 1]
                    m_e = jnp.maximum(jnp.max(jnp.maximum(c0, jnp.where(lo, c2, NEG)),
                                              axis=-1, keepdims=True), sink_e)
                    m_o = jnp.maximum(jnp.max(jnp.maximum(c1, jnp.where(lo, NEG, c2)),
                                              axis=-1, keepdims=True), sink_o)
                    p0 = jnp.exp(c0 - m_e)
                    p1 = jnp.exp(c1 - m_o)
                    p2 = jnp.exp(c2 - jnp.where(lo, m_e, m_o))
                    d_e = jnp.sum(p0 + jnp.where(lo, p2, 0.0), axis=-1, keepdims=True) \
                        + jnp.exp(sink_e - m_e)
                    d_o = jnp.sum(p1 + jnp.where(lo, 0.0, p2), axis=-1, keepdims=True) \
                        + jnp.exp(sink_o - m_o)
                    p_rows.append(jnp.concatenate([p0, p1, p2], axis=1).astype(BF16))
                    scale_rows.append(jnp.where(lo, 1.0 / d_e, 1.0 / d_o))
                pmat = jnp.concatenate(p_rows, axis=0)
                o = jnp.dot(pmat, vmat, preferred_element_type=F32)
                for j in range(N_PAIRS):
                    c_lo = g * GROUP * HEAD_DIM + j * PAIR_W
                    o_scr[q0:q0 + CHUNK, c_lo:c_lo + PAIR_W] = (
                        o[j * CHUNK:(j + 1) * CHUNK] * scale_rows[j])

    blk = 32

    def gate_body(i, c):
        r = pl.multiple_of(i * blk, blk)
        xn_scr[pl.ds(r, blk), :] = (
            o_scr[pl.ds(r, blk), :] * _silu(gate_scr[pl.ds(r, blk), :])).astype(BF16)
        return c

    lax.fori_loop(0, m // blk, gate_body, 0)
    for n in range(D_MODEL // MXU_N):
        cols = slice(n * MXU_N, (n + 1) * MXU_N)
        y_ref[:, cols] = x1_ref[:, cols] + jnp.dot(
            xn_scr[...], w_out_ref[:, cols], preferred_element_type=F32)


def _layer_b(x1, k_halo, v_halo, k_cur, v_cur, p, *, n_batch, t_len, nseg, seg, prompt):
    m = nseg * seg
    rows = n_batch * t_len
    const2 = lambda *_: (0, 0)
    if prompt:
        steps = t_len // m
        per_b = t_len // WINDOW
        grid = (n_batch, steps)
        row_map = lambda b, t: (b * steps + t, 0)
        halo_map = lambda b, t: (b * per_b + jnp.maximum(t * (m // WINDOW) - 1, 0), 0, 0)
    else:
        grid = (n_batch // nseg, 1)
        row_map = lambda i, t: (i, 0)
        halo_map = lambda i, t: (i, 0, 0)

    weights = [p["norm_b"], p["w_in_b"], p["hsel"], p["hexp"], p["qg"]]
    in_specs = [
        pl.BlockSpec((m, D_MODEL), row_map),
        pl.BlockSpec((nseg, WINDOW, KV_W), halo_map),
        pl.BlockSpec((m, KV_W), row_map),
        pl.BlockSpec((nseg, WINDOW, KV_W), halo_map),
        pl.BlockSpec((m, KV_W), row_map),
    ] + [pl.BlockSpec(w.shape, const2) for w in weights] + [
        pl.BlockSpec(memory_space=pltpu.SMEM),
        pl.BlockSpec(p["w_out_b"].shape, const2),
    ]
    kv_rows = WINDOW + seg
    scratch = [
        pltpu.VMEM((m, D_MODEL), BF16),
        pltpu.VMEM((m, D_MODEL), F32),
        pltpu.VMEM((m, D_MODEL), F32),
        pltpu.VMEM((m, D_MODEL), BF16),
        pltpu.VMEM((nseg, kv_rows, KV_W), BF16),
        pltpu.VMEM((nseg, kv_rows, KV_W), BF16),
        pltpu.VMEM((nseg, kv_rows, KV_W), BF16),
        pltpu.VMEM((nseg, kv_rows, KV_W), BF16),
        pltpu.VMEM((m, D_MODEL), F32),
    ]
    return pl.pallas_call(
        functools.partial(_layer_b_kernel, nseg=nseg, seg=seg, mask_first_halo=prompt),
        grid=grid,
        in_specs=in_specs,
        out_specs=pl.BlockSpec((m, D_MODEL), row_map),
        out_shape=jax.ShapeDtypeStruct((rows, D_MODEL), F32),
        scratch_shapes=scratch,
        compiler_params=pltpu.CompilerParams(
            dimension_semantics=("arbitrary", "arbitrary"), vmem_limit_bytes=VMEM_LIMIT_BYTES),
        name="layer_b_prompt" if prompt else "layer_b_sample",
    )(x1, k_halo, k_cur, v_halo, v_cur, *weights, p["sinks"], p["w_out_b"])


def _prep_params(norm_a, w_in_a, conv_w, conv_b, w_gate_x, b_gate_x, w_gate_a, b_gate_a,
                 lru_lambda, w_out_a, norm_kv, w_kv, k_norm, norm_b, w_in_b, q_norm, sinks,
                 w_out_b):
    row = lambda v: v.reshape(1, -1).astype(F32)
    head_of_col = jnp.arange(D_MODEL) // HEAD_DIM
    hsel = (head_of_col[:, None] == jnp.arange(LANES)[None, :]).astype(F32) * (1.0 / HEAD_DIM)
    hexp = (jnp.arange(2 * LANES)[:, None] % LANES == head_of_col[None, :]).astype(BF16)
    return {
        "norm_a": row(norm_a[0]),
        "w_in_a": w_in_a[0].astype(BF16),
        "conv_w": conv_w[0].astype(F32),
        "conv_b": row(conv_b[0]),
        "w_gate": jnp.concatenate([w_gate_x[0], w_gate_a[0]], axis=-1).astype(BF16),
        "b_gx": row(b_gate_x[0]),
        "b_ga": row(b_gate_a[0]),
        "lam": row(lru_lambda[0]),
        "w_out_a": w_out_a[0].astype(BF16),
        "norm_kv": row(norm_kv),
        "w_kv": w_kv.astype(BF16),
        "k_norm2": row(jnp.tile(k_norm, N_KV)),
        "norm_b": row(norm_b[0]),
        "w_in_b": w_in_b[0].astype(BF16),
        "hsel": hsel.astype(BF16),
        "hexp": hexp,
        "qg": row(jnp.tile(q_norm[0], N_HEADS) * (HEAD_DIM ** -0.5)),
        "sinks": sinks[0].astype(F32),
        "w_out_b": w_out_b[0].astype(BF16),
    }


def kernel(x_prompt, x_sample, state_conv, state_rglru, cache_k_win, cache_v_win, norm_a, w_in_a, conv_w, conv_b, w_gate_x, b_gate_x, w_gate_a, b_gate_a, lru_lambda, w_out_a, norm_kv, w_kv, k_norm, norm_b, w_in_b, q_norm, sinks, w_out_b):
    assert norm_a.shape[0] == 1 and norm_b.shape[0] == 1, "one recurrent + one attention layer"
    p = _prep_params(norm_a, w_in_a, conv_w, conv_b, w_gate_x, b_gate_x, w_gate_a, b_gate_a,
                     lru_lambda, w_out_a, norm_kv, w_kv, k_norm, norm_b, w_in_b, q_norm, sinks,
                     w_out_b)
    bp, tp, _ = x_prompt.shape
    bs, ts, _ = x_sample.shape

    x1p, kp, vp, conv_p, h_p = _layer_a(
        x_prompt, jnp.zeros((CONV_W - 1, bp, D_RNN), F32), jnp.zeros((bp, D_RNN), F32), p,
        "layer_a_prompt")
    tile_b = 256
    kp2 = kp.reshape(bp * tp, KV_W)
    vp2 = vp.reshape(bp * tp, KV_W)
    kp3 = kp.reshape(bp * tp // WINDOW, WINDOW, KV_W)
    vp3 = vp.reshape(bp * tp // WINDOW, WINDOW, KV_W)
    y_p = _layer_b(x1p.reshape(bp * tp, D_MODEL), kp3, vp3, kp2, vp2, p, n_batch=bp, t_len=tp,
                   nseg=1, seg=tile_b, prompt=True)

    x1s, ks, vs, conv_s, h_s = _layer_a(
        x_sample, jnp.transpose(state_conv[0], (1, 0, 2)), state_rglru[0], p, "layer_a_sample")
    y_s = _layer_b(x1s.reshape(bs * ts, D_MODEL), cache_k_win.reshape(bs, WINDOW, KV_W),
                   cache_v_win.reshape(bs, WINDOW, KV_W), ks.reshape(bs * ts, KV_W),
                   vs.reshape(bs * ts, KV_W), p, n_batch=bs, t_len=ts, nseg=4, seg=ts,
                   prompt=False)

    kp4 = kp.reshape(bp, tp, N_KV, HEAD_DIM)[:, -WINDOW:]
    vp4 = vp.reshape(bp, tp, N_KV, HEAD_DIM)[:, -WINDOW:]
    ks4 = jnp.concatenate([cache_k_win, ks.reshape(bs, ts, N_KV, HEAD_DIM)], axis=1)[:, -WINDOW:]
    vs4 = jnp.concatenate([cache_v_win, vs.reshape(bs, ts, N_KV, HEAD_DIM)], axis=1)[:, -WINDOW:]
    return (y_p.reshape(bp, tp, D_MODEL), y_s.reshape(bs, ts, D_MODEL),
            jnp.transpose(conv_p, (1, 0, 2))[None], h_p[None], kp4, vp4,
            jnp.transpose(conv_s, (1, 0, 2))[None], h_s[None], ks4, vs4)
```

```python
import functools

import jax
import jax.numpy as jnp
from jax import lax
from jax.experimental import pallas as pl
from jax.experimental.pallas import tpu as pltpu

D_MODEL = 1024
D_RNN = 1024
GATE_BLOCK = 256
N_GATE_BLOCKS = D_RNN // GATE_BLOCK
CONV_W = 4
LRU_C = 8.0
HEAD_DIM = 64
N_HEADS = 16
N_KV = 2
GROUP = N_HEADS // N_KV
CHUNK = 64
WINDOW = 128
EPS = 1e-6
NEG = -1e30
LOG2_E = 1.4426950408889634
LN_2 = 0.6931471805599453

SUBLANES = 8
LANES = 128
BF16_SUBLANES = 16
MXU_N = 256
VMEM_LIMIT_BYTES = 56 * 1024 * 1024

F32 = jnp.float32
BF16 = jnp.bfloat16

KV_W = N_KV * HEAD_DIM
PAIR_W = 2 * HEAD_DIM
N_PAIRS = GROUP // 2

NSEQ = SUBLANES
A_STEPS = 64
A_ROWS = A_STEPS * NSEQ
CONV_PRE = (CONV_W - 1) * NSEQ


def _silu_of_half(h):
    return h * jnp.tanh(h) + h


def _norm_rows_to(x_ref, g_ref, dst_ref, n_rows, blk):
    g = g_ref[...]
    n_blk = n_rows // blk

    def inv_rms(i):
        r = pl.multiple_of(i * blk, blk)
        x = x_ref[pl.ds(r, blk), :]
        return lax.rsqrt(jnp.mean(x * x, axis=-1, keepdims=True) + EPS)

    def body(i, rs):
        rs_next = inv_rms(jnp.minimum(i + 1, n_blk - 1))
        r = pl.multiple_of(i * blk, blk)
        dst_ref[pl.ds(r, blk), :] = (x_ref[pl.ds(r, blk), :] * rs * g).astype(BF16)
        return rs_next

    lax.fori_loop(0, n_blk, body, inv_rms(0), unroll=True)


def _layer_a_kernel(x_hbm, conv_in_ref, h_in_ref,
                    norm_a_ref, w_in_ref, conv_w_ref, conv_b_ref, w_gate_ref, b_gx_ref, b_ga_ref,
                    lam_ref, w_out_ref, norm_kv_ref, w_kv_ref, k_norm_ref,
                    x1_hbm, k_hbm, v_hbm, conv_out_ref, h_out_ref,
                    xin, x1o, ko, vo, sem_in, sem_out,
                    xn_scr, xbp_scr, gate_scr, xc_scr, xcb_scr, gx_scr, ga_scr, hy_scr, h_scr):
    g = pl.program_id(0)
    t = pl.program_id(1)
    nt = pl.num_programs(1)
    total = pl.num_programs(0) * nt
    i = g * nt + t
    slot = lax.rem(i, 2)

    def in_copies(step, sl):
        gg = step // nt
        tt = lax.rem(step, nt)
        return [pltpu.make_async_copy(x_hbm.at[gg * NSEQ + j, pl.ds(tt * A_STEPS, A_STEPS), :],
                                      xin.at[sl, :, j, :], sem_in.at[sl]) for j in range(NSEQ)]

    def out_copies(step, sl):
        gg = step // nt
        tt = lax.rem(step, nt)
        cps = []
        for buf, dst in ((x1o, x1_hbm), (ko, k_hbm), (vo, v_hbm)):
            cps += [pltpu.make_async_copy(buf.at[sl, :, j, :],
                                          dst.at[gg * NSEQ + j, pl.ds(tt * A_STEPS, A_STEPS), :],
                                          sem_out.at[sl]) for j in range(NSEQ)]
        return cps

    @pl.when(i == 0)
    def _():
        for c in in_copies(i, slot):
            c.start()

    @pl.when(i + 1 < total)
    def _():
        for c in in_copies(i + 1, 1 - slot):
            c.start()

    @pl.when(t == 0)
    def _():
        xbp_scr[0:CONV_PRE, :] = conv_in_ref[...].reshape(CONV_PRE, D_RNN)
        h_scr[...] = h_in_ref[...]

    @pl.when(i >= 2)
    def _():
        for c in out_copies(i - 2, slot):
            c.wait()

    for c in in_copies(i, slot):
        c.wait()

    x_ref = xin.at[slot].reshape(A_ROWS, D_MODEL)
    x1_ref = x1o.at[slot].reshape(A_ROWS, D_MODEL)
    k_ref = ko.at[slot].reshape(A_ROWS, KV_W)
    v_ref = vo.at[slot].reshape(A_ROWS, KV_W)

    _norm_rows_to(x_ref, norm_a_ref, xn_scr, A_ROWS, 128)
    lam = lam_ref[...]
    log_sig_lam = jnp.minimum(lam, 0.0) - jnp.log1p(jnp.exp(-jnp.abs(lam)))
    kk = jnp.broadcast_to((0.5 * LRU_C * LOG2_E) * log_sig_lam, (NSEQ, D_RNN))
    b_gx = jnp.broadcast_to(b_gx_ref[...], (NSEQ, D_RNN))
    b_ga = jnp.broadcast_to(b_ga_ref[...], (NSEQ, D_RNN))

    cw = conv_w_ref[...]
    cb = conv_b_ref[...]
    conv_blk = 32

    for n in range(N_GATE_BLOCKS):
        cols = slice(n * GATE_BLOCK, (n + 1) * GATE_BLOCK)
        xbp_scr[CONV_PRE:, cols] = jnp.dot(xn_scr[...], w_in_ref[:, cols],
                                           preferred_element_type=F32)
        gate_scr[:, cols] = jnp.dot(
            xn_scr[...], w_in_ref[:, D_RNN + n * GATE_BLOCK:D_RNN + (n + 1) * GATE_BLOCK],
            preferred_element_type=F32)

        for r in range(0, A_ROWS, conv_blk):
            acc = cb[:, cols]
            for j in range(CONV_W):
                acc = acc + cw[j:j + 1, cols] * xbp_scr[r + j * NSEQ:r + j * NSEQ + conv_blk, cols]
            xc_scr[r:r + conv_blk, cols] = acc
            xcb_scr[r:r + conv_blk, cols] = acc.astype(BF16)
        tail = xbp_scr[A_ROWS:A_ROWS + CONV_PRE, cols]
        conv_out_ref[:, :, cols] = tail.reshape(CONV_W - 1, NSEQ, GATE_BLOCK)
        xbp_scr[0:CONV_PRE, cols] = tail

        res = jnp.dot(xcb_scr[:, cols], w_gate_ref[n], preferred_element_type=F32)
        gx_scr[:, cols] = res[:, :GATE_BLOCK]
        ga_scr[:, cols] = res[:, GATE_BLOCK:]

        kk_n = kk[:, cols]
        bgx_n = b_gx[:, cols]
        bga_n = b_ga[:, cols]
        h = h_scr[:, cols]
        for r in range(0, A_ROWS, BF16_SUBLANES):
            hy = []
            for rr in (r, r + NSEQ):
                t_i = jnp.tanh(gx_scr[rr:rr + NSEQ, cols] + bgx_n)
                t_r = jnp.tanh(ga_scr[rr:rr + NSEQ, cols] + bga_n)
                log2_a = t_r * kk_n + kk_n
                a = jnp.exp2(log2_a)
                y = jnp.tanh(log2_a * (-LN_2)) * (a * a + 1.0)
                mult = jnp.where(y > 0.0, y * lax.rsqrt(y), 0.0)
                u = mult * xc_scr[rr:rr + NSEQ, cols]
                h = a * h + u * (0.5 * t_i + 0.5)
                hy.append(h * _silu_of_half(gate_scr[rr:rr + NSEQ, cols]))
            hy_scr[r:r + BF16_SUBLANES, cols] = jnp.concatenate(hy, axis=0).astype(BF16)
        h_scr[:, cols] = h
        h_out_ref[:, cols] = h

    for n in range(D_MODEL // MXU_N):
        cols = slice(n * MXU_N, (n + 1) * MXU_N)
        x1_ref[:, cols] = x_ref[:, cols] + jnp.dot(
            hy_scr[...], w_out_ref[:, cols], preferred_element_type=F32)

    _norm_rows_to(x1_ref, norm_kv_ref, xn_scr, A_ROWS, 128)
    kv = jnp.dot(xn_scr[...], w_kv_ref[...], preferred_element_type=F32)
    k = kv[:, :KV_W]
    lo = lax.broadcasted_iota(jnp.int32, (A_ROWS, KV_W), 1) < HEAD_DIM
    k2 = k * k
    ms_lo = jnp.sum(jnp.where(lo, k2, 0.0), axis=-1, keepdims=True) * (1.0 / HEAD_DIM)
    ms_hi = jnp.sum(jnp.where(lo, 0.0, k2), axis=-1, keepdims=True) * (1.0 / HEAD_DIM)
    rs = jnp.where(lo, lax.rsqrt(ms_lo + EPS), lax.rsqrt(ms_hi + EPS))
    k_ref[...] = k * rs * k_norm_ref[...]
    v_ref[...] = kv[:, KV_W:]

    for c in out_copies(i, slot):
        c.start()

    @pl.when(i == total - 1)
    def _():
        @pl.when(i >= 1)
        def _():
            for c in out_copies(i - 1, 1 - slot):
                c.wait()
        for c in out_copies(i, slot):
            c.wait()


def _layer_a(x, conv_in, h_in, p, name):
    n_seq, t_len, _ = x.shape
    assert n_seq % NSEQ == 0 and t_len % A_STEPS == 0
    grid = (n_seq // NSEQ, t_len // A_STEPS)
    const2 = lambda g, t: (0, 0)
    const3 = lambda g, t: (0, 0, 0)
    any_spec = pl.BlockSpec(memory_space=pl.ANY)

    weights = [p["norm_a"], p["w_in_a"], p["conv_w"], p["conv_b"], p["w_gate"], p["b_gx"],
               p["b_ga"], p["lam"], p["w_out_a"], p["norm_kv"], p["w_kv"], p["k_norm2"]]
    w_specs = [pl.BlockSpec(w.shape, const3 if w.ndim == 3 else const2) for w in weights]
    conv_spec = pl.BlockSpec((CONV_W - 1, NSEQ, D_RNN), lambda g, t: (0, g, 0))
    h_spec = pl.BlockSpec((NSEQ, D_RNN), lambda g, t: (g, 0))

    out_shape = [
        jax.ShapeDtypeStruct((n_seq, t_len, D_MODEL), F32),
        jax.ShapeDtypeStruct((n_seq, t_len, KV_W), F32),
        jax.ShapeDtypeStruct((n_seq, t_len, KV_W), F32),
        jax.ShapeDtypeStruct((CONV_W - 1, n_seq, D_RNN), F32),
        jax.ShapeDtypeStruct((n_seq, D_RNN), F32),
    ]
    scratch = [
        pltpu.VMEM((2, A_STEPS, NSEQ, D_MODEL), F32),
        pltpu.VMEM((2, A_STEPS, NSEQ, D_MODEL), F32),
        pltpu.VMEM((2, A_STEPS, NSEQ, KV_W), F32),
        pltpu.VMEM((2, A_STEPS, NSEQ, KV_W), F32),
        pltpu.SemaphoreType.DMA((2,)),
        pltpu.SemaphoreType.DMA((2,)),
        pltpu.VMEM((A_ROWS, D_MODEL), BF16),
        pltpu.VMEM((CONV_PRE + A_ROWS, D_RNN), F32),
        pltpu.VMEM((A_ROWS, D_RNN), F32),
        pltpu.VMEM((A_ROWS, D_RNN), F32),
        pltpu.VMEM((A_ROWS, D_RNN), BF16),
        pltpu.VMEM((A_ROWS, D_RNN), F32),
        pltpu.VMEM((A_ROWS, D_RNN), F32),
        pltpu.VMEM((A_ROWS, D_RNN), BF16),
        pltpu.VMEM((NSEQ, D_RNN), F32),
    ]
    return pl.pallas_call(
        _layer_a_kernel,
        grid=grid,
        in_specs=[any_spec, conv_spec, h_spec] + w_specs,
        out_specs=[any_spec, any_spec, any_spec, conv_spec, h_spec],
        out_shape=out_shape,
        scratch_shapes=scratch,
        compiler_params=pltpu.CompilerParams(
            dimension_semantics=("arbitrary", "arbitrary"), vmem_limit_bytes=VMEM_LIMIT_BYTES),
        name=name,
    )(x, conv_in, h_in, *weights)


def _layer_b_kernel(x1_ref, kh_ref, kc_ref, vh_ref, vc_ref, norm_b_ref, w_in_ref, hsel_ref,
                    hexp_ref, qg_ref, sinks_ref, w_out_ref, y_ref,
                    xn_scr, q_scr, gate_scr, qh_scr, ke_scr, ko_scr, ve_scr, vo_scr, o_scr,
                    *, nseg, seg, mask_first_halo):
    m = nseg * seg
    n_chunks = seg // CHUNK

    _norm_rows_to(x1_ref, norm_b_ref, xn_scr, m, 128)
    for n in range(D_MODEL // MXU_N):
        cols = slice(n * MXU_N, (n + 1) * MXU_N)
        q_scr[:, cols] = jnp.dot(xn_scr[...], w_in_ref[:, cols], preferred_element_type=F32)
        gate_scr[:, cols] = jnp.dot(
            xn_scr[...], w_in_ref[:, D_MODEL + n * MXU_N:D_MODEL + (n + 1) * MXU_N],
            preferred_element_type=F32)

    q = q_scr[...]
    ms = jnp.dot((q * q).astype(BF16), hsel_ref[...], preferred_element_type=F32)
    rs = lax.rsqrt(ms + EPS)
    rs_hi = rs.astype(BF16)
    rs_lo = (rs - rs_hi.astype(F32)).astype(BF16)
    rs_full = jnp.dot(jnp.concatenate([rs_hi, rs_lo], axis=1), hexp_ref[...],
                      preferred_element_type=F32)
    qh_scr[...] = (q * rs_full * qg_ref[...]).astype(BF16)

    lane = lax.broadcasted_iota(jnp.int32, (CHUNK, KV_W), 1)
    lo = lane < HEAD_DIM
    first = pl.program_id(1) == 0 if mask_first_halo else None

    for g in range(N_KV):
        for s in range(nseg):
            for src, e_scr, o_scr2 in ((0, ke_scr, ko_scr), (1, ve_scr, vo_scr)):
                h_ref, c_ref = (kh_ref, kc_ref) if src == 0 else (vh_ref, vc_ref)
                for r0, nrows, val in ((0, WINDOW, h_ref[s]),
                                       (WINDOW, seg, c_ref[s * seg:(s + 1) * seg, :])):
                    lane_kv = lax.broadcasted_iota(jnp.int32, val.shape, 1)
                    is_lo = lane_kv < HEAD_DIM
                    swapped = pltpu.roll(val, HEAD_DIM, 1)
                    low = val if g == 0 else swapped
                    high = swapped if g == 0 else val
                    e_scr[s, r0:r0 + nrows, :] = jnp.where(is_lo, low, 0.0).astype(BF16)
                    o_scr2[s, r0:r0 + nrows, :] = jnp.where(is_lo, 0.0, high).astype(BF16)

        for s in range(nseg):
            for c in range(n_chunks):
                q0 = s * seg + c * CHUNK
                k0 = c * CHUNK
                qst = jnp.concatenate(
                    [qh_scr[q0:q0 + CHUNK, g * GROUP * HEAD_DIM + j * PAIR_W:
                            g * GROUP * HEAD_DIM + (j + 1) * PAIR_W] for j in range(N_PAIRS)],
                    axis=0)
                kmat = jnp.concatenate(
                    [ke_scr[s, k0:k0 + WINDOW, :], ko_scr[s, k0:k0 + WINDOW, :],
                     ke_scr[s, k0 + WINDOW:k0 + WINDOW + CHUNK, :],
                     ko_scr[s, k0 + WINDOW:k0 + WINDOW + CHUNK, :]], axis=0)
                vmat = jnp.concatenate(
                    [ve_scr[s, k0:k0 + WINDOW, :], vo_scr[s, k0:k0 + WINDOW, :],
                     ve_scr[s, k0 + WINDOW:k0 + WINDOW + CHUNK, :],
                     vo_scr[s, k0 + WINDOW:k0 + WINDOW + CHUNK, :]], axis=0)
                sc = lax.dot_general(qst, kmat, (((1,), (1,)), ((), ())),
                                     preferred_element_type=F32)
                p_rows, scale_rows = [], []
                for j in range(N_PAIRS):
                    sj = sc[j * CHUNK:(j + 1) * CHUNK]
                    c0 = sj[:, 0:KV_W]
                    c1 = sj[:, KV_W:2 * KV_W]
                    c2 = sj[:, 2 * KV_W:]
                    if mask_first_halo and c * CHUNK < WINDOW:
                        n_bad = jnp.where(first, WINDOW - c * CHUNK, 0)
                        bad = lane < n_bad
                        c0 = jnp.where(bad, NEG, c0)
                        c1 = jnp.where(bad, NEG, c1)
                    sink_e = sinks_ref[g * GROUP + 2 * j]
                    sink_o = sinks_ref[g * GROUP + 2 * j + 1]
                    m_e = jnp.maximum(jnp.max(jnp.maximum(c0, jnp.where(lo, c2, NEG)),
                                              axis=-1, keepdims=True), sink_e)
                    m_o = jnp.maximum(jnp.max(jnp.maximum(c1, jnp.where(lo, NEG, c2)),
                                              axis=-1, keepdims=True), sink_o)
                    p0 = jnp.exp(c0 - m_e)
                    p1 = jnp.exp(c1 - m_o)
                    p2 = jnp.exp(c2 - jnp.where(lo, m_e, m_o))
                    d_e = jnp.sum(p0 + jnp.where(lo, p2, 0.0), axis=-1, keepdims=True) \
                        + jnp.exp(sink_e - m_e)
                    d_o = jnp.sum(p1 + jnp.where(lo, 0.0, p2), axis=-1, keepdims=True) \
                        + jnp.exp(sink_o - m_o)
                    p_rows.append(jnp.concatenate([p0, p1, p2], axis=1).astype(BF16))
                    scale_rows.append(jnp.where(lo, 1.0 / d_e, 1.0 / d_o))
                pmat = jnp.concatenate(p_rows, axis=0)
                o = jnp.dot(pmat, vmat, preferred_element_type=F32)
                for j in range(N_PAIRS):
                    c_lo = g * GROUP * HEAD_DIM + j * PAIR_W
                    o_scr[q0:q0 + CHUNK, c_lo:c_lo + PAIR_W] = (
                        o[j * CHUNK:(j + 1) * CHUNK] * scale_rows[j])

    blk = 32

    def gate_body(i, c):
        r = pl.multiple_of(i * blk, blk)
        xn_scr[pl.ds(r, blk), :] = (
            o_scr[pl.ds(r, blk), :] * _silu_of_half(gate_scr[pl.ds(r, blk), :])).astype(BF16)
        return c

    lax.fori_loop(0, m // blk, gate_body, 0)
    for n in range(D_MODEL // MXU_N):
        cols = slice(n * MXU_N, (n + 1) * MXU_N)
        y_ref[:, cols] = x1_ref[:, cols] + jnp.dot(
            xn_scr[...], w_out_ref[:, cols], preferred_element_type=F32)


def _layer_b(x1, k_halo, v_halo, k_cur, v_cur, p, *, n_batch, t_len, nseg, seg, prompt):
    m = nseg * seg
    rows = n_batch * t_len
    const2 = lambda *_: (0, 0)
    if prompt:
        steps = t_len // m
        per_b = t_len // WINDOW
        grid = (n_batch, steps)
        row_map = lambda b, t: (b * steps + t, 0)
        halo_map = lambda b, t: (b * per_b + jnp.maximum(t * (m // WINDOW) - 1, 0), 0, 0)
    else:
        grid = (n_batch // nseg, 1)
        row_map = lambda i, t: (i, 0)
        halo_map = lambda i, t: (i, 0, 0)

    weights = [p["norm_b"], p["w_in_b"], p["hsel"], p["hexp"], p["qg"]]
    in_specs = [
        pl.BlockSpec((m, D_MODEL), row_map),
        pl.BlockSpec((nseg, WINDOW, KV_W), halo_map),
        pl.BlockSpec((m, KV_W), row_map),
        pl.BlockSpec((nseg, WINDOW, KV_W), halo_map),
        pl.BlockSpec((m, KV_W), row_map),
    ] + [pl.BlockSpec(w.shape, const2) for w in weights] + [
        pl.BlockSpec(memory_space=pltpu.SMEM),
        pl.BlockSpec(p["w_out_b"].shape, const2),
    ]
    kv_rows = WINDOW + seg
    scratch = [
        pltpu.VMEM((m, D_MODEL), BF16),
        pltpu.VMEM((m, D_MODEL), F32),
        pltpu.VMEM((m, D_MODEL), F32),
        pltpu.VMEM((m, D_MODEL), BF16),
        pltpu.VMEM((nseg, kv_rows, KV_W), BF16),
        pltpu.VMEM((nseg, kv_rows, KV_W), BF16),
        pltpu.VMEM((nseg, kv_rows, KV_W), BF16),
        pltpu.VMEM((nseg, kv_rows, KV_W), BF16),
        pltpu.VMEM((m, D_MODEL), F32),
    ]
    return pl.pallas_call(
        functools.partial(_layer_b_kernel, nseg=nseg, seg=seg, mask_first_halo=prompt),
        grid=grid,
        in_specs=in_specs,
        out_specs=pl.BlockSpec((m, D_MODEL), row_map),
        out_shape=jax.ShapeDtypeStruct((rows, D_MODEL), F32),
        scratch_shapes=scratch,
        compiler_params=pltpu.CompilerParams(
            dimension_semantics=("arbitrary", "arbitrary"), vmem_limit_bytes=VMEM_LIMIT_BYTES),
        name="layer_b_prompt" if prompt else "layer_b_sample",
    )(x1, k_halo, k_cur, v_halo, v_cur, *weights, p["sinks"], p["w_out_b"])


def _prep_params(norm_a, w_in_a, conv_w, conv_b, w_gate_x, b_gate_x, w_gate_a, b_gate_a,
                 lru_lambda, w_out_a, norm_kv, w_kv, k_norm, norm_b, w_in_b, q_norm, sinks,
                 w_out_b):
    row = lambda v: v.reshape(1, -1).astype(F32)
    in_scale = jnp.concatenate([jnp.ones((D_MODEL,), F32), jnp.full((D_MODEL,), 0.5, F32)])
    head_of_col = jnp.arange(D_MODEL) // HEAD_DIM
    hsel = (head_of_col[:, None] == jnp.arange(LANES)[None, :]).astype(F32) * (1.0 / HEAD_DIM)
    hexp = (jnp.arange(2 * LANES)[:, None] % LANES == head_of_col[None, :]).astype(BF16)
    return {
        "norm_a": row(norm_a[0]),
        "w_in_a": (w_in_a[0] * in_scale).astype(BF16),
        "conv_w": conv_w[0].astype(F32),
        "conv_b": row(conv_b[0]),
        "w_gate": (0.5 * jnp.concatenate([w_gate_x[0], w_gate_a[0]], axis=-1)).astype(BF16),
        "b_gx": row(0.5 * b_gate_x[0]),
        "b_ga": row(0.5 * b_gate_a[0]),
        "lam": row(lru_lambda[0]),
        "w_out_a": w_out_a[0].astype(BF16),
        "norm_kv": row(norm_kv),
        "w_kv": w_kv.astype(BF16),
        "k_norm2": row(jnp.tile(k_norm, N_KV)),
        "norm_b": row(norm_b[0]),
        "w_in_b": (w_in_b[0] * in_scale).astype(BF16),
        "hsel": hsel.astype(BF16),
        "hexp": hexp,
        "qg": row(jnp.tile(q_norm[0], N_HEADS) * (HEAD_DIM ** -0.5)),
        "sinks": sinks[0].astype(F32),
        "w_out_b": w_out_b[0].astype(BF16),
    }


def kernel(x_prompt, x_sample, state_conv, state_rglru, cache_k_win, cache_v_win, norm_a, w_in_a, conv_w, conv_b, w_gate_x, b_gate_x, w_gate_a, b_gate_a, lru_lambda, w_out_a, norm_kv, w_kv, k_norm, norm_b, w_in_b, q_norm, sinks, w_out_b):
    assert norm_a.shape[0] == 1 and norm_b.shape[0] == 1, "one recurrent + one attention layer"
    p = _prep_params(norm_a, w_in_a, conv_w, conv_b, w_gate_x, b_gate_x, w_gate_a, b_gate_a,
                     lru_lambda, w_out_a, norm_kv, w_kv, k_norm, norm_b, w_in_b, q_norm, sinks,
                     w_out_b)
    bp, tp, _ = x_prompt.shape
    bs, ts, _ = x_sample.shape

    x1p, kp, vp, conv_p, h_p = _layer_a(
        x_prompt, jnp.zeros((CONV_W - 1, bp, D_RNN), F32), jnp.zeros((bp, D_RNN), F32), p,
        "layer_a_prompt")
    tile_b = 256
    kp2 = kp.reshape(bp * tp, KV_W)
    vp2 = vp.reshape(bp * tp, KV_W)
    kp3 = kp.reshape(bp * tp // WINDOW, WINDOW, KV_W)
    vp3 = vp.reshape(bp * tp // WINDOW, WINDOW, KV_W)
    y_p = _layer_b(x1p.reshape(bp * tp, D_MODEL), kp3, vp3, kp2, vp2, p, n_batch=bp, t_len=tp,
                   nseg=1, seg=tile_b, prompt=True)

    x1s, ks, vs, conv_s, h_s = _layer_a(
        x_sample, jnp.transpose(state_conv[0], (1, 0, 2)), state_rglru[0], p, "layer_a_sample")
    y_s = _layer_b(x1s.reshape(bs * ts, D_MODEL), cache_k_win.reshape(bs, WINDOW, KV_W),
                   cache_v_win.reshape(bs, WINDOW, KV_W), ks.reshape(bs * ts, KV_W),
                   vs.reshape(bs * ts, KV_W), p, n_batch=bs, t_len=ts, nseg=4, seg=ts,
                   prompt=False)

    kp4 = kp.reshape(bp, tp, N_KV, HEAD_DIM)[:, -WINDOW:]
    vp4 = vp.reshape(bp, tp, N_KV, HEAD_DIM)[:, -WINDOW:]
    ks4 = jnp.concatenate([cache_k_win, ks.reshape(bs, ts, N_KV, HEAD_DIM)], axis=1)[:, -WINDOW:]
    vs4 = jnp.concatenate([cache_v_win, vs.reshape(bs, ts, N_KV, HEAD_DIM)], axis=1)[:, -WINDOW:]
    return (y_p.reshape(bp, tp, D_MODEL), y_s.reshape(bs, ts, D_MODEL),
            jnp.transpose(conv_p, (1, 0, 2))[None], h_p[None], kp4, vp4,
            jnp.transpose(conv_s, (1, 0, 2))[None], h_s[None], ks4, vs4)
```

```python
import functools

import jax
import jax.numpy as jnp
from jax import lax
from jax.experimental import pallas as pl
from jax.experimental.pallas import tpu as pltpu

D_MODEL = 1024
D_RNN = 1024
GATE_BLOCK = 256
N_GATE_BLOCKS = D_RNN // GATE_BLOCK
CONV_W = 4
LRU_C = 8.0
HEAD_DIM = 64
N_HEADS = 16
N_KV = 2
GROUP = N_HEADS // N_KV
CHUNK = 64
WINDOW = 128
EPS = 1e-6
NEG = -1e30
LOG2_E = 1.4426950408889634
LN_2 = 0.6931471805599453

SUBLANES = 8
LANES = 128
BF16_SUBLANES = 16
MXU_N = 256
VMEM_LIMIT_BYTES = 56 * 1024 * 1024

F32 = jnp.float32
BF16 = jnp.bfloat16

KV_W = N_KV * HEAD_DIM
PAIR_W = 2 * HEAD_DIM
N_PAIRS = GROUP // 2

NSEQ = SUBLANES
A_STEPS = 64
A_ROWS = A_STEPS * NSEQ
CONV_PRE = (CONV_W - 1) * NSEQ


def _silu_of_half(h):
    return h * jnp.tanh(h) + h


def _norm_rows_to(x_ref, g_ref, dst_ref, n_rows, blk):
    g = g_ref[...]
    n_blk = n_rows // blk

    def inv_rms(i):
        r = pl.multiple_of(i * blk, blk)
        x = x_ref[pl.ds(r, blk), :]
        return lax.rsqrt(jnp.mean(x * x, axis=-1, keepdims=True) + EPS)

    def body(i, rs):
        rs_next = inv_rms(jnp.minimum(i + 1, n_blk - 1))
        r = pl.multiple_of(i * blk, blk)
        dst_ref[pl.ds(r, blk), :] = (x_ref[pl.ds(r, blk), :] * rs * g).astype(BF16)
        return rs_next

    lax.fori_loop(0, n_blk, body, inv_rms(0), unroll=True)


def _layer_a_kernel(x_hbm, conv_in_ref, h_in_ref,
                    norm_a_ref, w_in_ref, conv_w_ref, conv_b_ref, w_gate_ref, b_gx_ref, b_ga_ref,
                    lam_ref, w_out_ref, norm_kv_ref, w_kv_ref, k_norm_ref,
                    x1_hbm, k_hbm, v_hbm, conv_out_ref, h_out_ref,
                    xin, x1o, ko, vo, sem_in, sem_out,
                    xn_scr, xbp_scr, gate_scr, xc_scr, xcb_scr, gx_scr, ga_scr, hy_scr, h_scr):
    g = pl.program_id(0)
    t = pl.program_id(1)
    nt = pl.num_programs(1)
    total = pl.num_programs(0) * nt
    i = g * nt + t
    slot = lax.rem(i, 2)

    def in_copies(step, sl):
        gg = step // nt
        tt = lax.rem(step, nt)
        return [pltpu.make_async_copy(x_hbm.at[gg * NSEQ + j, pl.ds(tt * A_STEPS, A_STEPS), :],
                                      xin.at[sl, :, j, :], sem_in.at[sl]) for j in range(NSEQ)]

    def out_copies(step, sl):
        gg = step // nt
        tt = lax.rem(step, nt)
        cps = []
        for buf, dst in ((x1o, x1_hbm), (ko, k_hbm), (vo, v_hbm)):
            cps += [pltpu.make_async_copy(buf.at[sl, :, j, :],
                                          dst.at[gg * NSEQ + j, pl.ds(tt * A_STEPS, A_STEPS), :],
                                          sem_out.at[sl]) for j in range(NSEQ)]
        return cps

    @pl.when(i == 0)
    def _():
        for c in in_copies(i, slot):
            c.start()

    @pl.when(i + 1 < total)
    def _():
        for c in in_copies(i + 1, 1 - slot):
            c.start()

    @pl.when(t == 0)
    def _():
        xbp_scr[0:CONV_PRE, :] = conv_in_ref[...].reshape(CONV_PRE, D_RNN)
        h_scr[...] = h_in_ref[...]

    @pl.when(i >= 2)
    def _():
        for c in out_copies(i - 2, slot):
            c.wait()

    for c in in_copies(i, slot):
        c.wait()

    x_ref = xin.at[slot].reshape(A_ROWS, D_MODEL)
    x1_ref = x1o.at[slot].reshape(A_ROWS, D_MODEL)
    k_ref = ko.at[slot].reshape(A_ROWS, KV_W)
    v_ref = vo.at[slot].reshape(A_ROWS, KV_W)

    _norm_rows_to(x_ref, norm_a_ref, xn_scr, A_ROWS, 128)
    lam = lam_ref[...]
    log_sig_lam = jnp.minimum(lam, 0.0) - jnp.log1p(jnp.exp(-jnp.abs(lam)))
    kk = jnp.broadcast_to((0.5 * LRU_C * LOG2_E) * log_sig_lam, (NSEQ, D_RNN))
    b_gx = jnp.broadcast_to(b_gx_ref[...], (NSEQ, D_RNN))
    b_ga = jnp.broadcast_to(b_ga_ref[...], (NSEQ, D_RNN))

    cw = conv_w_ref[...]
    cb = conv_b_ref[...]
    conv_blk = 32

    for n in range(N_GATE_BLOCKS):
        cols = slice(n * GATE_BLOCK, (n + 1) * GATE_BLOCK)
        xbp_scr[CONV_PRE:, cols] = jnp.dot(xn_scr[...], w_in_ref[:, cols],
                                           preferred_element_type=F32)
        gate_scr[:, cols] = jnp.dot(
            xn_scr[...], w_in_ref[:, D_RNN + n * GATE_BLOCK:D_RNN + (n + 1) * GATE_BLOCK],
            preferred_element_type=F32)

        for r in range(0, A_ROWS, conv_blk):
            acc = cb[:, cols]
            for j in range(CONV_W):
                acc = acc + cw[j:j + 1, cols] * xbp_scr[r + j * NSEQ:r + j * NSEQ + conv_blk, cols]
            xc_scr[r:r + conv_blk, cols] = acc
            xcb_scr[r:r + conv_blk, cols] = acc.astype(BF16)
        tail = xbp_scr[A_ROWS:A_ROWS + CONV_PRE, cols]
        conv_out_ref[:, :, cols] = tail.reshape(CONV_W - 1, NSEQ, GATE_BLOCK)
        xbp_scr[0:CONV_PRE, cols] = tail

        res = jnp.dot(xcb_scr[:, cols], w_gate_ref[n], preferred_element_type=F32)
        gx_scr[:, cols] = res[:, :GATE_BLOCK]
        ga_scr[:, cols] = res[:, GATE_BLOCK:]

        kk_n = kk[:, cols]
        bgx_n = b_gx[:, cols]
        bga_n = b_ga[:, cols]
        h = h_scr[:, cols]
        for r in range(0, A_ROWS, BF16_SUBLANES):
            hy = []
            for rr in (r, r + NSEQ):
                t_i = jnp.tanh(gx_scr[rr:rr + NSEQ, cols] + bgx_n)
                t_r = jnp.tanh(ga_scr[rr:rr + NSEQ, cols] + bga_n)
                log2_a = t_r * kk_n + kk_n
                a = jnp.exp2(log2_a)
                y = jnp.tanh(log2_a * (-LN_2)) * (a * a + 1.0)
                mult = jnp.where(y > 0.0, y * lax.rsqrt(y), 0.0)
                u = mult * xc_scr[rr:rr + NSEQ, cols]
                h = a * h + u * (0.5 * t_i + 0.5)
                hy.append(h * _silu_of_half(gate_scr[rr:rr + NSEQ, cols]))
            hy_scr[r:r + BF16_SUBLANES, cols] = jnp.concatenate(hy, axis=0).astype(BF16)
        h_scr[:, cols] = h
        h_out_ref[:, cols] = h

    for n in range(D_MODEL // MXU_N):
        cols = slice(n * MXU_N, (n + 1) * MXU_N)
        x1_ref[:, cols] = x_ref[:, cols] + jnp.dot(
            hy_scr[...], w_out_ref[:, cols], preferred_element_type=F32)

    _norm_rows_to(x1_ref, norm_kv_ref, xn_scr, A_ROWS, 128)
    kv = jnp.dot(xn_scr[...], w_kv_ref[...], preferred_element_type=F32)
    k = kv[:, :KV_W]
    lo = lax.broadcasted_iota(jnp.int32, (A_ROWS, KV_W), 1) < HEAD_DIM
    k2 = k * k
    ms_lo = jnp.sum(jnp.where(lo, k2, 0.0), axis=-1, keepdims=True) * (1.0 / HEAD_DIM)
    ms_hi = jnp.sum(jnp.where(lo, 0.0, k2), axis=-1, keepdims=True) * (1.0 / HEAD_DIM)
    rs = jnp.where(lo, lax.rsqrt(ms_lo + EPS), lax.rsqrt(ms_hi + EPS))
    k_ref[...] = k * rs * k_norm_ref[...]
    v_ref[...] = kv[:, KV_W:]

    for c in out_copies(i, slot):
        c.start()

    @pl.when(i == total - 1)
    def _():
        @pl.when(i >= 1)
        def _():
            for c in out_copies(i - 1, 1 - slot):
                c.wait()
        for c in out_copies(i, slot):
            c.wait()


def _layer_a(x, conv_in, h_in, p, name):
    n_seq, t_len, _ = x.shape
    assert n_seq % NSEQ == 0 and t_len % A_STEPS == 0
    grid = (n_seq // NSEQ, t_len // A_STEPS)
    const2 = lambda g, t: (0, 0)
    const3 = lambda g, t: (0, 0, 0)
    any_spec = pl.BlockSpec(memory_space=pl.ANY)

    weights = [p["norm_a"], p["w_in_a"], p["conv_w"], p["conv_b"], p["w_gate"], p["b_gx"],
               p["b_ga"], p["lam"], p["w_out_a"], p["norm_kv"], p["w_kv"], p["k_norm2"]]
    w_specs = [pl.BlockSpec(w.shape, const3 if w.ndim == 3 else const2) for w in weights]
    conv_spec = pl.BlockSpec((CONV_W - 1, NSEQ, D_RNN), lambda g, t: (0, g, 0))
    h_spec = pl.BlockSpec((NSEQ, D_RNN), lambda g, t: (g, 0))

    out_shape = [
        jax.ShapeDtypeStruct((n_seq, t_len, D_MODEL), F32),
        jax.ShapeDtypeStruct((n_seq, t_len, KV_W), F32),
        jax.ShapeDtypeStruct((n_seq, t_len, KV_W), F32),
        jax.ShapeDtypeStruct((CONV_W - 1, n_seq, D_RNN), F32),
        jax.ShapeDtypeStruct((n_seq, D_RNN), F32),
    ]
    scratch = [
        pltpu.VMEM((2, A_STEPS, NSEQ, D_MODEL), F32),
        pltpu.VMEM((2, A_STEPS, NSEQ, D_MODEL), F32),
        pltpu.VMEM((2, A_STEPS, NSEQ, KV_W), F32),
        pltpu.VMEM((2, A_STEPS, NSEQ, KV_W), F32),
        pltpu.SemaphoreType.DMA((2,)),
        pltpu.SemaphoreType.DMA((2,)),
        pltpu.VMEM((A_ROWS, D_MODEL), BF16),
        pltpu.VMEM((CONV_PRE + A_ROWS, D_RNN), F32),
        pltpu.VMEM((A_ROWS, D_RNN), F32),
        pltpu.VMEM((A_ROWS, D_RNN), F32),
        pltpu.VMEM((A_ROWS, D_RNN), BF16),
        pltpu.VMEM((A_ROWS, D_RNN), F32),
        pltpu.VMEM((A_ROWS, D_RNN), F32),
        pltpu.VMEM((A_ROWS, D_RNN), BF16),
        pltpu.VMEM((NSEQ, D_RNN), F32),
    ]
    return pl.pallas_call(
        _layer_a_kernel,
        grid=grid,
        in_specs=[any_spec, conv_spec, h_spec] + w_specs,
        out_specs=[any_spec, any_spec, any_spec, conv_spec, h_spec],
        out_shape=out_shape,
        scratch_shapes=scratch,
        compiler_params=pltpu.CompilerParams(
            dimension_semantics=("arbitrary", "arbitrary"), vmem_limit_bytes=VMEM_LIMIT_BYTES),
        name=name,
    )(x, conv_in, h_in, *weights)


def _layer_b_kernel(x1_ref, kh_ref, kc_ref, vh_ref, vc_ref, norm_b_ref, w_in_ref, hsel_ref,
                    hexp_ref, qg_ref, sinks_ref, w_out_ref, y_ref,
                    xn_scr, q_scr, gate_scr, qh_scr, kmat_scr, vmat_scr, o_scr,
                    *, nseg, seg, mask_first_halo):
    m = nseg * seg
    n_chunks = seg // CHUNK

    _norm_rows_to(x1_ref, norm_b_ref, xn_scr, m, 128)
    for n in range(D_MODEL // MXU_N):
        cols = slice(n * MXU_N, (n + 1) * MXU_N)
        q_scr[:, cols] = jnp.dot(xn_scr[...], w_in_ref[:, cols], preferred_element_type=F32)

    q = q_scr[...]
    ms = jnp.dot((q * q).astype(BF16), hsel_ref[...], preferred_element_type=F32)
    rs = lax.rsqrt(ms + EPS)
    rs_hi = rs.astype(BF16)
    rs_lo = (rs - rs_hi.astype(F32)).astype(BF16)
    rs_full = jnp.dot(jnp.concatenate([rs_hi, rs_lo], axis=1), hexp_ref[...],
                      preferred_element_type=F32)
    qh_scr[...] = (q * rs_full * qg_ref[...]).astype(BF16)

    lane = lax.broadcasted_iota(jnp.int32, (CHUNK, KV_W), 1)
    lo = lane < HEAD_DIM
    first = pl.program_id(1) == 0 if mask_first_halo else None

    for s in range(nseg):
        for r0, nrows, kval, vval in ((0, WINDOW, kh_ref[s], vh_ref[s]),
                                      (WINDOW, seg, kc_ref[s * seg:(s + 1) * seg, :],
                                       vc_ref[s * seg:(s + 1) * seg, :])):
            is_lo = lax.broadcasted_iota(jnp.int32, kval.shape, 1) < HEAD_DIM
            ind_e = jnp.where(is_lo, 1.0, 0.0).astype(BF16)
            ind_o = jnp.where(is_lo, 0.0, 1.0).astype(BF16)
            rows = slice(r0, r0 + nrows)
            for val, scr in ((kval, kmat_scr), (vval, vmat_scr)):
                swapped = pltpu.roll(val, HEAD_DIM, 1)
                for g in range(N_KV):
                    low = val if g == 0 else swapped
                    high = swapped if g == 0 else val
                    scr[s, g, 0, rows, 0:KV_W] = jnp.where(is_lo, low, 0.0).astype(BF16)
                    scr[s, g, 1, rows, 0:KV_W] = jnp.where(is_lo, 0.0, high).astype(BF16)
            for g in range(N_KV):
                vmat_scr[s, g, 0, rows, KV_W:] = ind_e
                vmat_scr[s, g, 1, rows, KV_W:] = ind_o

    def placed(scr, s, g, k0):
        return jnp.concatenate(
            [scr[s, g, 0, k0:k0 + WINDOW, :], scr[s, g, 1, k0:k0 + WINDOW, :],
             scr[s, g, 0, k0 + WINDOW:k0 + WINDOW + CHUNK, :],
             scr[s, g, 1, k0 + WINDOW:k0 + WINDOW + CHUNK, :]], axis=0)

    def scores(s, c, g):
        q0 = s * seg + c * CHUNK
        qst = jnp.concatenate(
            [qh_scr[q0:q0 + CHUNK, g * GROUP * HEAD_DIM + j * PAIR_W:
                    g * GROUP * HEAD_DIM + (j + 1) * PAIR_W] for j in range(N_PAIRS)],
            axis=0)
        return lax.dot_general(qst, placed(kmat_scr, s, g, c * CHUNK),
                               (((1,), (1,)), ((), ())),
                               preferred_element_type=F32)

    units = [(s, c, g) for s in range(nseg) for c in range(n_chunks) for g in range(N_KV)]
    gate_cols = list(range(D_MODEL // MXU_N))
    sc_next = scores(*units[0])
    for k, (s, c, g) in enumerate(units):
        sc = sc_next
        if k + 1 < len(units):
            sc_next = scores(*units[k + 1])
        if gate_cols:
            n = gate_cols.pop(0)
            gate_scr[:, n * MXU_N:(n + 1) * MXU_N] = jnp.dot(
                xn_scr[...], w_in_ref[:, D_MODEL + n * MXU_N:D_MODEL + (n + 1) * MXU_N],
                preferred_element_type=F32)
        p_rows, sink_rows = [], []
        for j in range(N_PAIRS):
            sj = sc[j * CHUNK:(j + 1) * CHUNK]
            c0 = sj[:, 0:KV_W]
            c1 = sj[:, KV_W:2 * KV_W]
            c2 = sj[:, 2 * KV_W:]
            if mask_first_halo and c * CHUNK < WINDOW:
                n_bad = jnp.where(first, WINDOW - c * CHUNK, 0)
                bad = lane < n_bad
                c0 = jnp.where(bad, NEG, c0)
                c1 = jnp.where(bad, NEG, c1)
            sink_e = sinks_ref[g * GROUP + 2 * j] * LOG2_E
            sink_o = sinks_ref[g * GROUP + 2 * j + 1] * LOG2_E
            m_e = jnp.maximum(jnp.max(jnp.maximum(c0, jnp.where(lo, c2, NEG)),
                                      axis=-1, keepdims=True), sink_e)
            m_o = jnp.maximum(jnp.max(jnp.maximum(c1, jnp.where(lo, NEG, c2)),
                                      axis=-1, keepdims=True), sink_o)
            p_rows.append(jnp.concatenate(
                [jnp.exp2(c0 - m_e), jnp.exp2(c1 - m_o),
                 jnp.exp2(c2 - jnp.where(lo, m_e, m_o))], axis=1).astype(BF16))
            sink_rows.append(jnp.where(lo, jnp.exp2(sink_e - m_e), jnp.exp2(sink_o - m_o)))
        pmat = jnp.concatenate(p_rows, axis=0)
        ov = jnp.dot(pmat, placed(vmat_scr, s, g, c * CHUNK),
                     preferred_element_type=F32)
        q0 = s * seg + c * CHUNK
        for j in range(N_PAIRS):
            rows = slice(j * CHUNK, (j + 1) * CHUNK)
            c_lo = g * GROUP * HEAD_DIM + j * PAIR_W
            o_scr[q0:q0 + CHUNK, c_lo:c_lo + PAIR_W] = (
                ov[rows, :KV_W] / (ov[rows, KV_W:] + sink_rows[j]))
    assert not gate_cols

    blk = 32
    for r in range(0, m, blk):
        xn_scr[r:r + blk, :] = (
            o_scr[r:r + blk, :] * _silu_of_half(gate_scr[r:r + blk, :])).astype(BF16)
    for n in range(D_MODEL // MXU_N):
        cols = slice(n * MXU_N, (n + 1) * MXU_N)
        y_ref[:, cols] = x1_ref[:, cols] + jnp.dot(
            xn_scr[...], w_out_ref[:, cols], preferred_element_type=F32)


def _layer_b(x1, k_halo, v_halo, k_cur, v_cur, p, *, n_batch, t_len, nseg, seg, prompt):
    m = nseg * seg
    rows = n_batch * t_len
    const2 = lambda *_: (0, 0)
    if prompt:
        steps = t_len // m
        per_b = t_len // WINDOW
        grid = (n_batch, steps)
        row_map = lambda b, t: (b * steps + t, 0)
        halo_map = lambda b, t: (b * per_b + jnp.maximum(t * (m // WINDOW) - 1, 0), 0, 0)
    else:
        grid = (n_batch // nseg, 1)
        row_map = lambda i, t: (i, 0)
        halo_map = lambda i, t: (i, 0, 0)

    weights = [p["norm_b"], p["w_in_b"], p["hsel"], p["hexp"], p["qg"]]
    in_specs = [
        pl.BlockSpec((m, D_MODEL), row_map),
        pl.BlockSpec((nseg, WINDOW, KV_W), halo_map),
        pl.BlockSpec((m, KV_W), row_map),
        pl.BlockSpec((nseg, WINDOW, KV_W), halo_map),
        pl.BlockSpec((m, KV_W), row_map),
    ] + [pl.BlockSpec(w.shape, const2) for w in weights] + [
        pl.BlockSpec(memory_space=pltpu.SMEM),
        pl.BlockSpec(p["w_out_b"].shape, const2),
    ]
    kv_rows = WINDOW + seg
    scratch = [
        pltpu.VMEM((m, D_MODEL), BF16),
        pltpu.VMEM((m, D_MODEL), F32),
        pltpu.VMEM((m, D_MODEL), F32),
        pltpu.VMEM((m, D_MODEL), BF16),
        pltpu.VMEM((nseg, N_KV, 2, kv_rows, KV_W), BF16),
        pltpu.VMEM((nseg, N_KV, 2, kv_rows, 2 * KV_W), BF16),
        pltpu.VMEM((m, D_MODEL), F32),
    ]
    return pl.pallas_call(
        functools.partial(_layer_b_kernel, nseg=nseg, seg=seg, mask_first_halo=prompt),
        grid=grid,
        in_specs=in_specs,
        out_specs=pl.BlockSpec((m, D_MODEL), row_map),
        out_shape=jax.ShapeDtypeStruct((rows, D_MODEL), F32),
        scratch_shapes=scratch,
        compiler_params=pltpu.CompilerParams(
            dimension_semantics=("arbitrary", "arbitrary"), vmem_limit_bytes=VMEM_LIMIT_BYTES),
        name="layer_b_prompt" if prompt else "layer_b_sample",
    )(x1, k_halo, k_cur, v_halo, v_cur, *weights, p["sinks"], p["w_out_b"])


def _prep_params(norm_a, w_in_a, conv_w, conv_b, w_gate_x, b_gate_x, w_gate_a, b_gate_a,
                 lru_lambda, w_out_a, norm_kv, w_kv, k_norm, norm_b, w_in_b, q_norm, sinks,
                 w_out_b):
    row = lambda v: v.reshape(1, -1).astype(F32)
    in_scale = jnp.concatenate([jnp.ones((D_MODEL,), F32), jnp.full((D_MODEL,), 0.5, F32)])
    head_of_col = jnp.arange(D_MODEL) // HEAD_DIM
    hsel = (head_of_col[:, None] == jnp.arange(LANES)[None, :]).astype(F32) * (1.0 / HEAD_DIM)
    hexp = (jnp.arange(2 * LANES)[:, None] % LANES == head_of_col[None, :]).astype(BF16)
    return {
        "norm_a": row(norm_a[0]),
        "w_in_a": (w_in_a[0] * in_scale).astype(BF16),
        "conv_w": conv_w[0].astype(F32),
        "conv_b": row(conv_b[0]),
        "w_gate": (0.5 * jnp.concatenate([w_gate_x[0], w_gate_a[0]], axis=-1)).astype(BF16),
        "b_gx": row(0.5 * b_gate_x[0]),
        "b_ga": row(0.5 * b_gate_a[0]),
        "lam": row(lru_lambda[0]),
        "w_out_a": w_out_a[0].astype(BF16),
        "norm_kv": row(norm_kv),
        "w_kv": w_kv.astype(BF16),
        "k_norm2": row(jnp.tile(k_norm, N_KV)),
        "norm_b": row(norm_b[0]),
        "w_in_b": (w_in_b[0] * in_scale).astype(BF16),
        "hsel": hsel.astype(BF16),
        "hexp": hexp,
        "qg": row(jnp.tile(q_norm[0], N_HEADS) * (HEAD_DIM ** -0.5 * LOG2_E)),
        "sinks": sinks[0].astype(F32),
        "w_out_b": w_out_b[0].astype(BF16),
    }


def kernel(x_prompt, x_sample, state_conv, state_rglru, cache_k_win, cache_v_win, norm_a, w_in_a, conv_w, conv_b, w_gate_x, b_gate_x, w_gate_a, b_gate_a, lru_lambda, w_out_a, norm_kv, w_kv, k_norm, norm_b, w_in_b, q_norm, sinks, w_out_b):
    assert norm_a.shape[0] == 1 and norm_b.shape[0] == 1, "one recurrent + one attention layer"
    p = _prep_params(norm_a, w_in_a, conv_w, conv_b, w_gate_x, b_gate_x, w_gate_a, b_gate_a,
                     lru_lambda, w_out_a, norm_kv, w_kv, k_norm, norm_b, w_in_b, q_norm, sinks,
                     w_out_b)
    bp, tp, _ = x_prompt.shape
    bs, ts, _ = x_sample.shape

    x1p, kp, vp, conv_p, h_p = _layer_a(
        x_prompt, jnp.zeros((CONV_W - 1, bp, D_RNN), F32), jnp.zeros((bp, D_RNN), F32), p,
        "layer_a_prompt")
    tile_b = 256
    kp2 = kp.reshape(bp * tp, KV_W)
    vp2 = vp.reshape(bp * tp, KV_W)
    kp3 = kp.reshape(bp * tp // WINDOW, WINDOW, KV_W)
    vp3 = vp.reshape(bp * tp // WINDOW, WINDOW, KV_W)
    y_p = _layer_b(x1p.reshape(bp * tp, D_MODEL), kp3, vp3, kp2, vp2, p, n_batch=bp, t_len=tp,
                   nseg=1, seg=tile_b, prompt=True)

    x1s, ks, vs, conv_s, h_s = _layer_a(
        x_sample, jnp.transpose(state_conv[0], (1, 0, 2)), state_rglru[0], p, "layer_a_sample")
    y_s = _layer_b(x1s.reshape(bs * ts, D_MODEL), cache_k_win.reshape(bs, WINDOW, KV_W),
                   cache_v_win.reshape(bs, WINDOW, KV_W), ks.reshape(bs * ts, KV_W),
                   vs.reshape(bs * ts, KV_W), p, n_batch=bs, t_len=ts, nseg=4, seg=ts,
                   prompt=False)

    kp4 = kp.reshape(bp, tp, N_KV, HEAD_DIM)[:, -WINDOW:]
    vp4 = vp.reshape(bp, tp, N_KV, HEAD_DIM)[:, -WINDOW:]
    ks4 = jnp.concatenate([cache_k_win, ks.reshape(bs, ts, N_KV, HEAD_DIM)], axis=1)[:, -WINDOW:]
    vs4 = jnp.concatenate([cache_v_win, vs.reshape(bs, ts, N_KV, HEAD_DIM)], axis=1)[:, -WINDOW:]
    return (y_p.reshape(bp, tp, D_MODEL), y_s.reshape(bs, ts, D_MODEL),
            jnp.transpose(conv_p, (1, 0, 2))[None], h_p[None], kp4, vp4,
            jnp.transpose(conv_s, (1, 0, 2))[None], h_s[None], ks4, vs4)
```

```python
import functools

import jax
import jax.numpy as jnp
from jax import lax
from jax.experimental import pallas as pl
from jax.experimental.pallas import tpu as pltpu

D_MODEL = 1024
D_RNN = 1024
GATE_BLOCK = 256
N_GATE_BLOCKS = D_RNN // GATE_BLOCK
CONV_W = 4
LRU_C = 8.0
HEAD_DIM = 64
N_HEADS = 16
N_KV = 2
GROUP = N_HEADS // N_KV
CHUNK = 64
WINDOW = 128
EPS = 1e-6
NEG = -1e30
LOG2_E = 1.4426950408889634
LN_2 = 0.6931471805599453

SUBLANES = 8
LANES = 128
BF16_SUBLANES = 16
MXU_N = 256
VMEM_LIMIT_BYTES = 56 * 1024 * 1024

F32 = jnp.float32
BF16 = jnp.bfloat16

KV_W = N_KV * HEAD_DIM
PAIR_W = 2 * HEAD_DIM
N_PAIRS = GROUP // 2
LOOKAHEAD = 2

NSEQ = SUBLANES
A_STEPS = 64
A_ROWS = A_STEPS * NSEQ
CONV_PRE = (CONV_W - 1) * NSEQ


def _silu_of_half(h):
    return h * jnp.tanh(h) + h


def _norm_rows_to(x_ref, g_ref, dst_ref, n_rows, blk):
    g = g_ref[...]
    n_blk = n_rows // blk

    def inv_rms(i):
        r = pl.multiple_of(i * blk, blk)
        x = x_ref[pl.ds(r, blk), :]
        return lax.rsqrt(jnp.mean(x * x, axis=-1, keepdims=True) + EPS)

    def body(i, rs):
        rs_next = inv_rms(jnp.minimum(i + 1, n_blk - 1))
        r = pl.multiple_of(i * blk, blk)
        dst_ref[pl.ds(r, blk), :] = (x_ref[pl.ds(r, blk), :] * rs * g).astype(BF16)
        return rs_next

    lax.fori_loop(0, n_blk, body, inv_rms(0), unroll=True)


def _layer_a_kernel(x_hbm, conv_in_ref, h_in_ref,
                    norm_a_ref, w_in_ref, conv_w_ref, conv_b_ref, w_gate_ref, b_gx_ref, b_ga_ref,
                    lam_ref, w_out_ref, norm_kv_ref, w_kv_ref, k_norm_ref,
                    x1_hbm, k_hbm, v_hbm, conv_out_ref, h_out_ref,
                    xin, x1o, ko, vo, sem_in, sem_out,
                    xn_scr, xbp_scr, gate_scr, xc_scr, xcb_scr, gx_scr, ga_scr, hy_scr, h_scr):
    g = pl.program_id(0)
    t = pl.program_id(1)
    nt = pl.num_programs(1)
    total = pl.num_programs(0) * nt
    i = g * nt + t
    slot = lax.rem(i, 2)

    def in_copies(step, sl):
        gg = step // nt
        tt = lax.rem(step, nt)
        return [pltpu.make_async_copy(x_hbm.at[gg * NSEQ + j, pl.ds(tt * A_STEPS, A_STEPS), :],
                                      xin.at[sl, :, j, :], sem_in.at[sl]) for j in range(NSEQ)]

    def out_copies(step, sl):
        gg = step // nt
        tt = lax.rem(step, nt)
        cps = []
        for buf, dst in ((x1o, x1_hbm), (ko, k_hbm), (vo, v_hbm)):
            cps += [pltpu.make_async_copy(buf.at[sl, :, j, :],
                                          dst.at[gg * NSEQ + j, pl.ds(tt * A_STEPS, A_STEPS), :],
                                          sem_out.at[sl]) for j in range(NSEQ)]
        return cps

    @pl.when(i == 0)
    def _():
        for c in in_copies(i, slot):
            c.start()

    @pl.when(i + 1 < total)
    def _():
        for c in in_copies(i + 1, 1 - slot):
            c.start()

    @pl.when(t == 0)
    def _():
        xbp_scr[0:CONV_PRE, :] = conv_in_ref[...].reshape(CONV_PRE, D_RNN)
        h_scr[...] = h_in_ref[...]

    @pl.when(i >= 2)
    def _():
        for c in out_copies(i - 2, slot):
            c.wait()

    for c in in_copies(i, slot):
        c.wait()

    x_ref = xin.at[slot].reshape(A_ROWS, D_MODEL)
    x1_ref = x1o.at[slot].reshape(A_ROWS, D_MODEL)
    k_ref = ko.at[slot].reshape(A_ROWS, KV_W)
    v_ref = vo.at[slot].reshape(A_ROWS, KV_W)

    _norm_rows_to(x_ref, norm_a_ref, xn_scr, A_ROWS, 128)
    lam = lam_ref[...]
    log_sig_lam = jnp.minimum(lam, 0.0) - jnp.log1p(jnp.exp(-jnp.abs(lam)))
    kk = jnp.broadcast_to((0.5 * LRU_C * LOG2_E) * log_sig_lam, (NSEQ, D_RNN))
    b_gx = jnp.broadcast_to(b_gx_ref[...], (NSEQ, D_RNN))
    b_ga = jnp.broadcast_to(b_ga_ref[...], (NSEQ, D_RNN))

    cw = conv_w_ref[...]
    cb = conv_b_ref[...]
    conv_blk = 32

    def in_proj_x(n):
        cols = slice(n * GATE_BLOCK, (n + 1) * GATE_BLOCK)
        xbp_scr[CONV_PRE:, cols] = jnp.dot(xn_scr[...], w_in_ref[:, cols],
                                           preferred_element_type=F32)

    def in_proj_gate(n):
        cols = slice(n * GATE_BLOCK, (n + 1) * GATE_BLOCK)
        gate_scr[:, cols] = jnp.dot(
            xn_scr[...], w_in_ref[:, D_RNN + n * GATE_BLOCK:D_RNN + (n + 1) * GATE_BLOCK],
            preferred_element_type=F32)

    def conv_and_gates(n):
        cols = slice(n * GATE_BLOCK, (n + 1) * GATE_BLOCK)
        for r in range(0, A_ROWS, conv_blk):
            acc = cb[:, cols]
            for j in range(CONV_W):
                acc = acc + cw[j:j + 1, cols] * xbp_scr[r + j * NSEQ:r + j * NSEQ + conv_blk, cols]
            xc_scr[r:r + conv_blk, cols] = acc
            xcb_scr[r:r + conv_blk, cols] = acc.astype(BF16)
        tail = xbp_scr[A_ROWS:A_ROWS + CONV_PRE, cols]
        conv_out_ref[:, :, cols] = tail.reshape(CONV_W - 1, NSEQ, GATE_BLOCK)
        xbp_scr[0:CONV_PRE, cols] = tail

        res = jnp.dot(xcb_scr[:, cols], w_gate_ref[n], preferred_element_type=F32)
        gx_scr[:, cols] = res[:, :GATE_BLOCK]
        ga_scr[:, cols] = res[:, GATE_BLOCK:]

    def scan(n):
        cols = slice(n * GATE_BLOCK, (n + 1) * GATE_BLOCK)
        kk_n = kk[:, cols]
        bgx_n = b_gx[:, cols]
        bga_n = b_ga[:, cols]
        h = h_scr[:, cols]
        for r in range(0, A_ROWS, BF16_SUBLANES):
            hy = []
            for rr in (r, r + NSEQ):
                t_i = jnp.tanh(gx_scr[rr:rr + NSEQ, cols] + bgx_n)
                t_r = jnp.tanh(ga_scr[rr:rr + NSEQ, cols] + bga_n)
                log2_a = t_r * kk_n + kk_n
                a = jnp.exp2(log2_a)
                y = jnp.tanh(log2_a * (-LN_2)) * (a * a + 1.0)
                mult = jnp.where(y > 0.0, y * lax.rsqrt(y), 0.0)
                u = mult * xc_scr[rr:rr + NSEQ, cols]
                h = a * h + u * (0.5 * t_i + 0.5)
                hy.append(h * _silu_of_half(gate_scr[rr:rr + NSEQ, cols]))
            hy_scr[r:r + BF16_SUBLANES, cols] = jnp.concatenate(hy, axis=0).astype(BF16)
        h_scr[:, cols] = h
        h_out_ref[:, cols] = h

    in_proj_x(0)
    in_proj_gate(0)
    for n in range(N_GATE_BLOCKS):
        if n + 1 < N_GATE_BLOCKS:
            in_proj_x(n + 1)
        conv_and_gates(n)
        if n + 1 < N_GATE_BLOCKS:
            in_proj_gate(n + 1)
        scan(n)

    for n in range(D_MODEL // MXU_N):
        cols = slice(n * MXU_N, (n + 1) * MXU_N)
        x1_ref[:, cols] = x_ref[:, cols] + jnp.dot(
            hy_scr[...], w_out_ref[:, cols], preferred_element_type=F32)

    _norm_rows_to(x1_ref, norm_kv_ref, xn_scr, A_ROWS, 128)
    kv = jnp.dot(xn_scr[...], w_kv_ref[...], preferred_element_type=F32)
    k = kv[:, :KV_W]
    lo = lax.broadcasted_iota(jnp.int32, (A_ROWS, KV_W), 1) < HEAD_DIM
    k2 = k * k
    ms_lo = jnp.sum(jnp.where(lo, k2, 0.0), axis=-1, keepdims=True) * (1.0 / HEAD_DIM)
    ms_hi = jnp.sum(jnp.where(lo, 0.0, k2), axis=-1, keepdims=True) * (1.0 / HEAD_DIM)
    rs = jnp.where(lo, lax.rsqrt(ms_lo + EPS), lax.rsqrt(ms_hi + EPS))
    k_ref[...] = k * rs * k_norm_ref[...]
    v_ref[...] = kv[:, KV_W:]

    for c in out_copies(i, slot):
        c.start()

    @pl.when(i == total - 1)
    def _():
        @pl.when(i >= 1)
        def _():
            for c in out_copies(i - 1, 1 - slot):
                c.wait()
        for c in out_copies(i, slot):
            c.wait()


def _layer_a(x, conv_in, h_in, p, name):
    n_seq, t_len, _ = x.shape
    assert n_seq % NSEQ == 0 and t_len % A_STEPS == 0
    grid = (n_seq // NSEQ, t_len // A_STEPS)
    const2 = lambda g, t: (0, 0)
    const3 = lambda g, t: (0, 0, 0)
    any_spec = pl.BlockSpec(memory_space=pl.ANY)

    weights = [p["norm_a"], p["w_in_a"], p["conv_w"], p["conv_b"], p["w_gate"], p["b_gx"],
               p["b_ga"], p["lam"], p["w_out_a"], p["norm_kv"], p["w_kv"], p["k_norm2"]]
    w_specs = [pl.BlockSpec(w.shape, const3 if w.ndim == 3 else const2) for w in weights]
    conv_spec = pl.BlockSpec((CONV_W - 1, NSEQ, D_RNN), lambda g, t: (0, g, 0))
    h_spec = pl.BlockSpec((NSEQ, D_RNN), lambda g, t: (g, 0))

    out_shape = [
        jax.ShapeDtypeStruct((n_seq, t_len, D_MODEL), F32),
        jax.ShapeDtypeStruct((n_seq, t_len, KV_W), F32),
        jax.ShapeDtypeStruct((n_seq, t_len, KV_W), F32),
        jax.ShapeDtypeStruct((CONV_W - 1, n_seq, D_RNN), F32),
        jax.ShapeDtypeStruct((n_seq, D_RNN), F32),
    ]
    scratch = [
        pltpu.VMEM((2, A_STEPS, NSEQ, D_MODEL), F32),
        pltpu.VMEM((2, A_STEPS, NSEQ, D_MODEL), F32),
        pltpu.VMEM((2, A_STEPS, NSEQ, KV_W), F32),
        pltpu.VMEM((2, A_STEPS, NSEQ, KV_W), F32),
        pltpu.SemaphoreType.DMA((2,)),
        pltpu.SemaphoreType.DMA((2,)),
        pltpu.VMEM((A_ROWS, D_MODEL), BF16),
        pltpu.VMEM((CONV_PRE + A_ROWS, D_RNN), F32),
        pltpu.VMEM((A_ROWS, D_RNN), F32),
        pltpu.VMEM((A_ROWS, D_RNN), F32),
        pltpu.VMEM((A_ROWS, D_RNN), BF16),
        pltpu.VMEM((A_ROWS, D_RNN), F32),
        pltpu.VMEM((A_ROWS, D_RNN), F32),
        pltpu.VMEM((A_ROWS, D_RNN), BF16),
        pltpu.VMEM((NSEQ, D_RNN), F32),
    ]
    return pl.pallas_call(
        _layer_a_kernel,
        grid=grid,
        in_specs=[any_spec, conv_spec, h_spec] + w_specs,
        out_specs=[any_spec, any_spec, any_spec, conv_spec, h_spec],
        out_shape=out_shape,
        scratch_shapes=scratch,
        compiler_params=pltpu.CompilerParams(
            dimension_semantics=("arbitrary", "arbitrary"), vmem_limit_bytes=VMEM_LIMIT_BYTES),
        name=name,
    )(x, conv_in, h_in, *weights)


def _layer_b_kernel(x1_ref, kh_ref, kc_ref, vh_ref, vc_ref, norm_b_ref, w_in_ref,
                    qg_ref, sinks_ref, w_out_ref, y_ref,
                    xn_scr, gate_scr, qh_scr, kmat_scr, vmat_scr, o_scr,
                    *, nseg, seg, mask_first_halo):
    m = nseg * seg
    n_chunks = seg // CHUNK

    _norm_rows_to(x1_ref, norm_b_ref, xn_scr, m, 128)

    lo_m = lax.broadcasted_iota(jnp.int32, (m, LANES), 1) < HEAD_DIM
    qg = qg_ref[...]
    for n in range(D_MODEL // MXU_N):
        qn = jnp.dot(xn_scr[...], w_in_ref[:, n * MXU_N:(n + 1) * MXU_N],
                     preferred_element_type=F32)
        for cc in range(MXU_N // LANES):
            col = n * MXU_N + cc * LANES
            x = qn[:, cc * LANES:(cc + 1) * LANES]
            sq = x * x
            ms_lo = jnp.sum(jnp.where(lo_m, sq, 0.0), axis=-1, keepdims=True) * (1.0 / HEAD_DIM)
            ms_hi = jnp.sum(jnp.where(lo_m, 0.0, sq), axis=-1, keepdims=True) * (1.0 / HEAD_DIM)
            rs = jnp.where(lo_m, lax.rsqrt(ms_lo + EPS), lax.rsqrt(ms_hi + EPS))
            qh_scr[:, col:col + LANES] = (x * rs * qg[:, col:col + LANES]).astype(BF16)


    lane = lax.broadcasted_iota(jnp.int32, (CHUNK, KV_W), 1)
    lo = lane < HEAD_DIM
    first = pl.program_id(1) == 0 if mask_first_halo else None

    for s in range(nseg):
        for r0, nrows, kval, vval in ((0, WINDOW, kh_ref[s], vh_ref[s]),
                                      (WINDOW, seg, kc_ref[s * seg:(s + 1) * seg, :],
                                       vc_ref[s * seg:(s + 1) * seg, :])):
            is_lo = lax.broadcasted_iota(jnp.int32, kval.shape, 1) < HEAD_DIM
            ind_e = jnp.where(is_lo, 1.0, 0.0).astype(BF16)
            ind_o = jnp.where(is_lo, 0.0, 1.0).astype(BF16)
            rows = slice(r0, r0 + nrows)
            for val, scr in ((kval, kmat_scr), (vval, vmat_scr)):
                swapped = pltpu.roll(val, HEAD_DIM, 1)
                for g in range(N_KV):
                    low = val if g == 0 else swapped
                    high = swapped if g == 0 else val
                    scr[s, g, 0, rows, 0:KV_W] = jnp.where(is_lo, low, 0.0).astype(BF16)
                    scr[s, g, 1, rows, 0:KV_W] = jnp.where(is_lo, 0.0, high).astype(BF16)
            for g in range(N_KV):
                vmat_scr[s, g, 0, rows, KV_W:] = ind_e
                vmat_scr[s, g, 1, rows, KV_W:] = ind_o

    def placed(scr, s, g, k0):
        return jnp.concatenate(
            [scr[s, g, 0, k0:k0 + WINDOW, :], scr[s, g, 1, k0:k0 + WINDOW, :],
             scr[s, g, 0, k0 + WINDOW:k0 + WINDOW + CHUNK, :],
             scr[s, g, 1, k0 + WINDOW:k0 + WINDOW + CHUNK, :]], axis=0)

    def scores(s, c, g):
        q0 = s * seg + c * CHUNK
        qst = jnp.concatenate(
            [qh_scr[q0:q0 + CHUNK, g * GROUP * HEAD_DIM + j * PAIR_W:
                    g * GROUP * HEAD_DIM + (j + 1) * PAIR_W] for j in range(N_PAIRS)],
            axis=0)
        return lax.dot_general(qst, placed(kmat_scr, s, g, c * CHUNK),
                               (((1,), (1,)), ((), ())),
                               preferred_element_type=F32)

    units = [(s, c, g) for s in range(nseg) for c in range(n_chunks) for g in range(N_KV)]
    row_split = 2 if m >= 2 * MXU_N else 1
    gate_pieces = [(n, rh) for n in range(D_MODEL // MXU_N) for rh in range(row_split)]
    assert len(gate_pieces) <= len(units)

    def gate_piece():
        n, rh = gate_pieces.pop(0)
        rows = slice(rh * (m // row_split), (rh + 1) * (m // row_split))
        gate_scr[rows, n * MXU_N:(n + 1) * MXU_N] = jnp.dot(
            xn_scr[rows, :], w_in_ref[:, D_MODEL + n * MXU_N:D_MODEL + (n + 1) * MXU_N],
            preferred_element_type=F32)

    for _ in range(row_split):
        gate_piece()
    pending = [scores(*u) for u in units[:LOOKAHEAD]]
    for k, (s, c, g) in enumerate(units):
        sc = pending.pop(0)
        if k + LOOKAHEAD < len(units):
            pending.append(scores(*units[k + LOOKAHEAD]))
        if gate_pieces:
            gate_piece()
        p_rows, sink_rows = [], []
        for j in range(N_PAIRS):
            sj = sc[j * CHUNK:(j + 1) * CHUNK]
            c0 = sj[:, 0:KV_W]
            c1 = sj[:, KV_W:2 * KV_W]
            c2 = sj[:, 2 * KV_W:]
            if mask_first_halo and c * CHUNK < WINDOW:
                n_bad = jnp.where(first, WINDOW - c * CHUNK, 0)
                bad = lane < n_bad
                c0 = jnp.where(bad, NEG, c0)
                c1 = jnp.where(bad, NEG, c1)
            sink_e = sinks_ref[g * GROUP + 2 * j] * LOG2_E
            sink_o = sinks_ref[g * GROUP + 2 * j + 1] * LOG2_E
            m_e = jnp.maximum(jnp.max(jnp.maximum(c0, jnp.where(lo, c2, NEG)),
                                      axis=-1, keepdims=True), sink_e)
            m_o = jnp.maximum(jnp.max(jnp.maximum(c1, jnp.where(lo, NEG, c2)),
                                      axis=-1, keepdims=True), sink_o)
            p_rows.append(jnp.concatenate(
                [jnp.exp2(c0 - m_e), jnp.exp2(c1 - m_o),
                 jnp.exp2(c2 - jnp.where(lo, m_e, m_o))], axis=1).astype(BF16))
            sink_rows.append(jnp.where(lo, jnp.exp2(sink_e - m_e), jnp.exp2(sink_o - m_o)))
        pmat = jnp.concatenate(p_rows, axis=0)
        ov = jnp.dot(pmat, placed(vmat_scr, s, g, c * CHUNK),
                     preferred_element_type=F32)
        q0 = s * seg + c * CHUNK
        for j in range(N_PAIRS):
            rows = slice(j * CHUNK, (j + 1) * CHUNK)
            c_lo = g * GROUP * HEAD_DIM + j * PAIR_W
            o_scr[q0:q0 + CHUNK, c_lo:c_lo + PAIR_W] = (
                ov[rows, :KV_W] / (ov[rows, KV_W:] + sink_rows[j]))
    assert not gate_pieces

    blk = 32
    for r in range(0, m, blk):
        xn_scr[r:r + blk, :] = (
            o_scr[r:r + blk, :] * _silu_of_half(gate_scr[r:r + blk, :])).astype(BF16)
    for n in range(D_MODEL // MXU_N):
        cols = slice(n * MXU_N, (n + 1) * MXU_N)
        y_ref[:, cols] = x1_ref[:, cols] + jnp.dot(
            xn_scr[...], w_out_ref[:, cols], preferred_element_type=F32)


def _layer_b(x1, k_halo, v_halo, k_cur, v_cur, p, *, n_batch, t_len, nseg, seg, prompt):
    m = nseg * seg
    rows = n_batch * t_len
    const2 = lambda *_: (0, 0)
    if prompt:
        steps = t_len // m
        per_b = t_len // WINDOW
        grid = (n_batch, steps)
        row_map = lambda b, t: (b * steps + t, 0)
        halo_map = lambda b, t: (b * per_b + jnp.maximum(t * (m // WINDOW) - 1, 0), 0, 0)
    else:
        grid = (n_batch // nseg, 1)
        row_map = lambda i, t: (i, 0)
        halo_map = lambda i, t: (i, 0, 0)

    weights = [p["norm_b"], p["w_in_b"], p["qg"]]
    in_specs = [
        pl.BlockSpec((m, D_MODEL), row_map),
        pl.BlockSpec((nseg, WINDOW, KV_W), halo_map),
        pl.BlockSpec((m, KV_W), row_map),
        pl.BlockSpec((nseg, WINDOW, KV_W), halo_map),
        pl.BlockSpec((m, KV_W), row_map),
    ] + [pl.BlockSpec(w.shape, const2) for w in weights] + [
        pl.BlockSpec(memory_space=pltpu.SMEM),
        pl.BlockSpec(p["w_out_b"].shape, const2),
    ]
    kv_rows = WINDOW + seg
    scratch = [
        pltpu.VMEM((m, D_MODEL), BF16),
        pltpu.VMEM((m, D_MODEL), F32),
        pltpu.VMEM((m, D_MODEL), BF16),
        pltpu.VMEM((nseg, N_KV, 2, kv_rows, KV_W), BF16),
        pltpu.VMEM((nseg, N_KV, 2, kv_rows, 2 * KV_W), BF16),
        pltpu.VMEM((m, D_MODEL), F32),
    ]
    return pl.pallas_call(
        functools.partial(_layer_b_kernel, nseg=nseg, seg=seg, mask_first_halo=prompt),
        grid=grid,
        in_specs=in_specs,
        out_specs=pl.BlockSpec((m, D_MODEL), row_map),
        out_shape=jax.ShapeDtypeStruct((rows, D_MODEL), F32),
        scratch_shapes=scratch,
        compiler_params=pltpu.CompilerParams(
            dimension_semantics=("arbitrary", "arbitrary"), vmem_limit_bytes=VMEM_LIMIT_BYTES),
        name="layer_b_prompt" if prompt else "layer_b_sample",
    )(x1, k_halo, k_cur, v_halo, v_cur, *weights, p["sinks"], p["w_out_b"])


def _prep_params(norm_a, w_in_a, conv_w, conv_b, w_gate_x, b_gate_x, w_gate_a, b_gate_a,
                 lru_lambda, w_out_a, norm_kv, w_kv, k_norm, norm_b, w_in_b, q_norm, sinks,
                 w_out_b):
    row = lambda v: v.reshape(1, -1).astype(F32)
    in_scale = jnp.concatenate([jnp.ones((D_MODEL,), F32), jnp.full((D_MODEL,), 0.5, F32)])
    return {
        "norm_a": row(norm_a[0]),
        "w_in_a": (w_in_a[0] * in_scale).astype(BF16),
        "conv_w": conv_w[0].astype(F32),
        "conv_b": row(conv_b[0]),
        "w_gate": (0.5 * jnp.concatenate([w_gate_x[0], w_gate_a[0]], axis=-1)).astype(BF16),
        "b_gx": row(0.5 * b_gate_x[0]),
        "b_ga": row(0.5 * b_gate_a[0]),
        "lam": row(lru_lambda[0]),
        "w_out_a": w_out_a[0].astype(BF16),
        "norm_kv": row(norm_kv),
        "w_kv": w_kv.astype(BF16),
        "k_norm2": row(jnp.tile(k_norm, N_KV)),
        "norm_b": row(norm_b[0]),
        "w_in_b": (w_in_b[0] * in_scale).astype(BF16),
        "qg": row(jnp.tile(q_norm[0], N_HEADS) * (HEAD_DIM ** -0.5 * LOG2_E)),
        "sinks": sinks[0].astype(F32),
        "w_out_b": w_out_b[0].astype(BF16),
    }


def kernel(x_prompt, x_sample, state_conv, state_rglru, cache_k_win, cache_v_win, norm_a, w_in_a, conv_w, conv_b, w_gate_x, b_gate_x, w_gate_a, b_gate_a, lru_lambda, w_out_a, norm_kv, w_kv, k_norm, norm_b, w_in_b, q_norm, sinks, w_out_b):
    assert norm_a.shape[0] == 1 and norm_b.shape[0] == 1, "one recurrent + one attention layer"
    p = _prep_params(norm_a, w_in_a, conv_w, conv_b, w_gate_x, b_gate_x, w_gate_a, b_gate_a,
                     lru_lambda, w_out_a, norm_kv, w_kv, k_norm, norm_b, w_in_b, q_norm, sinks,
                     w_out_b)
    bp, tp, _ = x_prompt.shape
    bs, ts, _ = x_sample.shape

    x1p, kp, vp, conv_p, h_p = _layer_a(
        x_prompt, jnp.zeros((CONV_W - 1, bp, D_RNN), F32), jnp.zeros((bp, D_RNN), F32), p,
        "layer_a_prompt")
    tile_b = 512
    kp2 = kp.reshape(bp * tp, KV_W)
    vp2 = vp.reshape(bp * tp, KV_W)
    kp3 = kp.reshape(bp * tp // WINDOW, WINDOW, KV_W)
    vp3 = vp.reshape(bp * tp // WINDOW, WINDOW, KV_W)
    y_p = _layer_b(x1p.reshape(bp * tp, D_MODEL), kp3, vp3, kp2, vp2, p, n_batch=bp, t_len=tp,
                   nseg=1, seg=tile_b, prompt=True)

    x1s, ks, vs, conv_s, h_s = _layer_a(
        x_sample, jnp.transpose(state_conv[0], (1, 0, 2)), state_rglru[0], p, "layer_a_sample")
    y_s = _layer_b(x1s.reshape(bs * ts, D_MODEL), cache_k_win.reshape(bs, WINDOW, KV_W),
                   cache_v_win.reshape(bs, WINDOW, KV_W), ks.reshape(bs * ts, KV_W),
                   vs.reshape(bs * ts, KV_W), p, n_batch=bs, t_len=ts, nseg=8, seg=ts,
                   prompt=False)

    kp4 = kp.reshape(bp, tp, N_KV, HEAD_DIM)[:, -WINDOW:]
    vp4 = vp.reshape(bp, tp, N_KV, HEAD_DIM)[:, -WINDOW:]
    ks4 = jnp.concatenate([cache_k_win, ks.reshape(bs, ts, N_KV, HEAD_DIM)], axis=1)[:, -WINDOW:]
    vs4 = jnp.concatenate([cache_v_win, vs.reshape(bs, ts, N_KV, HEAD_DIM)], axis=1)[:, -WINDOW:]
    return (y_p.reshape(bp, tp, D_MODEL), y_s.reshape(bs, ts, D_MODEL),
            jnp.transpose(conv_p, (1, 0, 2))[None], h_p[None], kp4, vp4,
            jnp.transpose(conv_s, (1, 0, 2))[None], h_s[None], ks4, vs4)
```

```python
import functools

import jax
import jax.numpy as jnp
from jax import lax
from jax.experimental import pallas as pl
from jax.experimental.pallas import tpu as pltpu

D_MODEL = 1024
D_RNN = 1024
GATE_BLOCK = 256
N_GATE_BLOCKS = D_RNN // GATE_BLOCK
CONV_W = 4
LRU_C = 8.0
HEAD_DIM = 64
N_HEADS = 16
N_KV = 2
GROUP = N_HEADS // N_KV
CHUNK = 64
WINDOW = 128
EPS = 1e-6
NEG = -1e30
LOG2_E = 1.4426950408889634
LN_2 = 0.6931471805599453

SUBLANES = 8
LANES = 128
BF16_SUBLANES = 16
MXU_N = 256
VMEM_LIMIT_BYTES = 56 * 1024 * 1024

F32 = jnp.float32
BF16 = jnp.bfloat16

KV_W = N_KV * HEAD_DIM
PAIR_W = 2 * HEAD_DIM
N_PAIRS = GROUP // 2
LOOKAHEAD = 3

NSEQ = SUBLANES
A_MAX_STEPS = 128
CONV_PRE = (CONV_W - 1) * NSEQ
OUT_K_SPLIT = D_RNN // 2


def _silu_of_half(h):
    return h * jnp.tanh(h) + h


def _norm_rows_to(x_ref, g_ref, dst_ref, n_rows, blk, r0=0):
    g = g_ref[...]
    for r in range(r0, r0 + n_rows, blk):
        x = x_ref[r:r + blk, :]
        rs = lax.rsqrt(jnp.mean(x * x, axis=-1, keepdims=True) + EPS)
        dst_ref[r:r + blk, :] = (x_ref[r:r + blk, :] * rs * g).astype(BF16)


def _layer_a_kernel(x_hbm, conv_in_ref, h_in_ref,
                    norm_a_ref, w_in_ref, conv_w_ref, conv_b_ref, w_gate_ref, b_gx_ref, b_ga_ref,
                    lam_ref, w_out_ref, norm_kv_ref, w_kv_ref, k_norm_ref,
                    x1_hbm, k_hbm, v_hbm, conv_out_ref, h_out_ref,
                    xin, x1o, ko, vo, sem_in, sem_out,
                    xn_scr, xbp_scr, gate_scr, xc_scr, xcb_scr, gx_scr, ga_scr, hy_scr, h_scr,
                    *, a_steps):
    a_rows = a_steps * NSEQ
    g = pl.program_id(0)
    t = pl.program_id(1)
    nt = pl.num_programs(1)
    total = pl.num_programs(0) * nt
    i = g * nt + t
    slot = lax.rem(i, 2)

    def in_copies(step, sl):
        gg = step // nt
        tt = lax.rem(step, nt)
        return [pltpu.make_async_copy(x_hbm.at[gg * NSEQ + j, pl.ds(tt * a_steps, a_steps), :],
                                      xin.at[sl, :, j, :], sem_in.at[sl]) for j in range(NSEQ)]

    def out_copies(step, sl):
        gg = step // nt
        tt = lax.rem(step, nt)
        cps = []
        for buf, dst in ((x1o, x1_hbm), (ko, k_hbm), (vo, v_hbm)):
            cps += [pltpu.make_async_copy(buf.at[sl, :, j, :],
                                          dst.at[gg * NSEQ + j, pl.ds(tt * a_steps, a_steps), :],
                                          sem_out.at[sl]) for j in range(NSEQ)]
        return cps

    @pl.when(i == 0)
    def _():
        for c in in_copies(i, slot):
            c.start()

    @pl.when(i + 1 < total)
    def _():
        for c in in_copies(i + 1, 1 - slot):
            c.start()

    @pl.when(t == 0)
    def _():
        xbp_scr[0:CONV_PRE, :] = conv_in_ref[...].reshape(CONV_PRE, D_RNN)
        h_scr[...] = h_in_ref[...]

    @pl.when(i >= 2)
    def _():
        for c in out_copies(i - 2, slot):
            c.wait()

    for c in in_copies(i, slot):
        c.wait()

    x_ref = xin.at[slot].reshape(a_rows, D_MODEL)
    x1_ref = x1o.at[slot].reshape(a_rows, D_MODEL)
    k_ref = ko.at[slot].reshape(a_rows, KV_W)
    v_ref = vo.at[slot].reshape(a_rows, KV_W)

    _norm_rows_to(x_ref, norm_a_ref, xn_scr, a_rows, 128)
    lam = lam_ref[...]
    log_sig_lam = jnp.minimum(lam, 0.0) - jnp.log1p(jnp.exp(-jnp.abs(lam)))
    kk = jnp.broadcast_to((0.5 * LRU_C * LOG2_E) * log_sig_lam, (NSEQ, D_RNN))
    b_gx = jnp.broadcast_to(b_gx_ref[...], (NSEQ, D_RNN))
    b_ga = jnp.broadcast_to(b_ga_ref[...], (NSEQ, D_RNN))

    cw = conv_w_ref[...]
    cb = conv_b_ref[...]
    conv_blk = 32

    def in_proj_x(n):
        cols = slice(n * GATE_BLOCK, (n + 1) * GATE_BLOCK)
        xbp_scr[CONV_PRE:, cols] = jnp.dot(xn_scr[...], w_in_ref[:, cols],
                                           preferred_element_type=F32)

    def in_proj_gate(n):
        cols = slice(n * GATE_BLOCK, (n + 1) * GATE_BLOCK)
        gate_scr[:, cols] = jnp.dot(
            xn_scr[...], w_in_ref[:, D_RNN + n * GATE_BLOCK:D_RNN + (n + 1) * GATE_BLOCK],
            preferred_element_type=F32)

    def conv_and_gates(n):
        cols = slice(n * GATE_BLOCK, (n + 1) * GATE_BLOCK)
        for r in range(0, a_rows, conv_blk):
            acc = cb[:, cols]
            for j in range(CONV_W):
                acc = acc + cw[j:j + 1, cols] * xbp_scr[r + j * NSEQ:r + j * NSEQ + conv_blk, cols]
            xc_scr[r:r + conv_blk, cols] = acc
            xcb_scr[r:r + conv_blk, cols] = acc.astype(BF16)
        tail = xbp_scr[a_rows:a_rows + CONV_PRE, cols]
        conv_out_ref[:, :, cols] = tail.reshape(CONV_W - 1, NSEQ, GATE_BLOCK)
        xbp_scr[0:CONV_PRE, cols] = tail

        res = jnp.dot(xcb_scr[:, cols], w_gate_ref[n], preferred_element_type=F32)
        gx_scr[:, cols] = res[:, :GATE_BLOCK]
        ga_scr[:, cols] = res[:, GATE_BLOCK:]

    def scan(n):
        cols = slice(n * GATE_BLOCK, (n + 1) * GATE_BLOCK)
        kk_n = kk[:, cols]
        bgx_n = b_gx[:, cols]
        bga_n = b_ga[:, cols]
        h = h_scr[:, cols]
        for r in range(0, a_rows, BF16_SUBLANES):
            hy = []
            for rr in (r, r + NSEQ):
                t_i = jnp.tanh(gx_scr[rr:rr + NSEQ, cols] + bgx_n)
                t_r = jnp.tanh(ga_scr[rr:rr + NSEQ, cols] + bga_n)
                log2_a = t_r * kk_n + kk_n
                a = jnp.exp2(log2_a)
                y = jnp.tanh(log2_a * (-LN_2)) * (a * a + 1.0)
                mult = jnp.where(y > 0.0, y * lax.rsqrt(y), 0.0)
                u = mult * xc_scr[rr:rr + NSEQ, cols]
                h = a * h + u * (0.5 * t_i + 0.5)
                hy.append(h * _silu_of_half(gate_scr[rr:rr + NSEQ, cols]))
            hy_scr[r:r + BF16_SUBLANES, cols] = jnp.concatenate(hy, axis=0).astype(BF16)
        h_scr[:, cols] = h
        h_out_ref[:, cols] = h

    def out_proj_first_part():
        for nn in range(D_MODEL // MXU_N):
            cols = slice(nn * MXU_N, (nn + 1) * MXU_N)
            x1_ref[:, cols] = x_ref[:, cols] + jnp.dot(
                hy_scr[:, :OUT_K_SPLIT], w_out_ref[:OUT_K_SPLIT, cols],
                preferred_element_type=F32)

    in_proj_x(0)
    in_proj_gate(0)
    in_proj_x(1)
    conv_and_gates(0)
    in_proj_gate(1)
    for n in range(N_GATE_BLOCKS):
        if n + 2 < N_GATE_BLOCKS:
            in_proj_x(n + 2)
        if n + 1 < N_GATE_BLOCKS:
            conv_and_gates(n + 1)
        if n + 2 < N_GATE_BLOCKS:
            in_proj_gate(n + 2)
        if (n + 1) * GATE_BLOCK == D_RNN - GATE_BLOCK:
            out_proj_first_part()
        scan(n)

    half_rows = a_rows // 2
    lo = lax.broadcasted_iota(jnp.int32, (half_rows, KV_W), 1) < HEAD_DIM
    for rh in range(2):
        rows = slice(rh * half_rows, (rh + 1) * half_rows)
        for n in range(D_MODEL // MXU_N):
            cols = slice(n * MXU_N, (n + 1) * MXU_N)
            x1_ref[rows, cols] = x1_ref[rows, cols] + jnp.dot(
                hy_scr[rows, OUT_K_SPLIT:], w_out_ref[OUT_K_SPLIT:, cols],
                preferred_element_type=F32)
        _norm_rows_to(x1_ref, norm_kv_ref, xn_scr, half_rows, 64, r0=rh * half_rows)
        kv = jnp.dot(xn_scr[rows, :], w_kv_ref[...], preferred_element_type=F32)
        k = kv[:, :KV_W]
        k2 = k * k
        ms_lo = jnp.sum(jnp.where(lo, k2, 0.0), axis=-1, keepdims=True) * (1.0 / HEAD_DIM)
        ms_hi = jnp.sum(jnp.where(lo, 0.0, k2), axis=-1, keepdims=True) * (1.0 / HEAD_DIM)
        rs = jnp.where(lo, lax.rsqrt(ms_lo + EPS), lax.rsqrt(ms_hi + EPS))
        k_ref[rows, :] = k * rs * k_norm_ref[...]
        v_ref[rows, :] = kv[:, KV_W:]

    for c in out_copies(i, slot):
        c.start()

    @pl.when(i == total - 1)
    def _():
        @pl.when(i >= 1)
        def _():
            for c in out_copies(i - 1, 1 - slot):
                c.wait()
        for c in out_copies(i, slot):
            c.wait()


def _layer_a(x, conv_in, h_in, p, name):
    n_seq, t_len, _ = x.shape
    a_steps = min(t_len, A_MAX_STEPS)
    a_rows = a_steps * NSEQ
    assert n_seq % NSEQ == 0 and t_len % a_steps == 0
    grid = (n_seq // NSEQ, t_len // a_steps)
    const2 = lambda g, t: (0, 0)
    const3 = lambda g, t: (0, 0, 0)
    any_spec = pl.BlockSpec(memory_space=pl.ANY)

    weights = [p["norm_a"], p["w_in_a"], p["conv_w"], p["conv_b"], p["w_gate"], p["b_gx"],
               p["b_ga"], p["lam"], p["w_out_a"], p["norm_kv"], p["w_kv"], p["k_norm2"]]
    w_specs = [pl.BlockSpec(w.shape, const3 if w.ndim == 3 else const2,
                            pipeline_mode=pl.Buffered(1)) for w in weights]
    conv_spec = pl.BlockSpec((CONV_W - 1, NSEQ, D_RNN), lambda g, t: (0, g, 0))
    h_spec = pl.BlockSpec((NSEQ, D_RNN), lambda g, t: (g, 0))

    out_shape = [
        jax.ShapeDtypeStruct((n_seq, t_len, D_MODEL), F32),
        jax.ShapeDtypeStruct((n_seq, t_len, KV_W), F32),
        jax.ShapeDtypeStruct((n_seq, t_len, KV_W), F32),
        jax.ShapeDtypeStruct((CONV_W - 1, n_seq, D_RNN), F32),
        jax.ShapeDtypeStruct((n_seq, D_RNN), F32),
    ]
    scratch = [
        pltpu.VMEM((2, a_steps, NSEQ, D_MODEL), F32),
        pltpu.VMEM((2, a_steps, NSEQ, D_MODEL), F32),
        pltpu.VMEM((2, a_steps, NSEQ, KV_W), F32),
        pltpu.VMEM((2, a_steps, NSEQ, KV_W), F32),
        pltpu.SemaphoreType.DMA((2,)),
        pltpu.SemaphoreType.DMA((2,)),
        pltpu.VMEM((a_rows, D_MODEL), BF16),
        pltpu.VMEM((CONV_PRE + a_rows, D_RNN), F32),
        pltpu.VMEM((a_rows, D_RNN), F32),
        pltpu.VMEM((a_rows, D_RNN), F32),
        pltpu.VMEM((a_rows, D_RNN), BF16),
        pltpu.VMEM((a_rows, D_RNN), F32),
        pltpu.VMEM((a_rows, D_RNN), F32),
        pltpu.VMEM((a_rows, D_RNN), BF16),
        pltpu.VMEM((NSEQ, D_RNN), F32),
    ]
    return pl.pallas_call(
        functools.partial(_layer_a_kernel, a_steps=a_steps),
        grid=grid,
        in_specs=[any_spec, conv_spec, h_spec] + w_specs,
        out_specs=[any_spec, any_spec, any_spec, conv_spec, h_spec],
        out_shape=out_shape,
        scratch_shapes=scratch,
        compiler_params=pltpu.CompilerParams(
            dimension_semantics=("arbitrary", "arbitrary"), vmem_limit_bytes=VMEM_LIMIT_BYTES),
        name=name,
    )(x, conv_in, h_in, *weights)


def _layer_b_kernel(x1_ref, kh_ref, kc_ref, vh_ref, vc_ref, norm_b_ref, w_in_ref,
                    qg_ref, sinks_ref, w_out_ref, y_ref,
                    xn_scr, gate_scr, qh_scr, kmat_scr, vmat_scr, o_scr,
                    *, nseg, seg, mask_first_halo):
    m = nseg * seg
    n_chunks = seg // CHUNK

    _norm_rows_to(x1_ref, norm_b_ref, xn_scr, m, 128)

    lo_m = lax.broadcasted_iota(jnp.int32, (m, LANES), 1) < HEAD_DIM
    qg = qg_ref[...]
    for n in range(D_MODEL // MXU_N):
        qn = jnp.dot(xn_scr[...], w_in_ref[:, n * MXU_N:(n + 1) * MXU_N],
                     preferred_element_type=F32)
        for cc in range(MXU_N // LANES):
            col = n * MXU_N + cc * LANES
            x = qn[:, cc * LANES:(cc + 1) * LANES]
            sq = x * x
            ms_lo = jnp.sum(jnp.where(lo_m, sq, 0.0), axis=-1, keepdims=True) * (1.0 / HEAD_DIM)
            ms_hi = jnp.sum(jnp.where(lo_m, 0.0, sq), axis=-1, keepdims=True) * (1.0 / HEAD_DIM)
            rs = jnp.where(lo_m, lax.rsqrt(ms_lo + EPS), lax.rsqrt(ms_hi + EPS))
            qh_scr[:, col:col + LANES] = (x * rs * qg[:, col:col + LANES]).astype(BF16)


    lane = lax.broadcasted_iota(jnp.int32, (CHUNK, KV_W), 1)
    lo = lane < HEAD_DIM
    first = pl.program_id(1) == 0 if mask_first_halo else None

    for s in range(nseg):
        for r0, nrows, kval, vval in ((0, WINDOW, kh_ref[s], vh_ref[s]),
                                      (WINDOW, seg, kc_ref[s * seg:(s + 1) * seg, :],
                                       vc_ref[s * seg:(s + 1) * seg, :])):
            is_lo = lax.broadcasted_iota(jnp.int32, kval.shape, 1) < HEAD_DIM
            ind_e = jnp.where(is_lo, 1.0, 0.0).astype(BF16)
            ind_o = jnp.where(is_lo, 0.0, 1.0).astype(BF16)
            rows = slice(r0, r0 + nrows)
            for val, scr in ((kval, kmat_scr), (vval, vmat_scr)):
                swapped = pltpu.roll(val, HEAD_DIM, 1)
                for g in range(N_KV):
                    low = val if g == 0 else swapped
                    high = swapped if g == 0 else val
                    scr[s, g, 0, rows, 0:KV_W] = jnp.where(is_lo, low, 0.0).astype(BF16)
                    scr[s, g, 1, rows, 0:KV_W] = jnp.where(is_lo, 0.0, high).astype(BF16)
            for g in range(N_KV):
                vmat_scr[s, g, 0, rows, KV_W:] = ind_e
                vmat_scr[s, g, 1, rows, KV_W:] = ind_o

    def placed(scr, s, g, k0):
        return jnp.concatenate(
            [scr[s, g, 0, k0:k0 + WINDOW, :], scr[s, g, 1, k0:k0 + WINDOW, :],
             scr[s, g, 0, k0 + WINDOW:k0 + WINDOW + CHUNK, :],
             scr[s, g, 1, k0 + WINDOW:k0 + WINDOW + CHUNK, :]], axis=0)

    def scores(s, c, g):
        q0 = s * seg + c * CHUNK
        qst = jnp.concatenate(
            [qh_scr[q0:q0 + CHUNK, g * GROUP * HEAD_DIM + j * PAIR_W:
                    g * GROUP * HEAD_DIM + (j + 1) * PAIR_W] for j in range(N_PAIRS)],
            axis=0)
        return lax.dot_general(qst, placed(kmat_scr, s, g, c * CHUNK),
                               (((1,), (1,)), ((), ())),
                               preferred_element_type=F32)

    units = [(s, c, g) for s in range(nseg) for c in range(n_chunks) for g in range(N_KV)]
    row_split = 2 if m >= 2 * MXU_N else 1
    gate_pieces = [(n, rh) for n in range(D_MODEL // MXU_N) for rh in range(row_split)]
    assert len(gate_pieces) <= len(units)

    def gate_piece():
        n, rh = gate_pieces.pop(0)
        rows = slice(rh * (m // row_split), (rh + 1) * (m // row_split))
        gate_scr[rows, n * MXU_N:(n + 1) * MXU_N] = jnp.dot(
            xn_scr[rows, :], w_in_ref[:, D_MODEL + n * MXU_N:D_MODEL + (n + 1) * MXU_N],
            preferred_element_type=F32)

    for _ in range(row_split):
        gate_piece()
    pending = [scores(*u) for u in units[:LOOKAHEAD]]
    for k, (s, c, g) in enumerate(units):
        sc = pending.pop(0)
        if k + LOOKAHEAD < len(units):
            pending.append(scores(*units[k + LOOKAHEAD]))
        if gate_pieces:
            gate_piece()
        p_rows, sink_rows = [], []
        for j in range(N_PAIRS):
            sj = sc[j * CHUNK:(j + 1) * CHUNK]
            c0 = sj[:, 0:KV_W]
            c1 = sj[:, KV_W:2 * KV_W]
            c2 = sj[:, 2 * KV_W:]
            if mask_first_halo and c * CHUNK < WINDOW:
                n_bad = jnp.where(first, WINDOW - c * CHUNK, 0)
                bad = lane < n_bad
                c0 = jnp.where(bad, NEG, c0)
                c1 = jnp.where(bad, NEG, c1)
            sink_e = sinks_ref[g * GROUP + 2 * j] * LOG2_E
            sink_o = sinks_ref[g * GROUP + 2 * j + 1] * LOG2_E
            m_e = jnp.maximum(jnp.max(jnp.maximum(c0, jnp.where(lo, c2, NEG)),
                                      axis=-1, keepdims=True), sink_e)
            m_o = jnp.maximum(jnp.max(jnp.maximum(c1, jnp.where(lo, NEG, c2)),
                                      axis=-1, keepdims=True), sink_o)
            p_rows.append(jnp.concatenate(
                [jnp.exp2(c0 - m_e), jnp.exp2(c1 - m_o),
                 jnp.exp2(c2 - jnp.where(lo, m_e, m_o))], axis=1).astype(BF16))
            sink_rows.append(jnp.where(lo, jnp.exp2(sink_e - m_e), jnp.exp2(sink_o - m_o)))
        pmat = jnp.concatenate(p_rows, axis=0)
        ov = jnp.dot(pmat, placed(vmat_scr, s, g, c * CHUNK),
                     preferred_element_type=F32)
        q0 = s * seg + c * CHUNK
        for j in range(N_PAIRS):
            rows = slice(j * CHUNK, (j + 1) * CHUNK)
            c_lo = g * GROUP * HEAD_DIM + j * PAIR_W
            o_scr[q0:q0 + CHUNK, c_lo:c_lo + PAIR_W] = (
                ov[rows, :KV_W] / (ov[rows, KV_W:] + sink_rows[j]))
    assert not gate_pieces

    blk = 32
    for r in range(0, m, blk):
        xn_scr[r:r + blk, :] = (
            o_scr[r:r + blk, :] * _silu_of_half(gate_scr[r:r + blk, :])).astype(BF16)
    for n in range(D_MODEL // MXU_N):
        cols = slice(n * MXU_N, (n + 1) * MXU_N)
        y_ref[:, cols] = x1_ref[:, cols] + jnp.dot(
            xn_scr[...], w_out_ref[:, cols], preferred_element_type=F32)


def _layer_b(x1, k_halo, v_halo, k_cur, v_cur, p, *, n_batch, t_len, nseg, seg, prompt):
    m = nseg * seg
    rows = n_batch * t_len
    const2 = lambda *_: (0, 0)
    if prompt:
        steps = t_len // m
        per_b = t_len // WINDOW
        grid = (n_batch, steps)
        row_map = lambda b, t: (b * steps + t, 0)
        halo_map = lambda b, t: (b * per_b + jnp.maximum(t * (m // WINDOW) - 1, 0), 0, 0)
    else:
        grid = (n_batch // nseg, 1)
        row_map = lambda i, t: (i, 0)
        halo_map = lambda i, t: (i, 0, 0)

    weights = [p["norm_b"], p["w_in_b"], p["qg"]]
    in_specs = [
        pl.BlockSpec((m, D_MODEL), row_map),
        pl.BlockSpec((nseg, WINDOW, KV_W), halo_map),
        pl.BlockSpec((m, KV_W), row_map),
        pl.BlockSpec((nseg, WINDOW, KV_W), halo_map),
        pl.BlockSpec((m, KV_W), row_map),
    ] + [pl.BlockSpec(w.shape, const2) for w in weights] + [
        pl.BlockSpec(memory_space=pltpu.SMEM),
        pl.BlockSpec(p["w_out_b"].shape, const2),
    ]
    kv_rows = WINDOW + seg
    scratch = [
        pltpu.VMEM((m, D_MODEL), BF16),
        pltpu.VMEM((m, D_MODEL), F32),
        pltpu.VMEM((m, D_MODEL), BF16),
        pltpu.VMEM((nseg, N_KV, 2, kv_rows, KV_W), BF16),
        pltpu.VMEM((nseg, N_KV, 2, kv_rows, 2 * KV_W), BF16),
        pltpu.VMEM((m, D_MODEL), F32),
    ]
    return pl.pallas_call(
        functools.partial(_layer_b_kernel, nseg=nseg, seg=seg, mask_first_halo=prompt),
        grid=grid,
        in_specs=in_specs,
        out_specs=pl.BlockSpec((m, D_MODEL), row_map),
        out_shape=jax.ShapeDtypeStruct((rows, D_MODEL), F32),
        scratch_shapes=scratch,
        compiler_params=pltpu.CompilerParams(
            dimension_semantics=("arbitrary", "arbitrary"), vmem_limit_bytes=VMEM_LIMIT_BYTES),
        name="layer_b_prompt" if prompt else "layer_b_sample",
    )(x1, k_halo, k_cur, v_halo, v_cur, *weights, p["sinks"], p["w_out_b"])


def _prep_params(norm_a, w_in_a, conv_w, conv_b, w_gate_x, b_gate_x, w_gate_a, b_gate_a,
                 lru_lambda, w_out_a, norm_kv, w_kv, k_norm, norm_b, w_in_b, q_norm, sinks,
                 w_out_b):
    row = lambda v: v.reshape(1, -1).astype(F32)
    in_scale = jnp.concatenate([jnp.ones((D_MODEL,), F32), jnp.full((D_MODEL,), 0.5, F32)])
    return {
        "norm_a": row(norm_a[0]),
        "w_in_a": (w_in_a[0] * in_scale).astype(BF16),
        "conv_w": conv_w[0].astype(F32),
        "conv_b": row(conv_b[0]),
        "w_gate": (0.5 * jnp.concatenate([w_gate_x[0], w_gate_a[0]], axis=-1)).astype(BF16),
        "b_gx": row(0.5 * b_gate_x[0]),
        "b_ga": row(0.5 * b_gate_a[0]),
        "lam": row(lru_lambda[0]),
        "w_out_a": w_out_a[0].astype(BF16),
        "norm_kv": row(norm_kv),
        "w_kv": w_kv.astype(BF16),
        "k_norm2": row(jnp.tile(k_norm, N_KV)),
        "norm_b": row(norm_b[0]),
        "w_in_b": (w_in_b[0] * in_scale).astype(BF16),
        "qg": row(jnp.tile(q_norm[0], N_HEADS) * (HEAD_DIM ** -0.5 * LOG2_E)),
        "sinks": sinks[0].astype(F32),
        "w_out_b": w_out_b[0].astype(BF16),
    }


def kernel(x_prompt, x_sample, state_conv, state_rglru, cache_k_win, cache_v_win, norm_a, w_in_a, conv_w, conv_b, w_gate_x, b_gate_x, w_gate_a, b_gate_a, lru_lambda, w_out_a, norm_kv, w_kv, k_norm, norm_b, w_in_b, q_norm, sinks, w_out_b):
    assert norm_a.shape[0] == 1 and norm_b.shape[0] == 1, "one recurrent + one attention layer"
    p = _prep_params(norm_a, w_in_a, conv_w, conv_b, w_gate_x, b_gate_x, w_gate_a, b_gate_a,
                     lru_lambda, w_out_a, norm_kv, w_kv, k_norm, norm_b, w_in_b, q_norm, sinks,
                     w_out_b)
    bp, tp, _ = x_prompt.shape
    bs, ts, _ = x_sample.shape

    x1p, kp, vp, conv_p, h_p = _layer_a(
        x_prompt, jnp.zeros((CONV_W - 1, bp, D_RNN), F32), jnp.zeros((bp, D_RNN), F32), p,
        "layer_a_prompt")
    tile_b = 512
    kp2 = kp.reshape(bp * tp, KV_W)
    vp2 = vp.reshape(bp * tp, KV_W)
    kp3 = kp.reshape(bp * tp // WINDOW, WINDOW, KV_W)
    vp3 = vp.reshape(bp * tp // WINDOW, WINDOW, KV_W)
    y_p = _layer_b(x1p.reshape(bp * tp, D_MODEL), kp3, vp3, kp2, vp2, p, n_batch=bp, t_len=tp,
                   nseg=1, seg=tile_b, prompt=True)

    x1s, ks, vs, conv_s, h_s = _layer_a(
        x_sample, jnp.transpose(state_conv[0], (1, 0, 2)), state_rglru[0], p, "layer_a_sample")
    y_s = _layer_b(x1s.reshape(bs * ts, D_MODEL), cache_k_win.reshape(bs, WINDOW, KV_W),
                   cache_v_win.reshape(bs, WINDOW, KV_W), ks.reshape(bs * ts, KV_W),
                   vs.reshape(bs * ts, KV_W), p, n_batch=bs, t_len=ts, nseg=8, seg=ts,
                   prompt=False)

    kp4 = kp.reshape(bp, tp, N_KV, HEAD_DIM)[:, -WINDOW:]
    vp4 = vp.reshape(bp, tp, N_KV, HEAD_DIM)[:, -WINDOW:]
    ks4 = jnp.concatenate([cache_k_win, ks.reshape(bs, ts, N_KV, HEAD_DIM)], axis=1)[:, -WINDOW:]
    vs4 = jnp.concatenate([cache_v_win, vs.reshape(bs, ts, N_KV, HEAD_DIM)], axis=1)[:, -WINDOW:]
    return (y_p.reshape(bp, tp, D_MODEL), y_s.reshape(bs, ts, D_MODEL),
            jnp.transpose(conv_p, (1, 0, 2))[None], h_p[None], kp4, vp4,
            jnp.transpose(conv_s, (1, 0, 2))[None], h_s[None], ks4, vs4)
```

```python
import functools

import jax
import jax.numpy as jnp
from jax import lax
from jax.experimental import pallas as pl
from jax.experimental.pallas import tpu as pltpu

D_MODEL = 1024
D_RNN = 1024
GATE_BLOCK = 256
N_GATE_BLOCKS = D_RNN // GATE_BLOCK
CONV_W = 4
LRU_C = 8.0
HEAD_DIM = 64
N_HEADS = 16
N_KV = 2
GROUP = N_HEADS // N_KV
CHUNK = 64
WINDOW = 128
EPS = 1e-6
NEG = -1e30
LOG2_E = 1.4426950408889634
LN_2 = 0.6931471805599453

SUBLANES = 8
LANES = 128
BF16_SUBLANES = 16
MXU_N = 256
VMEM_LIMIT_BYTES = 56 * 1024 * 1024

F32 = jnp.float32
BF16 = jnp.bfloat16

KV_W = N_KV * HEAD_DIM
PAIR_W = 2 * HEAD_DIM
N_PAIRS = GROUP // 2
LOOKAHEAD = 2
B_PART_ROWS = 512

NSEQ = SUBLANES
A_MAX_STEPS = 128
CONV_PRE = (CONV_W - 1) * NSEQ
OUT_K_SPLIT = D_RNN // 2


def _silu_of_half(h):
    return h * jnp.tanh(h) + h


def _norm_rows_to(x_ref, g_ref, dst_ref, n_rows, blk, r0=0):
    g = g_ref[...]
    for r in range(r0, r0 + n_rows, blk):
        x = x_ref[r:r + blk, :]
        rs = lax.rsqrt(jnp.mean(x * x, axis=-1, keepdims=True) + EPS)
        dst_ref[r:r + blk, :] = (x_ref[r:r + blk, :] * rs * g).astype(BF16)


def _layer_a_kernel(x_hbm, conv_in_ref, h_in_ref,
                    norm_a_ref, w_in_ref, conv_w_ref, conv_b_ref, w_gate_ref, b_gx_ref, b_ga_ref,
                    lam_ref, w_out_ref, norm_kv_ref, w_kv_ref, k_norm_ref,
                    x1_hbm, k_hbm, v_hbm, conv_out_ref, h_out_ref,
                    xin, x1o, ko, vo, sem_in, sem_out,
                    xn_scr, xbp_scr, gate_scr, xc_scr, xcb_scr, gx_scr, ga_scr, hy_scr, h_scr,
                    *, a_steps):
    a_rows = a_steps * NSEQ
    g = pl.program_id(0)
    t = pl.program_id(1)
    nt = pl.num_programs(1)
    total = pl.num_programs(0) * nt
    i = g * nt + t
    slot = lax.rem(i, 2)

    def in_copies(step, sl):
        gg = step // nt
        tt = lax.rem(step, nt)
        return [pltpu.make_async_copy(x_hbm.at[gg * NSEQ + j, pl.ds(tt * a_steps, a_steps), :],
                                      xin.at[sl, :, j, :], sem_in.at[sl]) for j in range(NSEQ)]

    def out_copies(step, sl):
        gg = step // nt
        tt = lax.rem(step, nt)
        cps = []
        for buf, dst in ((x1o, x1_hbm), (ko, k_hbm), (vo, v_hbm)):
            cps += [pltpu.make_async_copy(buf.at[sl, :, j, :],
                                          dst.at[gg * NSEQ + j, pl.ds(tt * a_steps, a_steps), :],
                                          sem_out.at[sl]) for j in range(NSEQ)]
        return cps

    @pl.when(i == 0)
    def _():
        for c in in_copies(i, slot):
            c.start()

    @pl.when(i + 1 < total)
    def _():
        for c in in_copies(i + 1, 1 - slot):
            c.start()

    @pl.when(t == 0)
    def _():
        xbp_scr[0:CONV_PRE, :] = conv_in_ref[...].reshape(CONV_PRE, D_RNN)
        h_scr[...] = h_in_ref[...]

    @pl.when(i >= 2)
    def _():
        for c in out_copies(i - 2, slot):
            c.wait()

    for c in in_copies(i, slot):
        c.wait()

    x_ref = xin.at[slot].reshape(a_rows, D_MODEL)
    x1_ref = x1o.at[slot].reshape(a_rows, D_MODEL)
    k_ref = ko.at[slot].reshape(a_rows, KV_W)
    v_ref = vo.at[slot].reshape(a_rows, KV_W)

    _norm_rows_to(x_ref, norm_a_ref, xn_scr, a_rows, 128)
    lam = lam_ref[...]
    log_sig_lam = jnp.minimum(lam, 0.0) - jnp.log1p(jnp.exp(-jnp.abs(lam)))
    kk = jnp.broadcast_to((0.5 * LRU_C * LOG2_E) * log_sig_lam, (NSEQ, D_RNN))
    b_gx = jnp.broadcast_to(b_gx_ref[...], (NSEQ, D_RNN))
    b_ga = jnp.broadcast_to(b_ga_ref[...], (NSEQ, D_RNN))

    cw = conv_w_ref[...]
    cb = conv_b_ref[...]
    conv_blk = 32

    def in_proj_x(n):
        cols = slice(n * GATE_BLOCK, (n + 1) * GATE_BLOCK)
        xbp_scr[CONV_PRE:, cols] = jnp.dot(xn_scr[...], w_in_ref[:, cols],
                                           preferred_element_type=F32)

    def in_proj_gate(n):
        cols = slice(n * GATE_BLOCK, (n + 1) * GATE_BLOCK)
        gate_scr[:, cols] = jnp.dot(
            xn_scr[...], w_in_ref[:, D_RNN + n * GATE_BLOCK:D_RNN + (n + 1) * GATE_BLOCK],
            preferred_element_type=F32)

    def conv_and_gates(n):
        cols = slice(n * GATE_BLOCK, (n + 1) * GATE_BLOCK)
        for r in range(0, a_rows, conv_blk):
            acc = cb[:, cols]
            for j in range(CONV_W):
                acc = acc + cw[j:j + 1, cols] * xbp_scr[r + j * NSEQ:r + j * NSEQ + conv_blk, cols]
            xc_scr[r:r + conv_blk, cols] = acc
            xcb_scr[r:r + conv_blk, cols] = acc.astype(BF16)
        tail = xbp_scr[a_rows:a_rows + CONV_PRE, cols]
        conv_out_ref[:, :, cols] = tail.reshape(CONV_W - 1, NSEQ, GATE_BLOCK)
        xbp_scr[0:CONV_PRE, cols] = tail

        res = jnp.dot(xcb_scr[:, cols], w_gate_ref[n], preferred_element_type=F32)
        gx_scr[:, cols] = res[:, :GATE_BLOCK]
        ga_scr[:, cols] = res[:, GATE_BLOCK:]

    def scan(n):
        cols = slice(n * GATE_BLOCK, (n + 1) * GATE_BLOCK)
        kk_n = kk[:, cols]
        bgx_n = b_gx[:, cols]
        bga_n = b_ga[:, cols]
        h = h_scr[:, cols]
        for r in range(0, a_rows, BF16_SUBLANES):
            hy = []
            for rr in (r, r + NSEQ):
                t_i = jnp.tanh(gx_scr[rr:rr + NSEQ, cols] + bgx_n)
                t_r = jnp.tanh(ga_scr[rr:rr + NSEQ, cols] + bga_n)
                log2_a = t_r * kk_n + kk_n
                a = jnp.exp2(log2_a)
                y = jnp.tanh(log2_a * (-LN_2)) * (a * a + 1.0)
                mult = jnp.where(y > 0.0, y * lax.rsqrt(y), 0.0)
                u = mult * xc_scr[rr:rr + NSEQ, cols]
                h = a * h + u * (0.5 * t_i + 0.5)
                hy.append(h * _silu_of_half(gate_scr[rr:rr + NSEQ, cols]))
            hy_scr[r:r + BF16_SUBLANES, cols] = jnp.concatenate(hy, axis=0).astype(BF16)
        h_scr[:, cols] = h
        h_out_ref[:, cols] = h

    def out_proj_first_part():
        for nn in range(D_MODEL // MXU_N):
            cols = slice(nn * MXU_N, (nn + 1) * MXU_N)
            x1_ref[:, cols] = x_ref[:, cols] + jnp.dot(
                hy_scr[:, :OUT_K_SPLIT], w_out_ref[:OUT_K_SPLIT, cols],
                preferred_element_type=F32)

    in_proj_x(0)
    in_proj_gate(0)
    in_proj_x(1)
    conv_and_gates(0)
    in_proj_gate(1)
    for n in range(N_GATE_BLOCKS):
        if n + 2 < N_GATE_BLOCKS:
            in_proj_x(n + 2)
        if n + 1 < N_GATE_BLOCKS:
            conv_and_gates(n + 1)
        if n + 2 < N_GATE_BLOCKS:
            in_proj_gate(n + 2)
        if (n + 1) * GATE_BLOCK == D_RNN - GATE_BLOCK:
            out_proj_first_part()
        scan(n)

    half_rows = a_rows // 2
    lo = lax.broadcasted_iota(jnp.int32, (half_rows, KV_W), 1) < HEAD_DIM
    for rh in range(2):
        rows = slice(rh * half_rows, (rh + 1) * half_rows)
        for n in range(D_MODEL // MXU_N):
            cols = slice(n * MXU_N, (n + 1) * MXU_N)
            x1_ref[rows, cols] = x1_ref[rows, cols] + jnp.dot(
                hy_scr[rows, OUT_K_SPLIT:], w_out_ref[OUT_K_SPLIT:, cols],
                preferred_element_type=F32)
        _norm_rows_to(x1_ref, norm_kv_ref, xn_scr, half_rows, 64, r0=rh * half_rows)
        kv = jnp.dot(xn_scr[rows, :], w_kv_ref[...], preferred_element_type=F32)
        k = kv[:, :KV_W]
        k2 = k * k
        ms_lo = jnp.sum(jnp.where(lo, k2, 0.0), axis=-1, keepdims=True) * (1.0 / HEAD_DIM)
        ms_hi = jnp.sum(jnp.where(lo, 0.0, k2), axis=-1, keepdims=True) * (1.0 / HEAD_DIM)
        rs = jnp.where(lo, lax.rsqrt(ms_lo + EPS), lax.rsqrt(ms_hi + EPS))
        k_ref[rows, :] = k * rs * k_norm_ref[...]
        v_ref[rows, :] = kv[:, KV_W:]

    for c in out_copies(i, slot):
        c.start()

    @pl.when(i == total - 1)
    def _():
        @pl.when(i >= 1)
        def _():
            for c in out_copies(i - 1, 1 - slot):
                c.wait()
        for c in out_copies(i, slot):
            c.wait()


def _layer_a(x, conv_in, h_in, p, name):
    n_seq, t_len, _ = x.shape
    a_steps = min(t_len, A_MAX_STEPS)
    a_rows = a_steps * NSEQ
    assert n_seq % NSEQ == 0 and t_len % a_steps == 0
    grid = (n_seq // NSEQ, t_len // a_steps)
    const2 = lambda g, t: (0, 0)
    const3 = lambda g, t: (0, 0, 0)
    any_spec = pl.BlockSpec(memory_space=pl.ANY)

    weights = [p["norm_a"], p["w_in_a"], p["conv_w"], p["conv_b"], p["w_gate"], p["b_gx"],
               p["b_ga"], p["lam"], p["w_out_a"], p["norm_kv"], p["w_kv"], p["k_norm2"]]
    w_specs = [pl.BlockSpec(w.shape, const3 if w.ndim == 3 else const2,
                            pipeline_mode=pl.Buffered(1)) for w in weights]
    conv_spec = pl.BlockSpec((CONV_W - 1, NSEQ, D_RNN), lambda g, t: (0, g, 0))
    h_spec = pl.BlockSpec((NSEQ, D_RNN), lambda g, t: (g, 0))

    out_shape = [
        jax.ShapeDtypeStruct((n_seq, t_len, D_MODEL), F32),
        jax.ShapeDtypeStruct((n_seq, t_len, KV_W), F32),
        jax.ShapeDtypeStruct((n_seq, t_len, KV_W), F32),
        jax.ShapeDtypeStruct((CONV_W - 1, n_seq, D_RNN), F32),
        jax.ShapeDtypeStruct((n_seq, D_RNN), F32),
    ]
    scratch = [
        pltpu.VMEM((2, a_steps, NSEQ, D_MODEL), F32),
        pltpu.VMEM((2, a_steps, NSEQ, D_MODEL), F32),
        pltpu.VMEM((2, a_steps, NSEQ, KV_W), F32),
        pltpu.VMEM((2, a_steps, NSEQ, KV_W), F32),
        pltpu.SemaphoreType.DMA((2,)),
        pltpu.SemaphoreType.DMA((2,)),
        pltpu.VMEM((a_rows, D_MODEL), BF16),
        pltpu.VMEM((CONV_PRE + a_rows, D_RNN), F32),
        pltpu.VMEM((a_rows, D_RNN), F32),
        pltpu.VMEM((a_rows, D_RNN), F32),
        pltpu.VMEM((a_rows, D_RNN), BF16),
        pltpu.VMEM((a_rows, D_RNN), F32),
        pltpu.VMEM((a_rows, D_RNN), F32),
        pltpu.VMEM((a_rows, D_RNN), BF16),
        pltpu.VMEM((NSEQ, D_RNN), F32),
    ]
    return pl.pallas_call(
        functools.partial(_layer_a_kernel, a_steps=a_steps),
        grid=grid,
        in_specs=[any_spec, conv_spec, h_spec] + w_specs,
        out_specs=[any_spec, any_spec, any_spec, conv_spec, h_spec],
        out_shape=out_shape,
        scratch_shapes=scratch,
        compiler_params=pltpu.CompilerParams(
            dimension_semantics=("arbitrary", "arbitrary"), vmem_limit_bytes=VMEM_LIMIT_BYTES),
        name=name,
    )(x, conv_in, h_in, *weights)


def _layer_b_kernel(x1_ref, kh_ref, kc_ref, vh_ref, vc_ref, norm_b_ref, w_in_ref,
                    qg_ref, sinks_ref, w_out_ref, y_ref,
                    xn_scr, gate_scr, qh_scr, kmat_scr, vmat_scr, o_scr, og_scr,
                    *, nseg, seg, mask_first_halo):
    m = nseg * seg
    n_chunks = seg // CHUNK
    part = min(m, B_PART_ROWS)
    assert m % part == 0 and (part % seg == 0 or seg % part == 0)
    n_parts = m // part
    n_col_blocks = D_MODEL // MXU_N

    _norm_rows_to(x1_ref, norm_b_ref, xn_scr, m, 128)

    lo_m = lax.broadcasted_iota(jnp.int32, (part, LANES), 1) < HEAD_DIM
    qg = qg_ref[...]

    def q_block(h, n):
        rows = slice(h * part, (h + 1) * part)
        qn = jnp.dot(xn_scr[rows, :], w_in_ref[:, n * MXU_N:(n + 1) * MXU_N],
                     preferred_element_type=F32)
        for cc in range(MXU_N // LANES):
            col = n * MXU_N + cc * LANES
            x = qn[:, cc * LANES:(cc + 1) * LANES]
            sq = x * x
            ms_lo = jnp.sum(jnp.where(lo_m, sq, 0.0), axis=-1, keepdims=True) * (1.0 / HEAD_DIM)
            ms_hi = jnp.sum(jnp.where(lo_m, 0.0, sq), axis=-1, keepdims=True) * (1.0 / HEAD_DIM)
            rs = jnp.where(lo_m, lax.rsqrt(ms_lo + EPS), lax.rsqrt(ms_hi + EPS))
            qh_scr[rows, col:col + LANES] = (x * rs * qg[:, col:col + LANES]).astype(BF16)

    row_split = 2 if part >= 2 * MXU_N else 1

    def gate_piece(h, n, rh):
        r0 = h * part + rh * (part // row_split)
        rows = slice(r0, r0 + part // row_split)
        gate_scr[rows, n * MXU_N:(n + 1) * MXU_N] = jnp.dot(
            xn_scr[rows, :], w_in_ref[:, D_MODEL + n * MXU_N:D_MODEL + (n + 1) * MXU_N],
            preferred_element_type=F32)

    def gate_multiply(h):
        blk = 32
        for r in range(h * part, (h + 1) * part, blk):
            og_scr[r:r + blk, :] = (
                o_scr[r:r + blk, :] * _silu_of_half(gate_scr[r:r + blk, :])).astype(BF16)

    def out_piece(h, n):
        rows = slice(h * part, (h + 1) * part)
        cols = slice(n * MXU_N, (n + 1) * MXU_N)
        y_ref[rows, cols] = x1_ref[rows, cols] + jnp.dot(
            og_scr[rows, :], w_out_ref[:, cols], preferred_element_type=F32)

    lane = lax.broadcasted_iota(jnp.int32, (CHUNK, KV_W), 1)
    lo = lane < HEAD_DIM
    first = pl.program_id(1) == 0 if mask_first_halo else None

    for s in range(nseg):
        for r0, nrows, kval, vval in ((0, WINDOW, kh_ref[s], vh_ref[s]),
                                      (WINDOW, seg, kc_ref[s * seg:(s + 1) * seg, :],
                                       vc_ref[s * seg:(s + 1) * seg, :])):
            is_lo = lax.broadcasted_iota(jnp.int32, kval.shape, 1) < HEAD_DIM
            ind_e = jnp.where(is_lo, 1.0, 0.0).astype(BF16)
            ind_o = jnp.where(is_lo, 0.0, 1.0).astype(BF16)
            rows = slice(r0, r0 + nrows)
            for val, scr in ((kval, kmat_scr), (vval, vmat_scr)):
                swapped = pltpu.roll(val, HEAD_DIM, 1)
                for g in range(N_KV):
                    low = val if g == 0 else swapped
                    high = swapped if g == 0 else val
                    scr[s, g, 0, rows, 0:KV_W] = jnp.where(is_lo, low, 0.0).astype(BF16)
                    scr[s, g, 1, rows, 0:KV_W] = jnp.where(is_lo, 0.0, high).astype(BF16)
            for g in range(N_KV):
                vmat_scr[s, g, 0, rows, KV_W:] = ind_e
                vmat_scr[s, g, 1, rows, KV_W:] = ind_o

    def placed(scr, s, g, k0):
        return jnp.concatenate(
            [scr[s, g, 0, k0:k0 + WINDOW, :], scr[s, g, 1, k0:k0 + WINDOW, :],
             scr[s, g, 0, k0 + WINDOW:k0 + WINDOW + CHUNK, :],
             scr[s, g, 1, k0 + WINDOW:k0 + WINDOW + CHUNK, :]], axis=0)

    def scores(s, c, g):
        q0 = s * seg + c * CHUNK
        qst = jnp.concatenate(
            [qh_scr[q0:q0 + CHUNK, g * GROUP * HEAD_DIM + j * PAIR_W:
                    g * GROUP * HEAD_DIM + (j + 1) * PAIR_W] for j in range(N_PAIRS)],
            axis=0)
        return lax.dot_general(qst, placed(kmat_scr, s, g, c * CHUNK),
                               (((1,), (1,)), ((), ())),
                               preferred_element_type=F32)

    def softmax_pv(sc, s, c, g):
        p_rows, sink_rows = [], []
        for j in range(N_PAIRS):
            sj = sc[j * CHUNK:(j + 1) * CHUNK]
            c0 = sj[:, 0:KV_W]
            c1 = sj[:, KV_W:2 * KV_W]
            c2 = sj[:, 2 * KV_W:]
            if mask_first_halo and c * CHUNK < WINDOW:
                n_bad = jnp.where(first, WINDOW - c * CHUNK, 0)
                bad = lane < n_bad
                c0 = jnp.where(bad, NEG, c0)
                c1 = jnp.where(bad, NEG, c1)
            sink_e = sinks_ref[g * GROUP + 2 * j] * LOG2_E
            sink_o = sinks_ref[g * GROUP + 2 * j + 1] * LOG2_E
            m_e = jnp.maximum(jnp.max(jnp.maximum(c0, jnp.where(lo, c2, NEG)),
                                      axis=-1, keepdims=True), sink_e)
            m_o = jnp.maximum(jnp.max(jnp.maximum(c1, jnp.where(lo, NEG, c2)),
                                      axis=-1, keepdims=True), sink_o)
            p_rows.append(jnp.concatenate(
                [jnp.exp2(c0 - m_e), jnp.exp2(c1 - m_o),
                 jnp.exp2(c2 - jnp.where(lo, m_e, m_o))], axis=1).astype(BF16))
            sink_rows.append(jnp.where(lo, jnp.exp2(sink_e - m_e), jnp.exp2(sink_o - m_o)))
        pmat = jnp.concatenate(p_rows, axis=0)
        ov = jnp.dot(pmat, placed(vmat_scr, s, g, c * CHUNK),
                     preferred_element_type=F32)
        q0 = s * seg + c * CHUNK
        for j in range(N_PAIRS):
            rows = slice(j * CHUNK, (j + 1) * CHUNK)
            c_lo = g * GROUP * HEAD_DIM + j * PAIR_W
            o_scr[q0:q0 + CHUNK, c_lo:c_lo + PAIR_W] = (
                ov[rows, :KV_W] / (ov[rows, KV_W:] + sink_rows[j]))

    all_units = [(s, c, g) for s in range(nseg) for c in range(n_chunks) for g in range(N_KV)]

    for n in range(n_col_blocks):
        q_block(0, n)
    for h in range(n_parts):
        units = [u for u in all_units if (u[0] * seg + u[1] * CHUNK) // part == h]
        fillers = []
        if h >= 1:
            fillers += [functools.partial(out_piece, h - 1, n) for n in range(n_col_blocks)]
        gates = [functools.partial(gate_piece, h, n, rh)
                 for n in range(n_col_blocks) for rh in range(row_split)]
        nxt = ([functools.partial(q_block, h + 1, n) for n in range(n_col_blocks)]
               if h + 1 < n_parts else [])
        while gates or nxt:
            fillers += gates[:2]
            gates = gates[2:]
            fillers += nxt[:1]
            nxt = nxt[1:]
        assert len(fillers) <= len(units)

        fillers.pop(0)()
        pending = [scores(*u) for u in units[:LOOKAHEAD]]
        for k, (s, c, g) in enumerate(units):
            sc = pending.pop(0)
            if k + LOOKAHEAD < len(units):
                pending.append(scores(*units[k + LOOKAHEAD]))
            if fillers:
                fillers.pop(0)()
            softmax_pv(sc, s, c, g)
        assert not fillers
        gate_multiply(h)
    for n in range(n_col_blocks):
        out_piece(n_parts - 1, n)


def _layer_b(x1, k_halo, v_halo, k_cur, v_cur, p, *, n_batch, t_len, nseg, seg, prompt):
    m = nseg * seg
    rows = n_batch * t_len
    const2 = lambda *_: (0, 0)
    if prompt:
        steps = t_len // m
        per_b = t_len // WINDOW
        grid = (n_batch, steps)
        row_map = lambda b, t: (b * steps + t, 0)
        halo_map = lambda b, t: (b * per_b + jnp.maximum(t * (m // WINDOW) - 1, 0), 0, 0)
    else:
        grid = (n_batch // nseg, 1)
        row_map = lambda i, t: (i, 0)
        halo_map = lambda i, t: (i, 0, 0)

    weights = [p["norm_b"], p["w_in_b"], p["qg"]]
    in_specs = [
        pl.BlockSpec((m, D_MODEL), row_map),
        pl.BlockSpec((nseg, WINDOW, KV_W), halo_map),
        pl.BlockSpec((m, KV_W), row_map),
        pl.BlockSpec((nseg, WINDOW, KV_W), halo_map),
        pl.BlockSpec((m, KV_W), row_map),
    ] + [pl.BlockSpec(w.shape, const2) for w in weights] + [
        pl.BlockSpec(memory_space=pltpu.SMEM),
        pl.BlockSpec(p["w_out_b"].shape, const2),
    ]
    kv_rows = WINDOW + seg
    scratch = [
        pltpu.VMEM((m, D_MODEL), BF16),
        pltpu.VMEM((m, D_MODEL), F32),
        pltpu.VMEM((m, D_MODEL), BF16),
        pltpu.VMEM((nseg, N_KV, 2, kv_rows, KV_W), BF16),
        pltpu.VMEM((nseg, N_KV, 2, kv_rows, 2 * KV_W), BF16),
        pltpu.VMEM((m, D_MODEL), F32),
        pltpu.VMEM((m, D_MODEL), BF16),
    ]
    return pl.pallas_call(
        functools.partial(_layer_b_kernel, nseg=nseg, seg=seg, mask_first_halo=prompt),
        grid=grid,
        in_specs=in_specs,
        out_specs=pl.BlockSpec((m, D_MODEL), row_map),
        out_shape=jax.ShapeDtypeStruct((rows, D_MODEL), F32),
        scratch_shapes=scratch,
        compiler_params=pltpu.CompilerParams(
            dimension_semantics=("arbitrary", "arbitrary"), vmem_limit_bytes=VMEM_LIMIT_BYTES),
        name="layer_b_prompt" if prompt else "layer_b_sample",
    )(x1, k_halo, k_cur, v_halo, v_cur, *weights, p["sinks"], p["w_out_b"])


def _prep_params(norm_a, w_in_a, conv_w, conv_b, w_gate_x, b_gate_x, w_gate_a, b_gate_a,
                 lru_lambda, w_out_a, norm_kv, w_kv, k_norm, norm_b, w_in_b, q_norm, sinks,
                 w_out_b):
    row = lambda v: v.reshape(1, -1).astype(F32)
    in_scale = jnp.concatenate([jnp.ones((D_MODEL,), F32), jnp.full((D_MODEL,), 0.5, F32)])
    return {
        "norm_a": row(norm_a[0]),
        "w_in_a": (w_in_a[0] * in_scale).astype(BF16),
        "conv_w": conv_w[0].astype(F32),
        "conv_b": row(conv_b[0]),
        "w_gate": (0.5 * jnp.concatenate([w_gate_x[0], w_gate_a[0]], axis=-1)).astype(BF16),
        "b_gx": row(0.5 * b_gate_x[0]),
        "b_ga": row(0.5 * b_gate_a[0]),
        "lam": row(lru_lambda[0]),
        "w_out_a": w_out_a[0].astype(BF16),
        "norm_kv": row(norm_kv),
        "w_kv": w_kv.astype(BF16),
        "k_norm2": row(jnp.tile(k_norm, N_KV)),
        "norm_b": row(norm_b[0]),
        "w_in_b": (w_in_b[0] * in_scale).astype(BF16),
        "qg": row(jnp.tile(q_norm[0], N_HEADS) * (HEAD_DIM ** -0.5 * LOG2_E)),
        "sinks": sinks[0].astype(F32),
        "w_out_b": w_out_b[0].astype(BF16),
    }


def kernel(x_prompt, x_sample, state_conv, state_rglru, cache_k_win, cache_v_win, norm_a, w_in_a, conv_w, conv_b, w_gate_x, b_gate_x, w_gate_a, b_gate_a, lru_lambda, w_out_a, norm_kv, w_kv, k_norm, norm_b, w_in_b, q_norm, sinks, w_out_b):
    assert norm_a.shape[0] == 1 and norm_b.shape[0] == 1, "one recurrent + one attention layer"
    p = _prep_params(norm_a, w_in_a, conv_w, conv_b, w_gate_x, b_gate_x, w_gate_a, b_gate_a,
                     lru_lambda, w_out_a, norm_kv, w_kv, k_norm, norm_b, w_in_b, q_norm, sinks,
                     w_out_b)
    bp, tp, _ = x_prompt.shape
    bs, ts, _ = x_sample.shape

    x1p, kp, vp, conv_p, h_p = _layer_a(
        x_prompt, jnp.zeros((CONV_W - 1, bp, D_RNN), F32), jnp.zeros((bp, D_RNN), F32), p,
        "layer_a_prompt")
    tile_b = 1024
    kp2 = kp.reshape(bp * tp, KV_W)
    vp2 = vp.reshape(bp * tp, KV_W)
    kp3 = kp.reshape(bp * tp // WINDOW, WINDOW, KV_W)
    vp3 = vp.reshape(bp * tp // WINDOW, WINDOW, KV_W)
    y_p = _layer_b(x1p.reshape(bp * tp, D_MODEL), kp3, vp3, kp2, vp2, p, n_batch=bp, t_len=tp,
                   nseg=1, seg=tile_b, prompt=True)

    x1s, ks, vs, conv_s, h_s = _layer_a(
        x_sample, jnp.transpose(state_conv[0], (1, 0, 2)), state_rglru[0], p, "layer_a_sample")
    y_s = _layer_b(x1s.reshape(bs * ts, D_MODEL), cache_k_win.reshape(bs, WINDOW, KV_W),
                   cache_v_win.reshape(bs, WINDOW, KV_W), ks.reshape(bs * ts, KV_W),
                   vs.reshape(bs * ts, KV_W), p, n_batch=bs, t_len=ts, nseg=8, seg=ts,
                   prompt=False)

    kp4 = kp.reshape(bp, tp, N_KV, HEAD_DIM)[:, -WINDOW:]
    vp4 = vp.reshape(bp, tp, N_KV, HEAD_DIM)[:, -WINDOW:]
    ks4 = jnp.concatenate([cache_k_win, ks.reshape(bs, ts, N_KV, HEAD_DIM)], axis=1)[:, -WINDOW:]
    vs4 = jnp.concatenate([cache_v_win, vs.reshape(bs, ts, N_KV, HEAD_DIM)], axis=1)[:, -WINDOW:]
    return (y_p.reshape(bp, tp, D_MODEL), y_s.reshape(bs, ts, D_MODEL),
            jnp.transpose(conv_p, (1, 0, 2))[None], h_p[None], kp4, vp4,
            jnp.transpose(conv_s, (1, 0, 2))[None], h_s[None], ks4, vs4)
```

```python
import functools

import jax
import jax.numpy as jnp
from jax import lax
from jax.experimental import pallas as pl
from jax.experimental.pallas import tpu as pltpu

D_MODEL = 1024
D_RNN = 1024
GATE_BLOCK = 256
N_GATE_BLOCKS = D_RNN // GATE_BLOCK
CONV_W = 4
LRU_C = 8.0
HEAD_DIM = 64
N_HEADS = 16
N_KV = 2
GROUP = N_HEADS // N_KV
CHUNK = 64
WINDOW = 128
EPS = 1e-6
NEG = -1e30
LOG2_E = 1.4426950408889634
LN_2 = 0.6931471805599453

SUBLANES = 8
LANES = 128
BF16_SUBLANES = 16
MXU_N = 256
VMEM_LIMIT_BYTES = 56 * 1024 * 1024

F32 = jnp.float32
BF16 = jnp.bfloat16

W_CHUNK = 128
W_STAGE_COLS = 2 * D_MODEL

KV_W = N_KV * HEAD_DIM
PAIR_W = 2 * HEAD_DIM
N_PAIRS = GROUP // 2
LOOKAHEAD = 2
B_PART_ROWS = 256

NSEQ = SUBLANES
A_MAX_STEPS = 128
CONV_PRE = (CONV_W - 1) * NSEQ
OUT_K_SPLIT = D_RNN // 2


def _silu_of_half(h):
    return h * jnp.tanh(h) + h


def _norm_rows_to(x_ref, g_ref, dst_ref, n_rows, blk, r0=0):
    g = g_ref[...]
    for r in range(r0, r0 + n_rows, blk):
        x = x_ref[r:r + blk, :]
        rs = lax.rsqrt(jnp.mean(x * x, axis=-1, keepdims=True) + EPS)
        dst_ref[r:r + blk, :] = (x_ref[r:r + blk, :] * rs * g).astype(BF16)


def _stream_cast(src_hbm, dst_ref, stage, sem, *, rows, cols, dst_col0=0, col_scale=None):
    n = rows // W_CHUNK

    def copy(c):
        return pltpu.make_async_copy(src_hbm.at[pl.ds(c * W_CHUNK, W_CHUNK), :],
                                     stage.at[c % 2, :, 0:cols], sem.at[c % 2])

    copy(0).start()
    for c in range(n):
        if c + 1 < n:
            copy(c + 1).start()
        copy(c).wait()
        v = stage[c % 2, :, 0:cols]
        if col_scale is not None:
            v = v * col_scale
        dst_ref[c * W_CHUNK:(c + 1) * W_CHUNK, dst_col0:dst_col0 + cols] = v.astype(BF16)


def _gate_half_scale():
    col = lax.broadcasted_iota(jnp.int32, (1, 2 * D_MODEL), 1)
    return jnp.where(col >= D_MODEL, 0.5, 1.0).astype(F32)


def _layer_a_kernel(x_hbm, conv_in_ref, h_in_ref,
                    norm_a_ref, w_in_hbm, conv_w_ref, conv_b_ref, w_gx_hbm, w_ga_hbm, b_gx_ref,
                    b_ga_ref, lam_ref, w_out_hbm, norm_kv_ref, w_kv_hbm, k_norm_ref,
                    x1_hbm, k_hbm, v_hbm, conv_out_ref, h_out_ref,
                    xin, x1o, ko, vo, sem_in, sem_out,
                    xn_scr, xbp_scr, gate_scr, xc_scr, xcb_scr, gx_scr, ga_scr, hy_scr, h_scr,
                    w_in_ref, w_gate_ref, w_out_ref, w_kv_ref, w_stage, w_sem,
                    *, a_steps):
    a_rows = a_steps * NSEQ
    g = pl.program_id(0)
    t = pl.program_id(1)
    nt = pl.num_programs(1)
    total = pl.num_programs(0) * nt
    i = g * nt + t
    slot = lax.rem(i, 2)

    def in_copies(step, sl):
        gg = step // nt
        tt = lax.rem(step, nt)
        return [pltpu.make_async_copy(x_hbm.at[gg * NSEQ + j, pl.ds(tt * a_steps, a_steps), :],
                                      xin.at[sl, :, j, :], sem_in.at[sl]) for j in range(NSEQ)]

    def out_copies(step, sl):
        gg = step // nt
        tt = lax.rem(step, nt)
        cps = []
        for buf, dst in ((x1o, x1_hbm), (ko, k_hbm), (vo, v_hbm)):
            cps += [pltpu.make_async_copy(buf.at[sl, :, j, :],
                                          dst.at[gg * NSEQ + j, pl.ds(tt * a_steps, a_steps), :],
                                          sem_out.at[sl]) for j in range(NSEQ)]
        return cps

    @pl.when(i == 0)
    def _():
        for c in in_copies(i, slot):
            c.start()
        _stream_cast(w_in_hbm, w_in_ref, w_stage, w_sem, rows=D_MODEL, cols=2 * D_RNN,
                     col_scale=_gate_half_scale())
        _stream_cast(w_gx_hbm, w_gate_ref, w_stage, w_sem, rows=D_RNN, cols=GATE_BLOCK,
                     col_scale=0.5)
        _stream_cast(w_ga_hbm, w_gate_ref, w_stage, w_sem, rows=D_RNN, cols=GATE_BLOCK,
                     dst_col0=GATE_BLOCK, col_scale=0.5)
        _stream_cast(w_out_hbm, w_out_ref, w_stage, w_sem, rows=D_RNN, cols=D_MODEL)
        _stream_cast(w_kv_hbm, w_kv_ref, w_stage, w_sem, rows=D_MODEL, cols=2 * KV_W)

    @pl.when(i + 1 < total)
    def _():
        for c in in_copies(i + 1, 1 - slot):
            c.start()

    @pl.when(t == 0)
    def _():
        xbp_scr[0:CONV_PRE, :] = conv_in_ref[...].reshape(CONV_PRE, D_RNN)
        h_scr[...] = h_in_ref[...]

    @pl.when(i >= 2)
    def _():
        for c in out_copies(i - 2, slot):
            c.wait()

    for c in in_copies(i, slot):
        c.wait()

    x_ref = xin.at[slot].reshape(a_rows, D_MODEL)
    x1_ref = x1o.at[slot].reshape(a_rows, D_MODEL)
    k_ref = ko.at[slot].reshape(a_rows, KV_W)
    v_ref = vo.at[slot].reshape(a_rows, KV_W)

    _norm_rows_to(x_ref, norm_a_ref, xn_scr, a_rows, 128)
    lam = lam_ref[...]
    log_sig_lam = jnp.minimum(lam, 0.0) - jnp.log1p(jnp.exp(-jnp.abs(lam)))
    kk = jnp.broadcast_to((0.5 * LRU_C * LOG2_E) * log_sig_lam, (NSEQ, D_RNN))
    b_gx = jnp.broadcast_to(b_gx_ref[...], (NSEQ, D_RNN))
    b_ga = jnp.broadcast_to(b_ga_ref[...], (NSEQ, D_RNN))

    cw = conv_w_ref[...]
    cb = conv_b_ref[...]
    conv_blk = 32

    def in_proj_x(n):
        cols = slice(n * GATE_BLOCK, (n + 1) * GATE_BLOCK)
        xbp_scr[CONV_PRE:, cols] = jnp.dot(xn_scr[...], w_in_ref[:, cols],
                                           preferred_element_type=F32)

    def in_proj_gate(n):
        cols = slice(n * GATE_BLOCK, (n + 1) * GATE_BLOCK)
        gate_scr[:, cols] = jnp.dot(
            xn_scr[...], w_in_ref[:, D_RNN + n * GATE_BLOCK:D_RNN + (n + 1) * GATE_BLOCK],
            preferred_element_type=F32)

    def conv_and_gates(n):
        cols = slice(n * GATE_BLOCK, (n + 1) * GATE_BLOCK)
        for r in range(0, a_rows, conv_blk):
            acc = cb[:, cols]
            for j in range(CONV_W):
                acc = acc + cw[j:j + 1, cols] * xbp_scr[r + j * NSEQ:r + j * NSEQ + conv_blk, cols]
            xc_scr[r:r + conv_blk, cols] = acc
            xcb_scr[r:r + conv_blk, cols] = acc.astype(BF16)
        tail = xbp_scr[a_rows:a_rows + CONV_PRE, cols]
        conv_out_ref[:, :, cols] = tail.reshape(CONV_W - 1, NSEQ, GATE_BLOCK)
        xbp_scr[0:CONV_PRE, cols] = tail

        res = jnp.dot(xcb_scr[:, cols], w_gate_ref[cols, :], preferred_element_type=F32)
        gx_scr[:, cols] = res[:, :GATE_BLOCK]
        ga_scr[:, cols] = res[:, GATE_BLOCK:]

    def scan(n):
        cols = slice(n * GATE_BLOCK, (n + 1) * GATE_BLOCK)
        kk_n = kk[:, cols]
        bgx_n = b_gx[:, cols]
        bga_n = b_ga[:, cols]
        h = h_scr[:, cols]
        for r in range(0, a_rows, BF16_SUBLANES):
            hy = []
            for rr in (r, r + NSEQ):
                t_i = jnp.tanh(gx_scr[rr:rr + NSEQ, cols] + bgx_n)
                t_r = jnp.tanh(ga_scr[rr:rr + NSEQ, cols] + bga_n)
                log2_a = t_r * kk_n + kk_n
                a = jnp.exp2(log2_a)
                y = jnp.tanh(log2_a * (-LN_2)) * (a * a + 1.0)
                mult = jnp.where(y > 0.0, y * lax.rsqrt(y), 0.0)
                u = mult * xc_scr[rr:rr + NSEQ, cols]
                h = a * h + u * (0.5 * t_i + 0.5)
                hy.append(h * _silu_of_half(gate_scr[rr:rr + NSEQ, cols]))
            hy_scr[r:r + BF16_SUBLANES, cols] = jnp.concatenate(hy, axis=0).astype(BF16)
        h_scr[:, cols] = h
        h_out_ref[:, cols] = h

    def out_proj_first_part():
        for nn in range(D_MODEL // MXU_N):
            cols = slice(nn * MXU_N, (nn + 1) * MXU_N)
            x1_ref[:, cols] = x_ref[:, cols] + jnp.dot(
                hy_scr[:, :OUT_K_SPLIT], w_out_ref[:OUT_K_SPLIT, cols],
                preferred_element_type=F32)

    in_proj_x(0)
    in_proj_gate(0)
    in_proj_x(1)
    conv_and_gates(0)
    in_proj_gate(1)
    for n in range(N_GATE_BLOCKS):
        if n + 2 < N_GATE_BLOCKS:
            in_proj_x(n + 2)
        if n + 1 < N_GATE_BLOCKS:
            conv_and_gates(n + 1)
        if n + 2 < N_GATE_BLOCKS:
            in_proj_gate(n + 2)
        if (n + 1) * GATE_BLOCK == D_RNN - GATE_BLOCK:
            out_proj_first_part()
        scan(n)

    half_rows = a_rows // 2
    lo = lax.broadcasted_iota(jnp.int32, (half_rows, KV_W), 1) < HEAD_DIM
    for rh in range(2):
        rows = slice(rh * half_rows, (rh + 1) * half_rows)
        for n in range(D_MODEL // MXU_N):
            cols = slice(n * MXU_N, (n + 1) * MXU_N)
            x1_ref[rows, cols] = x1_ref[rows, cols] + jnp.dot(
                hy_scr[rows, OUT_K_SPLIT:], w_out_ref[OUT_K_SPLIT:, cols],
                preferred_element_type=F32)
        _norm_rows_to(x1_ref, norm_kv_ref, xn_scr, half_rows, 64, r0=rh * half_rows)
        kv = jnp.dot(xn_scr[rows, :], w_kv_ref[...], preferred_element_type=F32)
        k = kv[:, :KV_W]
        k2 = k * k
        ms_lo = jnp.sum(jnp.where(lo, k2, 0.0), axis=-1, keepdims=True) * (1.0 / HEAD_DIM)
        ms_hi = jnp.sum(jnp.where(lo, 0.0, k2), axis=-1, keepdims=True) * (1.0 / HEAD_DIM)
        rs = jnp.where(lo, lax.rsqrt(ms_lo + EPS), lax.rsqrt(ms_hi + EPS))
        k_ref[rows, :] = k * rs * k_norm_ref[...]
        v_ref[rows, :] = kv[:, KV_W:]

    for c in out_copies(i, slot):
        c.start()

    @pl.when(i == total - 1)
    def _():
        @pl.when(i >= 1)
        def _():
            for c in out_copies(i - 1, 1 - slot):
                c.wait()
        for c in out_copies(i, slot):
            c.wait()


def _layer_a(x, conv_in, h_in, p, name):
    n_seq, t_len, _ = x.shape
    a_steps = min(t_len, A_MAX_STEPS)
    a_rows = a_steps * NSEQ
    assert n_seq % NSEQ == 0 and t_len % a_steps == 0
    grid = (n_seq // NSEQ, t_len // a_steps)
    const2 = lambda g, t: (0, 0)
    const3 = lambda g, t: (0, 0, 0)
    any_spec = pl.BlockSpec(memory_space=pl.ANY)

    weights = [p["norm_a"], p["w_in_a"], p["conv_w"], p["conv_b"], p["w_gx"], p["w_ga"],
               p["b_gx"], p["b_ga"], p["lam"], p["w_out_a"], p["norm_kv"], p["w_kv"],
               p["k_norm2"]]
    big = ("w_in_a", "w_gx", "w_ga", "w_out_a", "w_kv")
    w_specs = [any_spec if any(w is p[name_] for name_ in big) else
               pl.BlockSpec(w.shape, const2, pipeline_mode=pl.Buffered(1)) for w in weights]
    conv_spec = pl.BlockSpec((CONV_W - 1, NSEQ, D_RNN), lambda g, t: (0, g, 0))
    h_spec = pl.BlockSpec((NSEQ, D_RNN), lambda g, t: (g, 0))

    out_shape = [
        jax.ShapeDtypeStruct((n_seq, t_len, D_MODEL), F32),
        jax.ShapeDtypeStruct((n_seq, t_len, KV_W), F32),
        jax.ShapeDtypeStruct((n_seq, t_len, KV_W), F32),
        jax.ShapeDtypeStruct((CONV_W - 1, n_seq, D_RNN), F32),
        jax.ShapeDtypeStruct((n_seq, D_RNN), F32),
    ]
    scratch = [
        pltpu.VMEM((2, a_steps, NSEQ, D_MODEL), F32),
        pltpu.VMEM((2, a_steps, NSEQ, D_MODEL), F32),
        pltpu.VMEM((2, a_steps, NSEQ, KV_W), F32),
        pltpu.VMEM((2, a_steps, NSEQ, KV_W), F32),
        pltpu.SemaphoreType.DMA((2,)),
        pltpu.SemaphoreType.DMA((2,)),
        pltpu.VMEM((a_rows, D_MODEL), BF16),
        pltpu.VMEM((CONV_PRE + a_rows, D_RNN), F32),
        pltpu.VMEM((a_rows, D_RNN), F32),
        pltpu.VMEM((a_rows, D_RNN), F32),
        pltpu.VMEM((a_rows, D_RNN), BF16),
        pltpu.VMEM((a_rows, D_RNN), F32),
        pltpu.VMEM((a_rows, D_RNN), F32),
        pltpu.VMEM((a_rows, D_RNN), BF16),
        pltpu.VMEM((NSEQ, D_RNN), F32),
        pltpu.VMEM((D_MODEL, 2 * D_RNN), BF16),
        pltpu.VMEM((D_RNN, 2 * GATE_BLOCK), BF16),
        pltpu.VMEM((D_RNN, D_MODEL), BF16),
        pltpu.VMEM((D_MODEL, 2 * KV_W), BF16),
        pltpu.VMEM((2, W_CHUNK, W_STAGE_COLS), F32),
        pltpu.SemaphoreType.DMA((2,)),
    ]
    return pl.pallas_call(
        functools.partial(_layer_a_kernel, a_steps=a_steps),
        grid=grid,
        in_specs=[any_spec, conv_spec, h_spec] + w_specs,
        out_specs=[any_spec, any_spec, any_spec, conv_spec, h_spec],
        out_shape=out_shape,
        scratch_shapes=scratch,
        compiler_params=pltpu.CompilerParams(
            dimension_semantics=("arbitrary", "arbitrary"), vmem_limit_bytes=VMEM_LIMIT_BYTES),
        name=name,
    )(x, conv_in, h_in, *weights)


def _layer_b_kernel(x1_ref, kh_ref, kc_ref, vh_ref, vc_ref, norm_b_ref, w_in_hbm,
                    qg_ref, sinks_ref, w_out_hbm, y_ref,
                    xn_scr, gate_scr, qh_scr, kmat_scr, vmat_scr, o_scr, og_scr,
                    w_in_ref, w_out_ref, w_stage, w_sem,
                    *, nseg, seg, mask_first_halo):
    m = nseg * seg
    n_chunks = seg // CHUNK

    @pl.when(jnp.logical_and(pl.program_id(0) == 0, pl.program_id(1) == 0))
    def _():
        _stream_cast(w_in_hbm, w_in_ref, w_stage, w_sem, rows=D_MODEL, cols=2 * D_MODEL,
                     col_scale=_gate_half_scale())
        _stream_cast(w_out_hbm, w_out_ref, w_stage, w_sem, rows=D_MODEL, cols=D_MODEL)

    part = min(m, B_PART_ROWS)
    assert m % part == 0 and (part % seg == 0 or seg % part == 0)
    n_parts = m // part
    n_col_blocks = D_MODEL // MXU_N

    _norm_rows_to(x1_ref, norm_b_ref, xn_scr, m, 128)

    lo_m = lax.broadcasted_iota(jnp.int32, (part, LANES), 1) < HEAD_DIM
    qg = qg_ref[...]

    def q_block(h, n):
        rows = slice(h * part, (h + 1) * part)
        qn = jnp.dot(xn_scr[rows, :], w_in_ref[:, n * MXU_N:(n + 1) * MXU_N],
                     preferred_element_type=F32)
        for cc in range(MXU_N // LANES):
            col = n * MXU_N + cc * LANES
            x = qn[:, cc * LANES:(cc + 1) * LANES]
            sq = x * x
            ms_lo = jnp.sum(jnp.where(lo_m, sq, 0.0), axis=-1, keepdims=True) * (1.0 / HEAD_DIM)
            ms_hi = jnp.sum(jnp.where(lo_m, 0.0, sq), axis=-1, keepdims=True) * (1.0 / HEAD_DIM)
            rs = jnp.where(lo_m, lax.rsqrt(ms_lo + EPS), lax.rsqrt(ms_hi + EPS))
            qh_scr[rows, col:col + LANES] = (x * rs * qg[:, col:col + LANES]).astype(BF16)

    row_split = 2 if part >= 2 * MXU_N else 1

    def gate_piece(h, n, rh):
        r0 = h * part + rh * (part // row_split)
        rows = slice(r0, r0 + part // row_split)
        gate_scr[rows, n * MXU_N:(n + 1) * MXU_N] = jnp.dot(
            xn_scr[rows, :], w_in_ref[:, D_MODEL + n * MXU_N:D_MODEL + (n + 1) * MXU_N],
            preferred_element_type=F32)

    def gate_multiply(h):
        blk = 32
        for r in range(h * part, (h + 1) * part, blk):
            og_scr[r:r + blk, :] = (
                o_scr[r:r + blk, :] * _silu_of_half(gate_scr[r:r + blk, :])).astype(BF16)

    def out_piece(h, n):
        rows = slice(h * part, (h + 1) * part)
        cols = slice(n * MXU_N, (n + 1) * MXU_N)
        y_ref[rows, cols] = x1_ref[rows, cols] + jnp.dot(
            og_scr[rows, :], w_out_ref[:, cols], preferred_element_type=F32)

    lane = lax.broadcasted_iota(jnp.int32, (CHUNK, KV_W), 1)
    lo = lane < HEAD_DIM
    first = pl.program_id(1) == 0 if mask_first_halo else None

    for s in range(nseg):
        for r0, nrows, kval, vval in ((0, WINDOW, kh_ref[s], vh_ref[s]),
                                      (WINDOW, seg, kc_ref[s * seg:(s + 1) * seg, :],
                                       vc_ref[s * seg:(s + 1) * seg, :])):
            is_lo = lax.broadcasted_iota(jnp.int32, kval.shape, 1) < HEAD_DIM
            ind_e = jnp.where(is_lo, 1.0, 0.0).astype(BF16)
            ind_o = jnp.where(is_lo, 0.0, 1.0).astype(BF16)
            rows = slice(r0, r0 + nrows)
            for val, scr in ((kval, kmat_scr), (vval, vmat_scr)):
                swapped = pltpu.roll(val, HEAD_DIM, 1)
                for g in range(N_KV):
                    low = val if g == 0 else swapped
                    high = swapped if g == 0 else val
                    scr[s, g, 0, rows, 0:KV_W] = jnp.where(is_lo, low, 0.0).astype(BF16)
                    scr[s, g, 1, rows, 0:KV_W] = jnp.where(is_lo, 0.0, high).astype(BF16)
            for g in range(N_KV):
                vmat_scr[s, g, 0, rows, KV_W:] = ind_e
                vmat_scr[s, g, 1, rows, KV_W:] = ind_o

    def placed(scr, s, g, k0):
        return jnp.concatenate(
            [scr[s, g, 0, k0:k0 + WINDOW, :], scr[s, g, 1, k0:k0 + WINDOW, :],
             scr[s, g, 0, k0 + WINDOW:k0 + WINDOW + CHUNK, :],
             scr[s, g, 1, k0 + WINDOW:k0 + WINDOW + CHUNK, :]], axis=0)

    def scores(s, c, g):
        q0 = s * seg + c * CHUNK
        qst = jnp.concatenate(
            [qh_scr[q0:q0 + CHUNK, g * GROUP * HEAD_DIM + j * PAIR_W:
                    g * GROUP * HEAD_DIM + (j + 1) * PAIR_W] for j in range(N_PAIRS)],
            axis=0)
        return lax.dot_general(qst, placed(kmat_scr, s, g, c * CHUNK),
                               (((1,), (1,)), ((), ())),
                               preferred_element_type=F32)

    def softmax_pv(sc, s, c, g):
        p_rows, sink_rows = [], []
        for j in range(N_PAIRS):
            sj = sc[j * CHUNK:(j + 1) * CHUNK]
            c0 = sj[:, 0:KV_W]
            c1 = sj[:, KV_W:2 * KV_W]
            c2 = sj[:, 2 * KV_W:]
            if mask_first_halo and c * CHUNK < WINDOW:
                n_bad = jnp.where(first, WINDOW - c * CHUNK, 0)
                bad = lane < n_bad
                c0 = jnp.where(bad, NEG, c0)
                c1 = jnp.where(bad, NEG, c1)
            sink_e = sinks_ref[g * GROUP + 2 * j] * LOG2_E
            sink_o = sinks_ref[g * GROUP + 2 * j + 1] * LOG2_E
            m_e = jnp.maximum(jnp.max(jnp.maximum(c0, jnp.where(lo, c2, NEG)),
                                      axis=-1, keepdims=True), sink_e)
            m_o = jnp.maximum(jnp.max(jnp.maximum(c1, jnp.where(lo, NEG, c2)),
                                      axis=-1, keepdims=True), sink_o)
            p_rows.append(jnp.concatenate(
                [jnp.exp2(c0 - m_e), jnp.exp2(c1 - m_o),
                 jnp.exp2(c2 - jnp.where(lo, m_e, m_o))], axis=1).astype(BF16))
            sink_rows.append(jnp.where(lo, jnp.exp2(sink_e - m_e), jnp.exp2(sink_o - m_o)))
        pmat = jnp.concatenate(p_rows, axis=0)
        ov = jnp.dot(pmat, placed(vmat_scr, s, g, c * CHUNK),
                     preferred_element_type=F32)
        q0 = s * seg + c * CHUNK
        for j in range(N_PAIRS):
            rows = slice(j * CHUNK, (j + 1) * CHUNK)
            c_lo = g * GROUP * HEAD_DIM + j * PAIR_W
            o_scr[q0:q0 + CHUNK, c_lo:c_lo + PAIR_W] = (
                ov[rows, :KV_W] / (ov[rows, KV_W:] + sink_rows[j]))

    all_units = [(s, c, g) for s in range(nseg) for c in range(n_chunks) for g in range(N_KV)]

    for n in range(n_col_blocks):
        q_block(0, n)
    for h in range(n_parts):
        units = [u for u in all_units if (u[0] * seg + u[1] * CHUNK) // part == h]
        fillers = []
        if h >= 1:
            fillers += [functools.partial(out_piece, h - 1, n) for n in range(n_col_blocks)]
        gates = [functools.partial(gate_piece, h, n, rh)
                 for n in range(n_col_blocks) for rh in range(row_split)]
        nxt = ([functools.partial(q_block, h + 1, n) for n in range(n_col_blocks)]
               if h + 1 < n_parts else [])
        while gates or nxt:
            fillers += gates[:2]
            gates = gates[2:]
            fillers += nxt[:1]
            nxt = nxt[1:]
        fillers.pop(0)()
        per_unit = -(-len(fillers) // len(units))
        pending = [scores(*u) for u in units[:LOOKAHEAD]]
        for k, (s, c, g) in enumerate(units):
            sc = pending.pop(0)
            if k + LOOKAHEAD < len(units):
                pending.append(scores(*units[k + LOOKAHEAD]))
            for _ in range(min(per_unit, len(fillers))):
                fillers.pop(0)()
            softmax_pv(sc, s, c, g)
        assert not fillers
        gate_multiply(h)
    for n in range(n_col_blocks):
        out_piece(n_parts - 1, n)


def _layer_b(x1, k_halo, v_halo, k_cur, v_cur, p, *, n_batch, t_len, nseg, seg, prompt):
    m = nseg * seg
    rows = n_batch * t_len
    const2 = lambda *_: (0, 0)
    if prompt:
        steps = t_len // m
        per_b = t_len // WINDOW
        grid = (n_batch, steps)
        row_map = lambda b, t: (b * steps + t, 0)
        halo_map = lambda b, t: (b * per_b + jnp.maximum(t * (m // WINDOW) - 1, 0), 0, 0)
    else:
        grid = (n_batch // nseg, 1)
        row_map = lambda i, t: (i, 0)
        halo_map = lambda i, t: (i, 0, 0)

    any_spec = pl.BlockSpec(memory_space=pl.ANY)
    in_specs = [
        pl.BlockSpec((m, D_MODEL), row_map),
        pl.BlockSpec((nseg, WINDOW, KV_W), halo_map),
        pl.BlockSpec((m, KV_W), row_map),
        pl.BlockSpec((nseg, WINDOW, KV_W), halo_map),
        pl.BlockSpec((m, KV_W), row_map),
        pl.BlockSpec(p["norm_b"].shape, const2),
        any_spec,
        pl.BlockSpec(p["qg"].shape, const2),
        pl.BlockSpec(memory_space=pltpu.SMEM),
        any_spec,
    ]
    kv_rows = WINDOW + seg
    scratch = [
        pltpu.VMEM((m, D_MODEL), BF16),
        pltpu.VMEM((m, D_MODEL), F32),
        pltpu.VMEM((m, D_MODEL), BF16),
        pltpu.VMEM((nseg, N_KV, 2, kv_rows, KV_W), BF16),
        pltpu.VMEM((nseg, N_KV, 2, kv_rows, 2 * KV_W), BF16),
        pltpu.VMEM((m, D_MODEL), F32),
        pltpu.VMEM((m, D_MODEL), BF16),
        pltpu.VMEM((D_MODEL, 2 * D_MODEL), BF16),
        pltpu.VMEM((D_MODEL, D_MODEL), BF16),
        pltpu.VMEM((2, W_CHUNK, W_STAGE_COLS), F32),
        pltpu.SemaphoreType.DMA((2,)),
    ]
    return pl.pallas_call(
        functools.partial(_layer_b_kernel, nseg=nseg, seg=seg, mask_first_halo=prompt),
        grid=grid,
        in_specs=in_specs,
        out_specs=pl.BlockSpec((m, D_MODEL), row_map),
        out_shape=jax.ShapeDtypeStruct((rows, D_MODEL), F32),
        scratch_shapes=scratch,
        compiler_params=pltpu.CompilerParams(
            dimension_semantics=("arbitrary", "arbitrary"), vmem_limit_bytes=VMEM_LIMIT_BYTES),
        name="layer_b_prompt" if prompt else "layer_b_sample",
    )(x1, k_halo, k_cur, v_halo, v_cur, p["norm_b"], p["w_in_b"], p["qg"], p["sinks"],
      p["w_out_b"])


def _prep_params(norm_a, w_in_a, conv_w, conv_b, w_gate_x, b_gate_x, w_gate_a, b_gate_a,
                 lru_lambda, w_out_a, norm_kv, w_kv, k_norm, norm_b, w_in_b, q_norm, sinks,
                 w_out_b):
    row = lambda v: v.reshape(1, -1).astype(F32)
    return {
        "norm_a": row(norm_a[0]),
        "w_in_a": w_in_a[0],
        "conv_w": conv_w[0].astype(F32),
        "conv_b": row(conv_b[0]),
        "w_gx": w_gate_x[0].reshape(D_RNN, GATE_BLOCK),
        "w_ga": w_gate_a[0].reshape(D_RNN, GATE_BLOCK),
        "b_gx": row(0.5 * b_gate_x[0]),
        "b_ga": row(0.5 * b_gate_a[0]),
        "lam": row(lru_lambda[0]),
        "w_out_a": w_out_a[0],
        "norm_kv": row(norm_kv),
        "w_kv": w_kv,
        "k_norm2": row(jnp.tile(k_norm, N_KV)),
        "norm_b": row(norm_b[0]),
        "w_in_b": w_in_b[0],
        "qg": row(jnp.tile(q_norm[0], N_HEADS) * (HEAD_DIM ** -0.5 * LOG2_E)),
        "sinks": sinks[0].astype(F32),
        "w_out_b": w_out_b[0],
    }


def kernel(x_prompt, x_sample, state_conv, state_rglru, cache_k_win, cache_v_win, norm_a, w_in_a, conv_w, conv_b, w_gate_x, b_gate_x, w_gate_a, b_gate_a, lru_lambda, w_out_a, norm_kv, w_kv, k_norm, norm_b, w_in_b, q_norm, sinks, w_out_b):
    assert norm_a.shape[0] == 1 and norm_b.shape[0] == 1, "one recurrent + one attention layer"
    p = _prep_params(norm_a, w_in_a, conv_w, conv_b, w_gate_x, b_gate_x, w_gate_a, b_gate_a,
                     lru_lambda, w_out_a, norm_kv, w_kv, k_norm, norm_b, w_in_b, q_norm, sinks,
                     w_out_b)
    bp, tp, _ = x_prompt.shape
    bs, ts, _ = x_sample.shape

    x1p, kp, vp, conv_p, h_p = _layer_a(
        x_prompt, jnp.zeros((CONV_W - 1, bp, D_RNN), F32), jnp.zeros((bp, D_RNN), F32), p,
        "layer_a_prompt")
    tile_b = 1024
    kp2 = kp.reshape(bp * tp, KV_W)
    vp2 = vp.reshape(bp * tp, KV_W)
    kp3 = kp.reshape(bp * tp // WINDOW, WINDOW, KV_W)
    vp3 = vp.reshape(bp * tp // WINDOW, WINDOW, KV_W)
    y_p = _layer_b(x1p.reshape(bp * tp, D_MODEL), kp3, vp3, kp2, vp2, p, n_batch=bp, t_len=tp,
                   nseg=1, seg=tile_b, prompt=True)

    x1s, ks, vs, conv_s, h_s = _layer_a(
        x_sample, jnp.transpose(state_conv[0], (1, 0, 2)), state_rglru[0], p, "layer_a_sample")
    y_s = _layer_b(x1s.reshape(bs * ts, D_MODEL), cache_k_win.reshape(bs, WINDOW, KV_W),
                   cache_v_win.reshape(bs, WINDOW, KV_W), ks.reshape(bs * ts, KV_W),
                   vs.reshape(bs * ts, KV_W), p, n_batch=bs, t_len=ts, nseg=8, seg=ts,
                   prompt=False)

    kp4 = kp.reshape(bp, tp, N_KV, HEAD_DIM)[:, -WINDOW:]
    vp4 = vp.reshape(bp, tp, N_KV, HEAD_DIM)[:, -WINDOW:]
    ks4 = jnp.concatenate([cache_k_win, ks.reshape(bs, ts, N_KV, HEAD_DIM)], axis=1)[:, -WINDOW:]
    vs4 = jnp.concatenate([cache_v_win, vs.reshape(bs, ts, N_KV, HEAD_DIM)], axis=1)[:, -WINDOW:]
    return (y_p.reshape(bp, tp, D_MODEL), y_s.reshape(bs, ts, D_MODEL),
            jnp.transpose(conv_p, (1, 0, 2))[None], h_p[None], kp4, vp4,
            jnp.transpose(conv_s, (1, 0, 2))[None], h_s[None], ks4, vs4)
```

```python
import functools

import jax
import jax.numpy as jnp
from jax import lax
from jax.experimental import pallas as pl
from jax.experimental.pallas import tpu as pltpu

D_MODEL = 1024
D_RNN = 1024
GATE_BLOCK = 256
N_GATE_BLOCKS = D_RNN // GATE_BLOCK
CONV_W = 4
LRU_C = 8.0
HEAD_DIM = 64
N_HEADS = 16
N_KV = 2
GROUP = N_HEADS // N_KV
CHUNK = 64
WINDOW = 128
EPS = 1e-6
NEG = -1e30
LOG2_E = 1.4426950408889634
LN_2 = 0.6931471805599453

SUBLANES = 8
LANES = 128
BF16_SUBLANES = 16
MXU_N = 256
VMEM_LIMIT_BYTES = 56 * 1024 * 1024

F32 = jnp.float32
BF16 = jnp.bfloat16

KV_W = N_KV * HEAD_DIM
PAIR_W = 2 * HEAD_DIM
N_PAIRS = GROUP // 2
LOOKAHEAD = 2
B_PART_ROWS = 256

A_MAX_STEPS = 128
A_TILE_ROWS = 1024
OUT_K_SPLIT = D_RNN // 2


def _silu_of_half(h):
    return h * jnp.tanh(h) + h


def _norm_rows_to(x_ref, g_ref, dst_ref, n_rows, blk, r0=0):
    g = g_ref[...]
    for r in range(r0, r0 + n_rows, blk):
        x = x_ref[r:r + blk, :]
        rs = lax.rsqrt(jnp.mean(x * x, axis=-1, keepdims=True) + EPS)
        dst_ref[r:r + blk, :] = (x_ref[r:r + blk, :] * rs * g).astype(BF16)


def _layer_a_kernel(x_hbm, conv_in_ref, h_in_ref,
                    norm_a_ref, w_in_ref, conv_w_ref, conv_b_ref, w_gate_ref, b_gx_ref, b_ga_ref,
                    lam_ref, w_out_ref, norm_kv_ref, w_kv_ref, k_norm_ref,
                    x1_hbm, k_hbm, v_hbm, conv_out_ref, h_out_ref,
                    xin, x1o, ko, vo, sem_in, sem_out,
                    xn_scr, xbp_scr, gate_scr, xc_scr, xcb_scr, gx_scr, ga_scr, hy_scr, h_scr,
                    *, a_steps, nseq):
    a_rows = a_steps * nseq
    conv_pre = (CONV_W - 1) * nseq
    g = pl.program_id(0)
    t = pl.program_id(1)
    nt = pl.num_programs(1)
    total = pl.num_programs(0) * nt
    i = g * nt + t
    slot = lax.rem(i, 2)

    def in_copies(step, sl):
        gg = step // nt
        tt = lax.rem(step, nt)
        return [pltpu.make_async_copy(x_hbm.at[gg * nseq + j, pl.ds(tt * a_steps, a_steps), :],
                                      xin.at[sl, :, j, :], sem_in.at[sl]) for j in range(nseq)]

    def out_copies(step, sl):
        gg = step // nt
        tt = lax.rem(step, nt)
        cps = []
        for buf, dst in ((x1o, x1_hbm), (ko, k_hbm), (vo, v_hbm)):
            cps += [pltpu.make_async_copy(buf.at[sl, :, j, :],
                                          dst.at[gg * nseq + j, pl.ds(tt * a_steps, a_steps), :],
                                          sem_out.at[sl]) for j in range(nseq)]
        return cps

    @pl.when(i == 0)
    def _():
        for c in in_copies(i, slot):
            c.start()

    @pl.when(i + 1 < total)
    def _():
        for c in in_copies(i + 1, 1 - slot):
            c.start()

    @pl.when(t == 0)
    def _():
        xbp_scr[0:conv_pre, :] = conv_in_ref[...].reshape(conv_pre, D_RNN)
        h_scr[...] = h_in_ref[...]

    @pl.when(i >= 2)
    def _():
        for c in out_copies(i - 2, slot):
            c.wait()

    for c in in_copies(i, slot):
        c.wait()

    x_ref = xin.at[slot].reshape(a_rows, D_MODEL)
    x1_ref = x1o.at[slot].reshape(a_rows, D_MODEL)
    k_ref = ko.at[slot].reshape(a_rows, KV_W)
    v_ref = vo.at[slot].reshape(a_rows, KV_W)

    _norm_rows_to(x_ref, norm_a_ref, xn_scr, a_rows, 128)
    lam = lam_ref[...]
    log_sig_lam = jnp.minimum(lam, 0.0) - jnp.log1p(jnp.exp(-jnp.abs(lam)))
    kk = jnp.broadcast_to((0.5 * LRU_C * LOG2_E) * log_sig_lam, (nseq, D_RNN))
    b_gx = jnp.broadcast_to(b_gx_ref[...], (nseq, D_RNN))
    b_ga = jnp.broadcast_to(b_ga_ref[...], (nseq, D_RNN))

    cw = conv_w_ref[...]
    cb = conv_b_ref[...]
    conv_blk = 32

    def in_proj_x(n):
        cols = slice(n * GATE_BLOCK, (n + 1) * GATE_BLOCK)
        xbp_scr[conv_pre:, cols] = jnp.dot(xn_scr[...], w_in_ref[:, cols],
                                           preferred_element_type=F32)

    def in_proj_gate(n):
        cols = slice(n * GATE_BLOCK, (n + 1) * GATE_BLOCK)
        gate_scr[:, cols] = jnp.dot(
            xn_scr[...], w_in_ref[:, D_RNN + n * GATE_BLOCK:D_RNN + (n + 1) * GATE_BLOCK],
            preferred_element_type=F32)

    def conv_and_gates(n):
        cols = slice(n * GATE_BLOCK, (n + 1) * GATE_BLOCK)
        for r in range(0, a_rows, conv_blk):
            acc = cb[:, cols]
            for j in range(CONV_W):
                acc = acc + cw[j:j + 1, cols] * xbp_scr[r + j * nseq:r + j * nseq + conv_blk, cols]
            xc_scr[r:r + conv_blk, cols] = acc
            xcb_scr[r:r + conv_blk, cols] = acc.astype(BF16)
        tail = xbp_scr[a_rows:a_rows + conv_pre, cols]
        conv_out_ref[:, :, cols] = tail.reshape(CONV_W - 1, nseq, GATE_BLOCK)
        xbp_scr[0:conv_pre, cols] = tail

        res = jnp.dot(xcb_scr[:, cols], w_gate_ref[n], preferred_element_type=F32)
        gx_scr[:, cols] = res[:, :GATE_BLOCK]
        ga_scr[:, cols] = res[:, GATE_BLOCK:]

    def scan(n):
        cols = slice(n * GATE_BLOCK, (n + 1) * GATE_BLOCK)
        kk_n = kk[:, cols]
        bgx_n = b_gx[:, cols]
        bga_n = b_ga[:, cols]
        h = h_scr[:, cols]
        rows_per_store = max(BF16_SUBLANES, nseq)
        for r in range(0, a_rows, rows_per_store):
            hy = []
            for rr in range(r, r + rows_per_store, nseq):
                t_i = jnp.tanh(gx_scr[rr:rr + nseq, cols] + bgx_n)
                t_r = jnp.tanh(ga_scr[rr:rr + nseq, cols] + bga_n)
                log2_a = t_r * kk_n + kk_n
                a = jnp.exp2(log2_a)
                y = jnp.tanh(log2_a * (-LN_2)) * (a * a + 1.0)
                mult = jnp.where(y > 0.0, y * lax.rsqrt(y), 0.0)
                u = mult * xc_scr[rr:rr + nseq, cols]
                h = a * h + u * (0.5 * t_i + 0.5)
                hy.append(h * _silu_of_half(gate_scr[rr:rr + nseq, cols]))
            hy_scr[r:r + rows_per_store, cols] = jnp.concatenate(hy, axis=0).astype(BF16)
        h_scr[:, cols] = h
        h_out_ref[:, cols] = h

    def out_proj_first_part():
        for nn in range(D_MODEL // MXU_N):
            cols = slice(nn * MXU_N, (nn + 1) * MXU_N)
            x1_ref[:, cols] = x_ref[:, cols] + jnp.dot(
                hy_scr[:, :OUT_K_SPLIT], w_out_ref[:OUT_K_SPLIT, cols],
                preferred_element_type=F32)

    in_proj_x(0)
    in_proj_gate(0)
    in_proj_x(1)
    conv_and_gates(0)
    in_proj_gate(1)
    for n in range(N_GATE_BLOCKS):
        if n + 2 < N_GATE_BLOCKS:
            in_proj_x(n + 2)
        if n + 1 < N_GATE_BLOCKS:
            conv_and_gates(n + 1)
        if n + 2 < N_GATE_BLOCKS:
            in_proj_gate(n + 2)
        if (n + 1) * GATE_BLOCK == D_RNN - GATE_BLOCK:
            out_proj_first_part()
        scan(n)

    half_rows = a_rows // 2
    lo = lax.broadcasted_iota(jnp.int32, (half_rows, KV_W), 1) < HEAD_DIM
    for rh in range(2):
        rows = slice(rh * half_rows, (rh + 1) * half_rows)
        for n in range(D_MODEL // MXU_N):
            cols = slice(n * MXU_N, (n + 1) * MXU_N)
            x1_ref[rows, cols] = x1_ref[rows, cols] + jnp.dot(
                hy_scr[rows, OUT_K_SPLIT:], w_out_ref[OUT_K_SPLIT:, cols],
                preferred_element_type=F32)
        _norm_rows_to(x1_ref, norm_kv_ref, xn_scr, half_rows, 64, r0=rh * half_rows)
        kv = jnp.dot(xn_scr[rows, :], w_kv_ref[...], preferred_element_type=F32)
        k = kv[:, :KV_W]
        k2 = k * k
        ms_lo = jnp.sum(jnp.where(lo, k2, 0.0), axis=-1, keepdims=True) * (1.0 / HEAD_DIM)
        ms_hi = jnp.sum(jnp.where(lo, 0.0, k2), axis=-1, keepdims=True) * (1.0 / HEAD_DIM)
        rs = jnp.where(lo, lax.rsqrt(ms_lo + EPS), lax.rsqrt(ms_hi + EPS))
        k_ref[rows, :] = k * rs * k_norm_ref[...]
        v_ref[rows, :] = kv[:, KV_W:]

    for c in out_copies(i, slot):
        c.start()

    @pl.when(i == total - 1)
    def _():
        @pl.when(i >= 1)
        def _():
            for c in out_copies(i - 1, 1 - slot):
                c.wait()
        for c in out_copies(i, slot):
            c.wait()


def _layer_a(x, conv_in, h_in, p, name):
    n_seq, t_len, _ = x.shape
    a_steps = min(t_len, A_MAX_STEPS)
    nseq = min(n_seq, max(SUBLANES, A_TILE_ROWS // a_steps), BF16_SUBLANES)
    a_rows = a_steps * nseq
    conv_pre = (CONV_W - 1) * nseq
    assert nseq % SUBLANES == 0 and n_seq % nseq == 0 and t_len % a_steps == 0
    grid = (n_seq // nseq, t_len // a_steps)
    const2 = lambda g, t: (0, 0)
    const3 = lambda g, t: (0, 0, 0)
    any_spec = pl.BlockSpec(memory_space=pl.ANY)

    weights = [p["norm_a"], p["w_in_a"], p["conv_w"], p["conv_b"], p["w_gate"], p["b_gx"],
               p["b_ga"], p["lam"], p["w_out_a"], p["norm_kv"], p["w_kv"], p["k_norm2"]]
    w_specs = [pl.BlockSpec(w.shape, const3 if w.ndim == 3 else const2,
                            pipeline_mode=pl.Buffered(1)) for w in weights]
    conv_spec = pl.BlockSpec((CONV_W - 1, nseq, D_RNN), lambda g, t: (0, g, 0))
    h_spec = pl.BlockSpec((nseq, D_RNN), lambda g, t: (g, 0))

    out_shape = [
        jax.ShapeDtypeStruct((n_seq, t_len, D_MODEL), F32),
        jax.ShapeDtypeStruct((n_seq, t_len, KV_W), F32),
        jax.ShapeDtypeStruct((n_seq, t_len, KV_W), F32),
        jax.ShapeDtypeStruct((CONV_W - 1, n_seq, D_RNN), F32),
        jax.ShapeDtypeStruct((n_seq, D_RNN), F32),
    ]
    scratch = [
        pltpu.VMEM((2, a_steps, nseq, D_MODEL), F32),
        pltpu.VMEM((2, a_steps, nseq, D_MODEL), F32),
        pltpu.VMEM((2, a_steps, nseq, KV_W), F32),
        pltpu.VMEM((2, a_steps, nseq, KV_W), F32),
        pltpu.SemaphoreType.DMA((2,)),
        pltpu.SemaphoreType.DMA((2,)),
        pltpu.VMEM((a_rows, D_MODEL), BF16),
        pltpu.VMEM((conv_pre + a_rows, D_RNN), F32),
        pltpu.VMEM((a_rows, D_RNN), F32),
        pltpu.VMEM((a_rows, D_RNN), F32),
        pltpu.VMEM((a_rows, D_RNN), BF16),
        pltpu.VMEM((a_rows, D_RNN), F32),
        pltpu.VMEM((a_rows, D_RNN), F32),
        pltpu.VMEM((a_rows, D_RNN), BF16),
        pltpu.VMEM((nseq, D_RNN), F32),
    ]
    return pl.pallas_call(
        functools.partial(_layer_a_kernel, a_steps=a_steps, nseq=nseq),
        grid=grid,
        in_specs=[any_spec, conv_spec, h_spec] + w_specs,
        out_specs=[any_spec, any_spec, any_spec, conv_spec, h_spec],
        out_shape=out_shape,
        scratch_shapes=scratch,
        compiler_params=pltpu.CompilerParams(
            dimension_semantics=("arbitrary", "arbitrary"), vmem_limit_bytes=VMEM_LIMIT_BYTES),
        name=name,
    )(x, conv_in, h_in, *weights)


def _layer_b_kernel(x1_ref, kh_ref, kc_ref, vh_ref, vc_ref, norm_b_ref, w_in_ref,
                    qg_ref, sinks_ref, w_out_ref, y_ref,
                    xn_scr, gate_scr, qh_scr, kmat_scr, vmat_scr, o_scr, og_scr,
                    *, nseg, seg, mask_first_halo):
    m = nseg * seg
    n_chunks = seg // CHUNK
    part = min(m, B_PART_ROWS)
    assert m % part == 0 and (part % seg == 0 or seg % part == 0)
    n_parts = m // part
    n_col_blocks = D_MODEL // MXU_N

    _norm_rows_to(x1_ref, norm_b_ref, xn_scr, m, 128)

    lo_m = lax.broadcasted_iota(jnp.int32, (part, LANES), 1) < HEAD_DIM
    qg = qg_ref[...]

    def q_block(h, n):
        rows = slice(h * part, (h + 1) * part)
        qn = jnp.dot(xn_scr[rows, :], w_in_ref[:, n * MXU_N:(n + 1) * MXU_N],
                     preferred_element_type=F32)
        for cc in range(MXU_N // LANES):
            col = n * MXU_N + cc * LANES
            x = qn[:, cc * LANES:(cc + 1) * LANES]
            sq = x * x
            ms_lo = jnp.sum(jnp.where(lo_m, sq, 0.0), axis=-1, keepdims=True) * (1.0 / HEAD_DIM)
            ms_hi = jnp.sum(jnp.where(lo_m, 0.0, sq), axis=-1, keepdims=True) * (1.0 / HEAD_DIM)
            rs = jnp.where(lo_m, lax.rsqrt(ms_lo + EPS), lax.rsqrt(ms_hi + EPS))
            qh_scr[rows, col:col + LANES] = (x * rs * qg[:, col:col + LANES]).astype(BF16)

    row_split = 2 if part >= 2 * MXU_N else 1

    def gate_piece(h, n, rh):
        r0 = h * part + rh * (part // row_split)
        rows = slice(r0, r0 + part // row_split)
        gate_scr[rows, n * MXU_N:(n + 1) * MXU_N] = jnp.dot(
            xn_scr[rows, :], w_in_ref[:, D_MODEL + n * MXU_N:D_MODEL + (n + 1) * MXU_N],
            preferred_element_type=F32)

    def gate_multiply(h):
        blk = 32
        for r in range(h * part, (h + 1) * part, blk):
            og_scr[r:r + blk, :] = (
                o_scr[r:r + blk, :] * _silu_of_half(gate_scr[r:r + blk, :])).astype(BF16)

    def out_piece(h, n):
        rows = slice(h * part, (h + 1) * part)
        cols = slice(n * MXU_N, (n + 1) * MXU_N)
        y_ref[rows, cols] = x1_ref[rows, cols] + jnp.dot(
            og_scr[rows, :], w_out_ref[:, cols], preferred_element_type=F32)

    lane = lax.broadcasted_iota(jnp.int32, (CHUNK, KV_W), 1)
    lo = lane < HEAD_DIM
    first = pl.program_id(1) == 0 if mask_first_halo else None

    for s in range(nseg):
        for r0, nrows, kval, vval in ((0, WINDOW, kh_ref[s], vh_ref[s]),
                                      (WINDOW, seg, kc_ref[s * seg:(s + 1) * seg, :],
                                       vc_ref[s * seg:(s + 1) * seg, :])):
            is_lo = lax.broadcasted_iota(jnp.int32, kval.shape, 1) < HEAD_DIM
            ind_e = jnp.where(is_lo, 1.0, 0.0).astype(BF16)
            ind_o = jnp.where(is_lo, 0.0, 1.0).astype(BF16)
            rows = slice(r0, r0 + nrows)
            for val, scr in ((kval, kmat_scr), (vval, vmat_scr)):
                swapped = pltpu.roll(val, HEAD_DIM, 1)
                for g in range(N_KV):
                    low = val if g == 0 else swapped
                    high = swapped if g == 0 else val
                    scr[s, g, 0, rows, 0:KV_W] = jnp.where(is_lo, low, 0.0).astype(BF16)
                    scr[s, g, 1, rows, 0:KV_W] = jnp.where(is_lo, 0.0, high).astype(BF16)
            for g in range(N_KV):
                vmat_scr[s, g, 0, rows, KV_W:] = ind_e
                vmat_scr[s, g, 1, rows, KV_W:] = ind_o

    def placed(scr, s, g, k0):
        return jnp.concatenate(
            [scr[s, g, 0, k0:k0 + WINDOW, :], scr[s, g, 1, k0:k0 + WINDOW, :],
             scr[s, g, 0, k0 + WINDOW:k0 + WINDOW + CHUNK, :],
             scr[s, g, 1, k0 + WINDOW:k0 + WINDOW + CHUNK, :]], axis=0)

    def scores(s, c, g):
        q0 = s * seg + c * CHUNK
        qst = jnp.concatenate(
            [qh_scr[q0:q0 + CHUNK, g * GROUP * HEAD_DIM + j * PAIR_W:
                    g * GROUP * HEAD_DIM + (j + 1) * PAIR_W] for j in range(N_PAIRS)],
            axis=0)
        return lax.dot_general(qst, placed(kmat_scr, s, g, c * CHUNK),
                               (((1,), (1,)), ((), ())),
                               preferred_element_type=F32)

    def softmax_pv(sc, s, c, g):
        p_rows, sink_rows = [], []
        for j in range(N_PAIRS):
            sj = sc[j * CHUNK:(j + 1) * CHUNK]
            c0 = sj[:, 0:KV_W]
            c1 = sj[:, KV_W:2 * KV_W]
            c2 = sj[:, 2 * KV_W:]
            if mask_first_halo and c * CHUNK < WINDOW:
                n_bad = jnp.where(first, WINDOW - c * CHUNK, 0)
                bad = lane < n_bad
                c0 = jnp.where(bad, NEG, c0)
                c1 = jnp.where(bad, NEG, c1)
            sink_e = sinks_ref[g * GROUP + 2 * j] * LOG2_E
            sink_o = sinks_ref[g * GROUP + 2 * j + 1] * LOG2_E
            m_e = jnp.maximum(jnp.max(jnp.maximum(c0, jnp.where(lo, c2, NEG)),
                                      axis=-1, keepdims=True), sink_e)
            m_o = jnp.maximum(jnp.max(jnp.maximum(c1, jnp.where(lo, NEG, c2)),
                                      axis=-1, keepdims=True), sink_o)
            p_rows.append(jnp.concatenate(
                [jnp.exp2(c0 - m_e), jnp.exp2(c1 - m_o),
                 jnp.exp2(c2 - jnp.where(lo, m_e, m_o))], axis=1).astype(BF16))
            sink_rows.append(jnp.where(lo, jnp.exp2(sink_e - m_e), jnp.exp2(sink_o - m_o)))
        pmat = jnp.concatenate(p_rows, axis=0)
        ov = jnp.dot(pmat, placed(vmat_scr, s, g, c * CHUNK),
                     preferred_element_type=F32)
        q0 = s * seg + c * CHUNK
        for j in range(N_PAIRS):
            rows = slice(j * CHUNK, (j + 1) * CHUNK)
            c_lo = g * GROUP * HEAD_DIM + j * PAIR_W
            o_scr[q0:q0 + CHUNK, c_lo:c_lo + PAIR_W] = (
                ov[rows, :KV_W] / (ov[rows, KV_W:] + sink_rows[j]))

    all_units = [(s, c, g) for s in range(nseg) for c in range(n_chunks) for g in range(N_KV)]

    for n in range(n_col_blocks):
        q_block(0, n)
    for h in range(n_parts):
        units = [u for u in all_units if (u[0] * seg + u[1] * CHUNK) // part == h]
        fillers = []
        if h >= 1:
            fillers += [functools.partial(out_piece, h - 1, n) for n in range(n_col_blocks)]
        gates = [functools.partial(gate_piece, h, n, rh)
                 for n in range(n_col_blocks) for rh in range(row_split)]
        nxt = ([functools.partial(q_block, h + 1, n) for n in range(n_col_blocks)]
               if h + 1 < n_parts else [])
        while gates or nxt:
            fillers += gates[:2]
            gates = gates[2:]
            fillers += nxt[:1]
            nxt = nxt[1:]
        fillers.pop(0)()
        per_unit = -(-len(fillers) // len(units))
        pending = [scores(*u) for u in units[:LOOKAHEAD]]
        for k, (s, c, g) in enumerate(units):
            sc = pending.pop(0)
            if k + LOOKAHEAD < len(units):
                pending.append(scores(*units[k + LOOKAHEAD]))
            for _ in range(min(per_unit, len(fillers))):
                fillers.pop(0)()
            softmax_pv(sc, s, c, g)
        assert not fillers
        gate_multiply(h)
    for n in range(n_col_blocks):
        out_piece(n_parts - 1, n)


def _layer_b(x1, k_halo, v_halo, k_cur, v_cur, p, *, n_batch, t_len, nseg, seg, prompt):
    m = nseg * seg
    rows = n_batch * t_len
    const2 = lambda *_: (0, 0)
    if prompt:
        steps = t_len // m
        per_b = t_len // WINDOW
        grid = (n_batch, steps)
        row_map = lambda b, t: (b * steps + t, 0)
        halo_map = lambda b, t: (b * per_b + jnp.maximum(t * (m // WINDOW) - 1, 0), 0, 0)
    else:
        grid = (n_batch // nseg, 1)
        row_map = lambda i, t: (i, 0)
        halo_map = lambda i, t: (i, 0, 0)

    weights = [p["norm_b"], p["w_in_b"], p["qg"]]
    in_specs = [
        pl.BlockSpec((m, D_MODEL), row_map),
        pl.BlockSpec((nseg, WINDOW, KV_W), halo_map),
        pl.BlockSpec((m, KV_W), row_map),
        pl.BlockSpec((nseg, WINDOW, KV_W), halo_map),
        pl.BlockSpec((m, KV_W), row_map),
    ] + [pl.BlockSpec(w.shape, const2) for w in weights] + [
        pl.BlockSpec(memory_space=pltpu.SMEM),
        pl.BlockSpec(p["w_out_b"].shape, const2),
    ]
    kv_rows = WINDOW + seg
    scratch = [
        pltpu.VMEM((m, D_MODEL), BF16),
        pltpu.VMEM((m, D_MODEL), F32),
        pltpu.VMEM((m, D_MODEL), BF16),
        pltpu.VMEM((nseg, N_KV, 2, kv_rows, KV_W), BF16),
        pltpu.VMEM((nseg, N_KV, 2, kv_rows, 2 * KV_W), BF16),
        pltpu.VMEM((m, D_MODEL), F32),
        pltpu.VMEM((m, D_MODEL), BF16),
    ]
    return pl.pallas_call(
        functools.partial(_layer_b_kernel, nseg=nseg, seg=seg, mask_first_halo=prompt),
        grid=grid,
        in_specs=in_specs,
        out_specs=pl.BlockSpec((m, D_MODEL), row_map),
        out_shape=jax.ShapeDtypeStruct((rows, D_MODEL), F32),
        scratch_shapes=scratch,
        compiler_params=pltpu.CompilerParams(
            dimension_semantics=("arbitrary", "arbitrary"), vmem_limit_bytes=VMEM_LIMIT_BYTES),
        name="layer_b_prompt" if prompt else "layer_b_sample",
    )(x1, k_halo, k_cur, v_halo, v_cur, *weights, p["sinks"], p["w_out_b"])


def _prep_params(norm_a, w_in_a, conv_w, conv_b, w_gate_x, b_gate_x, w_gate_a, b_gate_a,
                 lru_lambda, w_out_a, norm_kv, w_kv, k_norm, norm_b, w_in_b, q_norm, sinks,
                 w_out_b):
    row = lambda v: v.reshape(1, -1).astype(F32)
    in_scale = jnp.concatenate([jnp.ones((D_MODEL,), F32), jnp.full((D_MODEL,), 0.5, F32)])
    return {
        "norm_a": row(norm_a[0]),
        "w_in_a": (w_in_a[0] * in_scale).astype(BF16),
        "conv_w": conv_w[0].astype(F32),
        "conv_b": row(conv_b[0]),
        "w_gate": (0.5 * jnp.concatenate([w_gate_x[0], w_gate_a[0]], axis=-1)).astype(BF16),
        "b_gx": row(0.5 * b_gate_x[0]),
        "b_ga": row(0.5 * b_gate_a[0]),
        "lam": row(lru_lambda[0]),
        "w_out_a": w_out_a[0].astype(BF16),
        "norm_kv": row(norm_kv),
        "w_kv": w_kv.astype(BF16),
        "k_norm2": row(jnp.tile(k_norm, N_KV)),
        "norm_b": row(norm_b[0]),
        "w_in_b": (w_in_b[0] * in_scale).astype(BF16),
        "qg": row(jnp.tile(q_norm[0], N_HEADS) * (HEAD_DIM ** -0.5 * LOG2_E)),
        "sinks": sinks[0].astype(F32),
        "w_out_b": w_out_b[0].astype(BF16),
    }


def kernel(x_prompt, x_sample, state_conv, state_rglru, cache_k_win, cache_v_win, norm_a, w_in_a, conv_w, conv_b, w_gate_x, b_gate_x, w_gate_a, b_gate_a, lru_lambda, w_out_a, norm_kv, w_kv, k_norm, norm_b, w_in_b, q_norm, sinks, w_out_b):
    assert norm_a.shape[0] == 1 and norm_b.shape[0] == 1, "one recurrent + one attention layer"
    p = _prep_params(norm_a, w_in_a, conv_w, conv_b, w_gate_x, b_gate_x, w_gate_a, b_gate_a,
                     lru_lambda, w_out_a, norm_kv, w_kv, k_norm, norm_b, w_in_b, q_norm, sinks,
                     w_out_b)
    bp, tp, _ = x_prompt.shape
    bs, ts, _ = x_sample.shape

    x1p, kp, vp, conv_p, h_p = _layer_a(
        x_prompt, jnp.zeros((CONV_W - 1, bp, D_RNN), F32), jnp.zeros((bp, D_RNN), F32), p,
        "layer_a_prompt")
    tile_b = 1024
    kp2 = kp.reshape(bp * tp, KV_W)
    vp2 = vp.reshape(bp * tp, KV_W)
    kp3 = kp.reshape(bp * tp // WINDOW, WINDOW, KV_W)
    vp3 = vp.reshape(bp * tp // WINDOW, WINDOW, KV_W)
    y_p = _layer_b(x1p.reshape(bp * tp, D_MODEL), kp3, vp3, kp2, vp2, p, n_batch=bp, t_len=tp,
                   nseg=1, seg=tile_b, prompt=True)

    x1s, ks, vs, conv_s, h_s = _layer_a(
        x_sample, jnp.transpose(state_conv[0], (1, 0, 2)), state_rglru[0], p, "layer_a_sample")
    y_s = _layer_b(x1s.reshape(bs * ts, D_MODEL), cache_k_win.reshape(bs, WINDOW, KV_W),
                   cache_v_win.reshape(bs, WINDOW, KV_W), ks.reshape(bs * ts, KV_W),
                   vs.reshape(bs * ts, KV_W), p, n_batch=bs, t_len=ts, nseg=16, seg=ts,
                   prompt=False)

    kp4 = kp.reshape(bp, tp, N_KV, HEAD_DIM)[:, -WINDOW:]
    vp4 = vp.reshape(bp, tp, N_KV, HEAD_DIM)[:, -WINDOW:]
    ks4 = jnp.concatenate([cache_k_win, ks.reshape(bs, ts, N_KV, HEAD_DIM)], axis=1)[:, -WINDOW:]
    vs4 = jnp.concatenate([cache_v_win, vs.reshape(bs, ts, N_KV, HEAD_DIM)], axis=1)[:, -WINDOW:]
    return (y_p.reshape(bp, tp, D_MODEL), y_s.reshape(bs, ts, D_MODEL),
            jnp.transpose(conv_p, (1, 0, 2))[None], h_p[None], kp4, vp4,
            jnp.transpose(conv_s, (1, 0, 2))[None], h_s[None], ks4, vs4)
```

```python
import functools

import jax
import jax.numpy as jnp
from jax import lax
from jax.experimental import pallas as pl
from jax.experimental.pallas import tpu as pltpu

D_MODEL = 1024
D_RNN = 1024
GATE_BLOCK = 256
N_GATE_BLOCKS = D_RNN // GATE_BLOCK
CONV_W = 4
LRU_C = 8.0
HEAD_DIM = 64
N_HEADS = 16
N_KV = 2
GROUP = N_HEADS // N_KV
CHUNK = 64
WINDOW = 128
EPS = 1e-6
NEG = -1e30
LOG2_E = 1.4426950408889634
LN_2 = 0.6931471805599453

SUBLANES = 8
LANES = 128
BF16_SUBLANES = 16
MXU_N = 256
VMEM_LIMIT_BYTES = 56 * 1024 * 1024

F32 = jnp.float32
BF16 = jnp.bfloat16

KV_W = N_KV * HEAD_DIM
PAIR_W = 2 * HEAD_DIM
N_PAIRS = GROUP // 2
LOOKAHEAD = 2
B_PART_ROWS = 256

A_MAX_STEPS = 128
A_TILE_ROWS = 1024
OUT_K_SPLIT = D_RNN // 2


def _silu_of_half(h):
    return h * jnp.tanh(h) + h


def _norm_rows_to(x_ref, dst_ref, n_rows, blk, r0=0):
    for r in range(r0, r0 + n_rows, blk):
        x = x_ref[r:r + blk, :]
        rs = lax.rsqrt(jnp.mean(x * x, axis=-1, keepdims=True) + EPS)
        dst_ref[r:r + blk, :] = (x_ref[r:r + blk, :] * rs).astype(BF16)


def _layer_a_kernel(x_hbm, conv_in_ref, h_in_ref,
                    w_in_ref, conv_w_ref, conv_b_ref, w_gate_ref, b_gx_ref, b_ga_ref,
                    lam_ref, w_out_ref, w_kv_ref, k_norm_ref,
                    x1_hbm, k_hbm, v_hbm, conv_out_ref, h_out_ref,
                    xin, x1o, ko, vo, sem_in, sem_out,
                    xn_scr, xbp_scr, gate_scr, xc_scr, xcb_scr, gx_scr, ga_scr, hy_scr, h_scr,
                    *, a_steps, nseq):
    a_rows = a_steps * nseq
    conv_pre = (CONV_W - 1) * nseq
    g = pl.program_id(0)
    t = pl.program_id(1)
    nt = pl.num_programs(1)
    total = pl.num_programs(0) * nt
    i = g * nt + t
    slot = lax.rem(i, 2)

    def in_copies(step, sl):
        gg = step // nt
        tt = lax.rem(step, nt)
        return [pltpu.make_async_copy(x_hbm.at[gg * nseq + j, pl.ds(tt * a_steps, a_steps), :],
                                      xin.at[sl, :, j, :], sem_in.at[sl]) for j in range(nseq)]

    def out_copies(step, sl):
        gg = step // nt
        tt = lax.rem(step, nt)
        cps = []
        for buf, dst in ((x1o, x1_hbm), (ko, k_hbm), (vo, v_hbm)):
            cps += [pltpu.make_async_copy(buf.at[sl, :, j, :],
                                          dst.at[gg * nseq + j, pl.ds(tt * a_steps, a_steps), :],
                                          sem_out.at[sl]) for j in range(nseq)]
        return cps

    @pl.when(i == 0)
    def _():
        for c in in_copies(i, slot):
            c.start()

    @pl.when(i + 1 < total)
    def _():
        for c in in_copies(i + 1, 1 - slot):
            c.start()

    @pl.when(t == 0)
    def _():
        xbp_scr[0:conv_pre, :] = conv_in_ref[...].reshape(conv_pre, D_RNN)
        h_scr[...] = h_in_ref[...]

    @pl.when(i >= 2)
    def _():
        for c in out_copies(i - 2, slot):
            c.wait()

    for c in in_copies(i, slot):
        c.wait()

    x_ref = xin.at[slot].reshape(a_rows, D_MODEL)
    x1_ref = x1o.at[slot].reshape(a_rows, D_MODEL)
    k_ref = ko.at[slot].reshape(a_rows, KV_W)
    v_ref = vo.at[slot].reshape(a_rows, KV_W)

    _norm_rows_to(x_ref, xn_scr, a_rows, 128)
    lam = lam_ref[...]
    log_sig_lam = jnp.minimum(lam, 0.0) - jnp.log1p(jnp.exp(-jnp.abs(lam)))
    kk = jnp.broadcast_to((0.5 * LRU_C * LOG2_E) * log_sig_lam, (nseq, D_RNN))
    b_gx = jnp.broadcast_to(b_gx_ref[...], (nseq, D_RNN))
    b_ga = jnp.broadcast_to(b_ga_ref[...], (nseq, D_RNN))

    cw = conv_w_ref[...]
    cb = conv_b_ref[...]
    conv_blk = 32

    def in_proj_x(n):
        cols = slice(n * GATE_BLOCK, (n + 1) * GATE_BLOCK)
        xbp_scr[conv_pre:, cols] = jnp.dot(xn_scr[...], w_in_ref[:, cols],
                                           preferred_element_type=F32)

    def in_proj_gate(n):
        cols = slice(n * GATE_BLOCK, (n + 1) * GATE_BLOCK)
        gate_scr[:, cols] = jnp.dot(
            xn_scr[...], w_in_ref[:, D_RNN + n * GATE_BLOCK:D_RNN + (n + 1) * GATE_BLOCK],
            preferred_element_type=F32)

    def conv_and_gates(n):
        cols = slice(n * GATE_BLOCK, (n + 1) * GATE_BLOCK)
        for r in range(0, a_rows, conv_blk):
            acc = cb[:, cols]
            for j in range(CONV_W):
                acc = acc + cw[j:j + 1, cols] * xbp_scr[r + j * nseq:r + j * nseq + conv_blk, cols]
            xc_scr[r:r + conv_blk, cols] = acc
            xcb_scr[r:r + conv_blk, cols] = acc.astype(BF16)
        tail = xbp_scr[a_rows:a_rows + conv_pre, cols]
        conv_out_ref[:, :, cols] = tail.reshape(CONV_W - 1, nseq, GATE_BLOCK)
        xbp_scr[0:conv_pre, cols] = tail

        res = jnp.dot(xcb_scr[:, cols], w_gate_ref[n], preferred_element_type=F32)
        gx_scr[:, cols] = res[:, :GATE_BLOCK]
        ga_scr[:, cols] = res[:, GATE_BLOCK:]

    def scan(n):
        cols = slice(n * GATE_BLOCK, (n + 1) * GATE_BLOCK)
        kk_n = kk[:, cols]
        bgx_n = b_gx[:, cols]
        bga_n = b_ga[:, cols]
        h = h_scr[:, cols]
        rows_per_store = max(BF16_SUBLANES, nseq)
        for r in range(0, a_rows, rows_per_store):
            hy = []
            for rr in range(r, r + rows_per_store, nseq):
                t_i = jnp.tanh(gx_scr[rr:rr + nseq, cols] + bgx_n)
                t_r = jnp.tanh(ga_scr[rr:rr + nseq, cols] + bga_n)
                log2_a = t_r * kk_n + kk_n
                a = jnp.exp2(log2_a)
                y = jnp.tanh(log2_a * (-LN_2)) * (a * a + 1.0)
                mult = jnp.where(y > 0.0, y * lax.rsqrt(y), 0.0)
                u = mult * xc_scr[rr:rr + nseq, cols]
                h = a * h + (u * t_i + u)
                hy.append(h * _silu_of_half(gate_scr[rr:rr + nseq, cols]))
            hy_scr[r:r + rows_per_store, cols] = jnp.concatenate(hy, axis=0).astype(BF16)
        h_scr[:, cols] = h
        h_out_ref[:, cols] = h

    def out_proj_first_part():
        for nn in range(D_MODEL // MXU_N):
            cols = slice(nn * MXU_N, (nn + 1) * MXU_N)
            x1_ref[:, cols] = x_ref[:, cols] + jnp.dot(
                hy_scr[:, :OUT_K_SPLIT], w_out_ref[:OUT_K_SPLIT, cols],
                preferred_element_type=F32)

    in_proj_x(0)
    in_proj_gate(0)
    in_proj_x(1)
    conv_and_gates(0)
    in_proj_gate(1)
    for n in range(N_GATE_BLOCKS):
        if n + 2 < N_GATE_BLOCKS:
            in_proj_x(n + 2)
        if n + 1 < N_GATE_BLOCKS:
            conv_and_gates(n + 1)
        if n + 2 < N_GATE_BLOCKS:
            in_proj_gate(n + 2)
        if (n + 1) * GATE_BLOCK == D_RNN - GATE_BLOCK:
            out_proj_first_part()
        scan(n)

    half_rows = a_rows // 2
    lo = lax.broadcasted_iota(jnp.int32, (half_rows, KV_W), 1) < HEAD_DIM
    for rh in range(2):
        rows = slice(rh * half_rows, (rh + 1) * half_rows)
        for n in range(D_MODEL // MXU_N):
            cols = slice(n * MXU_N, (n + 1) * MXU_N)
            x1_ref[rows, cols] = x1_ref[rows, cols] + jnp.dot(
                hy_scr[rows, OUT_K_SPLIT:], w_out_ref[OUT_K_SPLIT:, cols],
                preferred_element_type=F32)
        _norm_rows_to(x1_ref, xn_scr, half_rows, 64, r0=rh * half_rows)
        kv = jnp.dot(xn_scr[rows, :], w_kv_ref[...], preferred_element_type=F32)
        k = kv[:, :KV_W]
        k2 = k * k
        ms_lo = jnp.sum(jnp.where(lo, k2, 0.0), axis=-1, keepdims=True) * (1.0 / HEAD_DIM)
        ms_hi = jnp.sum(jnp.where(lo, 0.0, k2), axis=-1, keepdims=True) * (1.0 / HEAD_DIM)
        rs = jnp.where(lo, lax.rsqrt(ms_lo + EPS), lax.rsqrt(ms_hi + EPS))
        k_ref[rows, :] = k * rs * k_norm_ref[...]
        v_ref[rows, :] = kv[:, KV_W:]

    for c in out_copies(i, slot):
        c.start()

    @pl.when(i == total - 1)
    def _():
        @pl.when(i >= 1)
        def _():
            for c in out_copies(i - 1, 1 - slot):
                c.wait()
        for c in out_copies(i, slot):
            c.wait()


def _layer_a(x, conv_in, h_in, p, name):
    n_seq, t_len, _ = x.shape
    a_steps = min(t_len, A_MAX_STEPS)
    nseq = min(n_seq, max(SUBLANES, A_TILE_ROWS // a_steps), BF16_SUBLANES)
    a_rows = a_steps * nseq
    conv_pre = (CONV_W - 1) * nseq
    assert nseq % SUBLANES == 0 and n_seq % nseq == 0 and t_len % a_steps == 0
    grid = (n_seq // nseq, t_len // a_steps)
    const2 = lambda g, t: (0, 0)
    const3 = lambda g, t: (0, 0, 0)
    any_spec = pl.BlockSpec(memory_space=pl.ANY)

    weights = [p["w_in_a"], p["conv_w"], p["conv_b"], p["w_gate"], p["b_gx"],
               p["b_ga"], p["lam"], p["w_out_a"], p["w_kv"], p["k_norm2"]]
    w_specs = [pl.BlockSpec(w.shape, const3 if w.ndim == 3 else const2,
                            pipeline_mode=pl.Buffered(1)) for w in weights]
    conv_spec = pl.BlockSpec((CONV_W - 1, nseq, D_RNN), lambda g, t: (0, g, 0))
    h_spec = pl.BlockSpec((nseq, D_RNN), lambda g, t: (g, 0))

    out_shape = [
        jax.ShapeDtypeStruct((n_seq, t_len, D_MODEL), F32),
        jax.ShapeDtypeStruct((n_seq, t_len, KV_W), F32),
        jax.ShapeDtypeStruct((n_seq, t_len, KV_W), F32),
        jax.ShapeDtypeStruct((CONV_W - 1, n_seq, D_RNN), F32),
        jax.ShapeDtypeStruct((n_seq, D_RNN), F32),
    ]
    scratch = [
        pltpu.VMEM((2, a_steps, nseq, D_MODEL), F32),
        pltpu.VMEM((2, a_steps, nseq, D_MODEL), F32),
        pltpu.VMEM((2, a_steps, nseq, KV_W), F32),
        pltpu.VMEM((2, a_steps, nseq, KV_W), F32),
        pltpu.SemaphoreType.DMA((2,)),
        pltpu.SemaphoreType.DMA((2,)),
        pltpu.VMEM((a_rows, D_MODEL), BF16),
        pltpu.VMEM((conv_pre + a_rows, D_RNN), F32),
        pltpu.VMEM((a_rows, D_RNN), F32),
        pltpu.VMEM((a_rows, D_RNN), F32),
        pltpu.VMEM((a_rows, D_RNN), BF16),
        pltpu.VMEM((a_rows, D_RNN), F32),
        pltpu.VMEM((a_rows, D_RNN), F32),
        pltpu.VMEM((a_rows, D_RNN), BF16),
        pltpu.VMEM((nseq, D_RNN), F32),
    ]
    return pl.pallas_call(
        functools.partial(_layer_a_kernel, a_steps=a_steps, nseq=nseq),
        grid=grid,
        in_specs=[any_spec, conv_spec, h_spec] + w_specs,
        out_specs=[any_spec, any_spec, any_spec, conv_spec, h_spec],
        out_shape=out_shape,
        scratch_shapes=scratch,
        compiler_params=pltpu.CompilerParams(
            dimension_semantics=("arbitrary", "arbitrary"), vmem_limit_bytes=VMEM_LIMIT_BYTES),
        name=name,
    )(x, conv_in, h_in, *weights)


def _layer_b_kernel(x1_ref, kh_ref, kc_ref, vh_ref, vc_ref, w_in_ref,
                    qg_ref, sinks_ref, w_out_ref, y_ref,
                    xn_scr, gate_scr, qh_scr, kmat_scr, vmat_scr, o_scr, og_scr,
                    *, nseg, seg, mask_first_halo):
    m = nseg * seg
    n_chunks = seg // CHUNK
    part = min(m, B_PART_ROWS)
    assert m % part == 0 and (part % seg == 0 or seg % part == 0)
    n_parts = m // part
    n_col_blocks = D_MODEL // MXU_N

    _norm_rows_to(x1_ref, xn_scr, m, 128)

    lo_m = lax.broadcasted_iota(jnp.int32, (part, LANES), 1) < HEAD_DIM
    qg = qg_ref[...]

    def q_block(h, n):
        rows = slice(h * part, (h + 1) * part)
        qn = jnp.dot(xn_scr[rows, :], w_in_ref[:, n * MXU_N:(n + 1) * MXU_N],
                     preferred_element_type=F32)
        for cc in range(MXU_N // LANES):
            col = n * MXU_N + cc * LANES
            x = qn[:, cc * LANES:(cc + 1) * LANES]
            sq = x * x
            ms_lo = jnp.sum(jnp.where(lo_m, sq, 0.0), axis=-1, keepdims=True) * (1.0 / HEAD_DIM)
            ms_hi = jnp.sum(jnp.where(lo_m, 0.0, sq), axis=-1, keepdims=True) * (1.0 / HEAD_DIM)
            rs = jnp.where(lo_m, lax.rsqrt(ms_lo + EPS), lax.rsqrt(ms_hi + EPS))
            qh_scr[rows, col:col + LANES] = (x * rs * qg[:, col:col + LANES]).astype(BF16)

    row_split = 2 if part >= 2 * MXU_N else 1

    def gate_piece(h, n, rh):
        r0 = h * part + rh * (part // row_split)
        rows = slice(r0, r0 + part // row_split)
        gate_scr[rows, n * MXU_N:(n + 1) * MXU_N] = jnp.dot(
            xn_scr[rows, :], w_in_ref[:, D_MODEL + n * MXU_N:D_MODEL + (n + 1) * MXU_N],
            preferred_element_type=F32)

    def gate_multiply(h):
        blk = 32
        for r in range(h * part, (h + 1) * part, blk):
            og_scr[r:r + blk, :] = (
                o_scr[r:r + blk, :] * _silu_of_half(gate_scr[r:r + blk, :])).astype(BF16)

    def out_piece(h, n):
        rows = slice(h * part, (h + 1) * part)
        cols = slice(n * MXU_N, (n + 1) * MXU_N)
        y_ref[rows, cols] = x1_ref[rows, cols] + jnp.dot(
            og_scr[rows, :], w_out_ref[:, cols], preferred_element_type=F32)

    lane = lax.broadcasted_iota(jnp.int32, (CHUNK, KV_W), 1)
    lo = lane < HEAD_DIM
    first = pl.program_id(1) == 0 if mask_first_halo else None

    for s in range(nseg):
        for r0, nrows, kval, vval in ((0, WINDOW, kh_ref[s], vh_ref[s]),
                                      (WINDOW, seg, kc_ref[s * seg:(s + 1) * seg, :],
                                       vc_ref[s * seg:(s + 1) * seg, :])):
            is_lo = lax.broadcasted_iota(jnp.int32, kval.shape, 1) < HEAD_DIM
            ind_e = jnp.where(is_lo, 1.0, 0.0).astype(BF16)
            ind_o = jnp.where(is_lo, 0.0, 1.0).astype(BF16)
            rows = slice(r0, r0 + nrows)
            for val, scr in ((kval, kmat_scr), (vval, vmat_scr)):
                swapped = pltpu.roll(val, HEAD_DIM, 1)
                for g in range(N_KV):
                    low = val if g == 0 else swapped
                    high = swapped if g == 0 else val
                    scr[s, g, 0, rows, 0:KV_W] = jnp.where(is_lo, low, 0.0).astype(BF16)
                    scr[s, g, 1, rows, 0:KV_W] = jnp.where(is_lo, 0.0, high).astype(BF16)
            for g in range(N_KV):
                vmat_scr[s, g, 0, rows, KV_W:] = ind_e
                vmat_scr[s, g, 1, rows, KV_W:] = ind_o

    def placed(scr, s, g, k0):
        return jnp.concatenate(
            [scr[s, g, 0, k0:k0 + WINDOW, :], scr[s, g, 1, k0:k0 + WINDOW, :],
             scr[s, g, 0, k0 + WINDOW:k0 + WINDOW + CHUNK, :],
             scr[s, g, 1, k0 + WINDOW:k0 + WINDOW + CHUNK, :]], axis=0)

    def scores(s, c, g):
        q0 = s * seg + c * CHUNK
        qst = jnp.concatenate(
            [qh_scr[q0:q0 + CHUNK, g * GROUP * HEAD_DIM + j * PAIR_W:
                    g * GROUP * HEAD_DIM + (j + 1) * PAIR_W] for j in range(N_PAIRS)],
            axis=0)
        return lax.dot_general(qst, placed(kmat_scr, s, g, c * CHUNK),
                               (((1,), (1,)), ((), ())),
                               preferred_element_type=F32)

    def softmax_pv(sc, s, c, g):
        p_rows, sink_rows = [], []
        for j in range(N_PAIRS):
            sj = sc[j * CHUNK:(j + 1) * CHUNK]
            c0 = sj[:, 0:KV_W]
            c1 = sj[:, KV_W:2 * KV_W]
            c2 = sj[:, 2 * KV_W:]
            if mask_first_halo and c * CHUNK < WINDOW:
                n_bad = jnp.where(first, WINDOW - c * CHUNK, 0)
                bad = lane < n_bad
                c0 = jnp.where(bad, NEG, c0)
                c1 = jnp.where(bad, NEG, c1)
            sink_e = sinks_ref[g * GROUP + 2 * j] * LOG2_E
            sink_o = sinks_ref[g * GROUP + 2 * j + 1] * LOG2_E
            m_e = jnp.maximum(jnp.max(jnp.maximum(c0, jnp.where(lo, c2, NEG)),
                                      axis=-1, keepdims=True), sink_e)
            m_o = jnp.maximum(jnp.max(jnp.maximum(c1, jnp.where(lo, NEG, c2)),
                                      axis=-1, keepdims=True), sink_o)
            p_rows.append(jnp.concatenate(
                [jnp.exp2(c0 - m_e), jnp.exp2(c1 - m_o),
                 jnp.exp2(c2 - jnp.where(lo, m_e, m_o))], axis=1).astype(BF16))
            sink_rows.append(jnp.where(lo, jnp.exp2(sink_e - m_e), jnp.exp2(sink_o - m_o)))
        pmat = jnp.concatenate(p_rows, axis=0)
        ov = jnp.dot(pmat, placed(vmat_scr, s, g, c * CHUNK),
                     preferred_element_type=F32)
        q0 = s * seg + c * CHUNK
        for j in range(N_PAIRS):
            rows = slice(j * CHUNK, (j + 1) * CHUNK)
            c_lo = g * GROUP * HEAD_DIM + j * PAIR_W
            o_scr[q0:q0 + CHUNK, c_lo:c_lo + PAIR_W] = (
                ov[rows, :KV_W] / (ov[rows, KV_W:] + sink_rows[j]))

    all_units = [(s, c, g) for s in range(nseg) for c in range(n_chunks) for g in range(N_KV)]

    for n in range(n_col_blocks):
        q_block(0, n)
    for h in range(n_parts):
        units = [u for u in all_units if (u[0] * seg + u[1] * CHUNK) // part == h]
        fillers = []
        if h >= 1:
            fillers += [functools.partial(out_piece, h - 1, n) for n in range(n_col_blocks)]
        gates = [functools.partial(gate_piece, h, n, rh)
                 for n in range(n_col_blocks) for rh in range(row_split)]
        nxt = ([functools.partial(q_block, h + 1, n) for n in range(n_col_blocks)]
               if h + 1 < n_parts else [])
        while gates or nxt:
            fillers += gates[:2]
            gates = gates[2:]
            fillers += nxt[:1]
            nxt = nxt[1:]
        fillers.pop(0)()
        per_unit = -(-len(fillers) // len(units))
        pending = [scores(*u) for u in units[:LOOKAHEAD]]
        for k, (s, c, g) in enumerate(units):
            sc = pending.pop(0)
            if k + LOOKAHEAD < len(units):
                pending.append(scores(*units[k + LOOKAHEAD]))
            for _ in range(min(per_unit, len(fillers))):
                fillers.pop(0)()
            softmax_pv(sc, s, c, g)
        assert not fillers
        gate_multiply(h)
    for n in range(n_col_blocks):
        out_piece(n_parts - 1, n)


def _layer_b(x1, k_halo, v_halo, k_cur, v_cur, p, *, n_batch, t_len, nseg, seg, prompt):
    m = nseg * seg
    rows = n_batch * t_len
    const2 = lambda *_: (0, 0)
    if prompt:
        steps = t_len // m
        per_b = t_len // WINDOW
        grid = (n_batch, steps)
        row_map = lambda b, t: (b * steps + t, 0)
        halo_map = lambda b, t: (b * per_b + jnp.maximum(t * (m // WINDOW) - 1, 0), 0, 0)
    else:
        grid = (n_batch // nseg, 1)
        row_map = lambda i, t: (i, 0)
        halo_map = lambda i, t: (i, 0, 0)

    weights = [p["w_in_b"], p["qg"]]
    in_specs = [
        pl.BlockSpec((m, D_MODEL), row_map),
        pl.BlockSpec((nseg, WINDOW, KV_W), halo_map),
        pl.BlockSpec((m, KV_W), row_map),
        pl.BlockSpec((nseg, WINDOW, KV_W), halo_map),
        pl.BlockSpec((m, KV_W), row_map),
    ] + [pl.BlockSpec(w.shape, const2) for w in weights] + [
        pl.BlockSpec(memory_space=pltpu.SMEM),
        pl.BlockSpec(p["w_out_b"].shape, const2),
    ]
    kv_rows = WINDOW + seg
    scratch = [
        pltpu.VMEM((m, D_MODEL), BF16),
        pltpu.VMEM((m, D_MODEL), F32),
        pltpu.VMEM((m, D_MODEL), BF16),
        pltpu.VMEM((nseg, N_KV, 2, kv_rows, KV_W), BF16),
        pltpu.VMEM((nseg, N_KV, 2, kv_rows, 2 * KV_W), BF16),
        pltpu.VMEM((m, D_MODEL), F32),
        pltpu.VMEM((m, D_MODEL), BF16),
    ]
    return pl.pallas_call(
        functools.partial(_layer_b_kernel, nseg=nseg, seg=seg, mask_first_halo=prompt),
        grid=grid,
        in_specs=in_specs,
        out_specs=pl.BlockSpec((m, D_MODEL), row_map),
        out_shape=jax.ShapeDtypeStruct((rows, D_MODEL), F32),
        scratch_shapes=scratch,
        compiler_params=pltpu.CompilerParams(
            dimension_semantics=("arbitrary", "arbitrary"), vmem_limit_bytes=VMEM_LIMIT_BYTES),
        name="layer_b_prompt" if prompt else "layer_b_sample",
    )(x1, k_halo, k_cur, v_halo, v_cur, *weights, p["sinks"], p["w_out_b"])


def _prep_params(norm_a, w_in_a, conv_w, conv_b, w_gate_x, b_gate_x, w_gate_a, b_gate_a,
                 lru_lambda, w_out_a, norm_kv, w_kv, k_norm, norm_b, w_in_b, q_norm, sinks,
                 w_out_b):
    row = lambda v: v.reshape(1, -1).astype(F32)
    in_scale = jnp.concatenate([jnp.ones((D_MODEL,), F32), jnp.full((D_MODEL,), 0.5, F32)])
    return {
        "w_in_a": (norm_a[0][:, None] * w_in_a[0] * in_scale).astype(BF16),
        "conv_w": (0.5 * conv_w[0]).astype(F32),
        "conv_b": row(0.5 * conv_b[0]),
        "w_gate": jnp.concatenate([w_gate_x[0], w_gate_a[0]], axis=-1).astype(BF16),
        "b_gx": row(0.5 * b_gate_x[0]),
        "b_ga": row(0.5 * b_gate_a[0]),
        "lam": row(lru_lambda[0]),
        "w_out_a": w_out_a[0].astype(BF16),
        "w_kv": (norm_kv[:, None] * w_kv).astype(BF16),
        "k_norm2": row(jnp.tile(k_norm, N_KV)),
        "w_in_b": (norm_b[0][:, None] * w_in_b[0] * in_scale).astype(BF16),
        "qg": row(jnp.tile(q_norm[0], N_HEADS) * (HEAD_DIM ** -0.5 * LOG2_E)),
        "sinks": sinks[0].astype(F32),
        "w_out_b": w_out_b[0].astype(BF16),
    }


def kernel(x_prompt, x_sample, state_conv, state_rglru, cache_k_win, cache_v_win, norm_a, w_in_a, conv_w, conv_b, w_gate_x, b_gate_x, w_gate_a, b_gate_a, lru_lambda, w_out_a, norm_kv, w_kv, k_norm, norm_b, w_in_b, q_norm, sinks, w_out_b):
    assert norm_a.shape[0] == 1 and norm_b.shape[0] == 1, "one recurrent + one attention layer"
    p = _prep_params(norm_a, w_in_a, conv_w, conv_b, w_gate_x, b_gate_x, w_gate_a, b_gate_a,
                     lru_lambda, w_out_a, norm_kv, w_kv, k_norm, norm_b, w_in_b, q_norm, sinks,
                     w_out_b)
    bp, tp, _ = x_prompt.shape
    bs, ts, _ = x_sample.shape

    x1p, kp, vp, conv_p, h_p = _layer_a(
        x_prompt, jnp.zeros((CONV_W - 1, bp, D_RNN), F32), jnp.zeros((bp, D_RNN), F32), p,
        "layer_a_prompt")
    tile_b = 1024
    kp2 = kp.reshape(bp * tp, KV_W)
    vp2 = vp.reshape(bp * tp, KV_W)
    kp3 = kp.reshape(bp * tp // WINDOW, WINDOW, KV_W)
    vp3 = vp.reshape(bp * tp // WINDOW, WINDOW, KV_W)
    y_p = _layer_b(x1p.reshape(bp * tp, D_MODEL), kp3, vp3, kp2, vp2, p, n_batch=bp, t_len=tp,
                   nseg=1, seg=tile_b, prompt=True)

    x1s, ks, vs, conv_s, h_s = _layer_a(
        x_sample, jnp.transpose(state_conv[0], (1, 0, 2)), state_rglru[0], p, "layer_a_sample")
    y_s = _layer_b(x1s.reshape(bs * ts, D_MODEL), cache_k_win.reshape(bs, WINDOW, KV_W),
                   cache_v_win.reshape(bs, WINDOW, KV_W), ks.reshape(bs * ts, KV_W),
                   vs.reshape(bs * ts, KV_W), p, n_batch=bs, t_len=ts, nseg=16, seg=ts,
                   prompt=False)

    kp4 = kp.reshape(bp, tp, N_KV, HEAD_DIM)[:, -WINDOW:]
    vp4 = vp.reshape(bp, tp, N_KV, HEAD_DIM)[:, -WINDOW:]
    ks4 = jnp.concatenate([cache_k_win, ks.reshape(bs, ts, N_KV, HEAD_DIM)], axis=1)[:, -WINDOW:]
    vs4 = jnp.concatenate([cache_v_win, vs.reshape(bs, ts, N_KV, HEAD_DIM)], axis=1)[:, -WINDOW:]
    return (y_p.reshape(bp, tp, D_MODEL), y_s.reshape(bs, ts, D_MODEL),
            jnp.transpose(conv_p, (1, 0, 2))[None], h_p[None], kp4, vp4,
            jnp.transpose(conv_s, (1, 0, 2))[None], h_s[None], ks4, vs4)
```

```python
import functools

import jax
import jax.numpy as jnp
from jax import lax
from jax.experimental import pallas as pl
from jax.experimental.pallas import tpu as pltpu

D_MODEL = 1024
D_RNN = 1024
GATE_BLOCK = 256
N_GATE_BLOCKS = D_RNN // GATE_BLOCK
CONV_W = 4
LRU_C = 8.0
HEAD_DIM = 64
N_HEADS = 16
N_KV = 2
GROUP = N_HEADS // N_KV
CHUNK = 64
WINDOW = 128
EPS = 1e-6
NEG = -1e30
LOG2_E = 1.4426950408889634
LN_2 = 0.6931471805599453

SUBLANES = 8
LANES = 128
BF16_SUBLANES = 16
MXU_N = 256
VMEM_LIMIT_BYTES = 56 * 1024 * 1024

F32 = jnp.float32
BF16 = jnp.bfloat16

KV_W = N_KV * HEAD_DIM
PAIR_W = 2 * HEAD_DIM
N_PAIRS = GROUP // 2
LOOKAHEAD = 2
B_PART_ROWS = 256

A_MAX_STEPS = 128
A_TILE_ROWS = 1024
OUT_K_SPLIT = D_RNN // 2


def _silu_of_half(h):
    return h * jnp.tanh(h) + h


def _norm_rows_to(x_ref, g_ref, dst_ref, n_rows, blk, r0=0):
    g = g_ref[...]
    for r in range(r0, r0 + n_rows, blk):
        x = x_ref[r:r + blk, :]
        rs = lax.rsqrt(jnp.mean(x * x, axis=-1, keepdims=True) + EPS)
        dst_ref[r:r + blk, :] = (x_ref[r:r + blk, :] * rs * g).astype(BF16)


def _layer_a_kernel(x_hbm, conv_in_ref, h_in_ref,
                    norm_a_ref, w_in_ref, conv_w_ref, conv_b_ref, w_gate_ref, b_gx_ref, b_ga_ref,
                    lam_ref, w_out_ref, norm_kv_ref, w_kv_ref, k_norm_ref,
                    x1_hbm, k_hbm, v_hbm, conv_out_ref, h_out_ref,
                    xin, x1o, ko, vo, sem_in, sem_out,
                    xn_scr, xbp_scr, gate_scr, xc_scr, xcb_scr, gx_scr, ga_scr, hy_scr, h_scr,
                    *, a_steps, nseq):
    a_rows = a_steps * nseq
    conv_pre = (CONV_W - 1) * nseq
    g = pl.program_id(0)
    t = pl.program_id(1)
    nt = pl.num_programs(1)
    total = pl.num_programs(0) * nt
    i = g * nt + t
    slot = lax.rem(i, 2)

    def in_copies(step, sl):
        gg = step // nt
        tt = lax.rem(step, nt)
        return [pltpu.make_async_copy(x_hbm.at[gg * nseq + j, pl.ds(tt * a_steps, a_steps), :],
                                      xin.at[sl, :, j, :], sem_in.at[sl]) for j in range(nseq)]

    def out_copies(step, sl):
        gg = step // nt
        tt = lax.rem(step, nt)
        cps = []
        for buf, dst in ((x1o, x1_hbm), (ko, k_hbm), (vo, v_hbm)):
            cps += [pltpu.make_async_copy(buf.at[sl, :, j, :],
                                          dst.at[gg * nseq + j, pl.ds(tt * a_steps, a_steps), :],
                                          sem_out.at[sl]) for j in range(nseq)]
        return cps

    @pl.when(i == 0)
    def _():
        for c in in_copies(i, slot):
            c.start()

    @pl.when(i + 1 < total)
    def _():
        for c in in_copies(i + 1, 1 - slot):
            c.start()

    @pl.when(t == 0)
    def _():
        xbp_scr[0:conv_pre, :] = conv_in_ref[...].reshape(conv_pre, D_RNN)
        h_scr[...] = h_in_ref[...]

    @pl.when(i >= 2)
    def _():
        for c in out_copies(i - 2, slot):
            c.wait()

    for c in in_copies(i, slot):
        c.wait()

    x_ref = xin.at[slot].reshape(a_rows, D_MODEL)
    x1_ref = x1o.at[slot].reshape(a_rows, D_MODEL)
    k_ref = ko.at[slot].reshape(a_rows, KV_W)
    v_ref = vo.at[slot].reshape(a_rows, KV_W)

    _norm_rows_to(x_ref, norm_a_ref, xn_scr, a_rows, 128)
    lam = lam_ref[...]
    log_sig_lam = jnp.minimum(lam, 0.0) - jnp.log1p(jnp.exp(-jnp.abs(lam)))
    kk = jnp.broadcast_to((0.5 * LRU_C * LOG2_E) * log_sig_lam, (nseq, D_RNN))
    b_gx = jnp.broadcast_to(b_gx_ref[...], (nseq, D_RNN))
    b_ga = jnp.broadcast_to(b_ga_ref[...], (nseq, D_RNN))

    cw = conv_w_ref[...]
    cb = conv_b_ref[...]
    conv_blk = 32

    def in_proj_x(n):
        cols = slice(n * GATE_BLOCK, (n + 1) * GATE_BLOCK)
        xbp_scr[conv_pre:, cols] = jnp.dot(xn_scr[...], w_in_ref[:, cols],
                                           preferred_element_type=F32)

    def in_proj_gate(n):
        cols = slice(n * GATE_BLOCK, (n + 1) * GATE_BLOCK)
        gate_scr[:, cols] = jnp.dot(
            xn_scr[...], w_in_ref[:, D_RNN + n * GATE_BLOCK:D_RNN + (n + 1) * GATE_BLOCK],
            preferred_element_type=F32)

    def conv_and_gates(n):
        cols = slice(n * GATE_BLOCK, (n + 1) * GATE_BLOCK)
        for r in range(0, a_rows, conv_blk):
            acc = cb[:, cols]
            for j in range(CONV_W):
                acc = acc + cw[j:j + 1, cols] * xbp_scr[r + j * nseq:r + j * nseq + conv_blk, cols]
            xc_scr[r:r + conv_blk, cols] = acc
            xcb_scr[r:r + conv_blk, cols] = acc.astype(BF16)
        tail = xbp_scr[a_rows:a_rows + conv_pre, cols]
        conv_out_ref[:, :, cols] = tail.reshape(CONV_W - 1, nseq, GATE_BLOCK)
        xbp_scr[0:conv_pre, cols] = tail

        res = jnp.dot(xcb_scr[:, cols], w_gate_ref[n], preferred_element_type=F32)
        gx_scr[:, cols] = res[:, :GATE_BLOCK]
        ga_scr[:, cols] = res[:, GATE_BLOCK:]

    def scan(n):
        cols = slice(n * GATE_BLOCK, (n + 1) * GATE_BLOCK)
        kk_n = kk[:, cols]
        bgx_n = b_gx[:, cols]
        bga_n = b_ga[:, cols]
        h = h_scr[:, cols]
        rows_per_store = max(BF16_SUBLANES, nseq)
        for r in range(0, a_rows, rows_per_store):
            hy = []
            for rr in range(r, r + rows_per_store, nseq):
                t_i = jnp.tanh(gx_scr[rr:rr + nseq, cols] + bgx_n)
                t_r = jnp.tanh(ga_scr[rr:rr + nseq, cols] + bga_n)
                log2_a = t_r * kk_n + kk_n
                a = jnp.exp2(log2_a)
                y = jnp.tanh(log2_a * (-LN_2)) * (a * a + 1.0)
                mult = jnp.where(y > 0.0, y * lax.rsqrt(y), 0.0)
                u = mult * xc_scr[rr:rr + nseq, cols]
                h = a * h + (u * t_i + u)
                hy.append(h * _silu_of_half(gate_scr[rr:rr + nseq, cols]))
            hy_scr[r:r + rows_per_store, cols] = jnp.concatenate(hy, axis=0).astype(BF16)
        h_scr[:, cols] = h
        h_out_ref[:, cols] = h

    def out_proj_first_part():
        for nn in range(D_MODEL // MXU_N):
            cols = slice(nn * MXU_N, (nn + 1) * MXU_N)
            x1_ref[:, cols] = x_ref[:, cols] + jnp.dot(
                hy_scr[:, :OUT_K_SPLIT], w_out_ref[:OUT_K_SPLIT, cols],
                preferred_element_type=F32)

    in_proj_x(0)
    in_proj_gate(0)
    in_proj_x(1)
    conv_and_gates(0)
    in_proj_gate(1)
    for n in range(N_GATE_BLOCKS):
        if n + 2 < N_GATE_BLOCKS:
            in_proj_x(n + 2)
        if n + 1 < N_GATE_BLOCKS:
            conv_and_gates(n + 1)
        if n + 2 < N_GATE_BLOCKS:
            in_proj_gate(n + 2)
        if (n + 1) * GATE_BLOCK == D_RNN - GATE_BLOCK:
            out_proj_first_part()
        scan(n)

    half_rows = a_rows // 2
    lo = lax.broadcasted_iota(jnp.int32, (half_rows, KV_W), 1) < HEAD_DIM
    for rh in range(2):
        rows = slice(rh * half_rows, (rh + 1) * half_rows)
        for n in range(D_MODEL // MXU_N):
            cols = slice(n * MXU_N, (n + 1) * MXU_N)
            x1_ref[rows, cols] = x1_ref[rows, cols] + jnp.dot(
                hy_scr[rows, OUT_K_SPLIT:], w_out_ref[OUT_K_SPLIT:, cols],
                preferred_element_type=F32)
        _norm_rows_to(x1_ref, norm_kv_ref, xn_scr, half_rows, 64, r0=rh * half_rows)
        kv = jnp.dot(xn_scr[rows, :], w_kv_ref[...], preferred_element_type=F32)
        k = kv[:, :KV_W]
        k2 = k * k
        ms_lo = jnp.sum(jnp.where(lo, k2, 0.0), axis=-1, keepdims=True) * (1.0 / HEAD_DIM)
        ms_hi = jnp.sum(jnp.where(lo, 0.0, k2), axis=-1, keepdims=True) * (1.0 / HEAD_DIM)
        rs = jnp.where(lo, lax.rsqrt(ms_lo + EPS), lax.rsqrt(ms_hi + EPS))
        k_ref[rows, :] = k * rs * k_norm_ref[...]
        v_ref[rows, :] = kv[:, KV_W:]

    for c in out_copies(i, slot):
        c.start()

    @pl.when(i == total - 1)
    def _():
        @pl.when(i >= 1)
        def _():
            for c in out_copies(i - 1, 1 - slot):
                c.wait()
        for c in out_copies(i, slot):
            c.wait()


def _layer_a(x, conv_in, h_in, p, name):
    n_seq, t_len, _ = x.shape
    a_steps = min(t_len, A_MAX_STEPS)
    nseq = min(n_seq, max(SUBLANES, A_TILE_ROWS // a_steps), BF16_SUBLANES)
    a_rows = a_steps * nseq
    conv_pre = (CONV_W - 1) * nseq
    assert nseq % SUBLANES == 0 and n_seq % nseq == 0 and t_len % a_steps == 0
    grid = (n_seq // nseq, t_len // a_steps)
    const2 = lambda g, t: (0, 0)
    const3 = lambda g, t: (0, 0, 0)
    any_spec = pl.BlockSpec(memory_space=pl.ANY)

    weights = [p["norm_a"], p["w_in_a"], p["conv_w"], p["conv_b"], p["w_gate"], p["b_gx"],
               p["b_ga"], p["lam"], p["w_out_a"], p["norm_kv"], p["w_kv"], p["k_norm2"]]
    w_specs = [pl.BlockSpec(w.shape, const3 if w.ndim == 3 else const2,
                            pipeline_mode=pl.Buffered(1)) for w in weights]
    conv_spec = pl.BlockSpec((CONV_W - 1, nseq, D_RNN), lambda g, t: (0, g, 0))
    h_spec = pl.BlockSpec((nseq, D_RNN), lambda g, t: (g, 0))

    out_shape = [
        jax.ShapeDtypeStruct((n_seq, t_len, D_MODEL), F32),
        jax.ShapeDtypeStruct((n_seq, t_len, KV_W), F32),
        jax.ShapeDtypeStruct((n_seq, t_len, KV_W), F32),
        jax.ShapeDtypeStruct((CONV_W - 1, n_seq, D_RNN), F32),
        jax.ShapeDtypeStruct((n_seq, D_RNN), F32),
    ]
    scratch = [
        pltpu.VMEM((2, a_steps, nseq, D_MODEL), F32),
        pltpu.VMEM((2, a_steps, nseq, D_MODEL), F32),
        pltpu.VMEM((2, a_steps, nseq, KV_W), F32),
        pltpu.VMEM((2, a_steps, nseq, KV_W), F32),
        pltpu.SemaphoreType.DMA((2,)),
        pltpu.SemaphoreType.DMA((2,)),
        pltpu.VMEM((a_rows, D_MODEL), BF16),
        pltpu.VMEM((conv_pre + a_rows, D_RNN), F32),
        pltpu.VMEM((a_rows, D_RNN), F32),
        pltpu.VMEM((a_rows, D_RNN), F32),
        pltpu.VMEM((a_rows, D_RNN), BF16),
        pltpu.VMEM((a_rows, D_RNN), F32),
        pltpu.VMEM((a_rows, D_RNN), F32),
        pltpu.VMEM((a_rows, D_RNN), BF16),
        pltpu.VMEM((nseq, D_RNN), F32),
    ]
    return pl.pallas_call(
        functools.partial(_layer_a_kernel, a_steps=a_steps, nseq=nseq),
        grid=grid,
        in_specs=[any_spec, conv_spec, h_spec] + w_specs,
        out_specs=[any_spec, any_spec, any_spec, conv_spec, h_spec],
        out_shape=out_shape,
        scratch_shapes=scratch,
        compiler_params=pltpu.CompilerParams(
            dimension_semantics=("arbitrary", "arbitrary"), vmem_limit_bytes=VMEM_LIMIT_BYTES),
        name=name,
    )(x, conv_in, h_in, *weights)


def _layer_b_kernel(x1_ref, kh_ref, kc_ref, vh_ref, vc_ref, norm_b_ref, w_in_ref,
                    qg_ref, sinks_ref, w_out_ref, y_ref,
                    xn_scr, gate_scr, qh_scr, kmat_scr, vmat_scr, o_scr, og_scr,
                    *, nseg, seg, mask_first_halo):
    m = nseg * seg
    n_chunks = seg // CHUNK
    part = min(m, B_PART_ROWS)
    assert m % part == 0 and (part % seg == 0 or seg % part == 0)
    n_parts = m // part
    n_col_blocks = D_MODEL // MXU_N

    _norm_rows_to(x1_ref, norm_b_ref, xn_scr, m, 128)

    lo_m = lax.broadcasted_iota(jnp.int32, (part, LANES), 1) < HEAD_DIM
    qg = qg_ref[...]

    def q_block(h, n):
        rows = slice(h * part, (h + 1) * part)
        qn = jnp.dot(xn_scr[rows, :], w_in_ref[:, n * MXU_N:(n + 1) * MXU_N],
                     preferred_element_type=F32)
        for cc in range(MXU_N // LANES):
            col = n * MXU_N + cc * LANES
            x = qn[:, cc * LANES:(cc + 1) * LANES]
            sq = x * x
            ms_lo = jnp.sum(jnp.where(lo_m, sq, 0.0), axis=-1, keepdims=True) * (1.0 / HEAD_DIM)
            ms_hi = jnp.sum(jnp.where(lo_m, 0.0, sq), axis=-1, keepdims=True) * (1.0 / HEAD_DIM)
            rs = jnp.where(lo_m, lax.rsqrt(ms_lo + EPS), lax.rsqrt(ms_hi + EPS))
            qh_scr[rows, col:col + LANES] = (x * rs * qg[:, col:col + LANES]).astype(BF16)

    row_split = 2 if part >= 2 * MXU_N else 1

    def gate_piece(h, n, rh):
        r0 = h * part + rh * (part // row_split)
        rows = slice(r0, r0 + part // row_split)
        gate_scr[rows, n * MXU_N:(n + 1) * MXU_N] = jnp.dot(
            xn_scr[rows, :], w_in_ref[:, D_MODEL + n * MXU_N:D_MODEL + (n + 1) * MXU_N],
            preferred_element_type=F32)

    def gate_multiply(h):
        blk = 32
        for r in range(h * part, (h + 1) * part, blk):
            og_scr[r:r + blk, :] = (
                o_scr[r:r + blk, :] * _silu_of_half(gate_scr[r:r + blk, :])).astype(BF16)

    def out_piece(h, n):
        rows = slice(h * part, (h + 1) * part)
        cols = slice(n * MXU_N, (n + 1) * MXU_N)
        y_ref[rows, cols] = x1_ref[rows, cols] + jnp.dot(
            og_scr[rows, :], w_out_ref[:, cols], preferred_element_type=F32)

    lane = lax.broadcasted_iota(jnp.int32, (CHUNK, KV_W), 1)
    lo = lane < HEAD_DIM
    first = pl.program_id(1) == 0 if mask_first_halo else None

    for s in range(nseg):
        for r0, nrows, kval, vval in ((0, WINDOW, kh_ref[s], vh_ref[s]),
                                      (WINDOW, seg, kc_ref[s * seg:(s + 1) * seg, :],
                                       vc_ref[s * seg:(s + 1) * seg, :])):
            is_lo = lax.broadcasted_iota(jnp.int32, kval.shape, 1) < HEAD_DIM
            ind_e = jnp.where(is_lo, 1.0, 0.0).astype(BF16)
            ind_o = jnp.where(is_lo, 0.0, 1.0).astype(BF16)
            rows = slice(r0, r0 + nrows)
            for val, scr in ((kval, kmat_scr), (vval, vmat_scr)):
                swapped = pltpu.roll(val, HEAD_DIM, 1)
                for g in range(N_KV):
                    low = val if g == 0 else swapped
                    high = swapped if g == 0 else val
                    scr[s, g, 0, rows, 0:KV_W] = jnp.where(is_lo, low, 0.0).astype(BF16)
                    scr[s, g, 1, rows, 0:KV_W] = jnp.where(is_lo, 0.0, high).astype(BF16)
            for g in range(N_KV):
                vmat_scr[s, g, 0, rows, KV_W:] = ind_e
                vmat_scr[s, g, 1, rows, KV_W:] = ind_o

    def placed(scr, s, g, k0):
        return jnp.concatenate(
            [scr[s, g, 0, k0:k0 + WINDOW, :], scr[s, g, 1, k0:k0 + WINDOW, :],
             scr[s, g, 0, k0 + WINDOW:k0 + WINDOW + CHUNK, :],
             scr[s, g, 1, k0 + WINDOW:k0 + WINDOW + CHUNK, :]], axis=0)

    def scores(s, c, g):
        q0 = s * seg + c * CHUNK
        qst = jnp.concatenate(
            [qh_scr[q0:q0 + CHUNK, g * GROUP * HEAD_DIM + j * PAIR_W:
                    g * GROUP * HEAD_DIM + (j + 1) * PAIR_W] for j in range(N_PAIRS)],
            axis=0)
        return lax.dot_general(qst, placed(kmat_scr, s, g, c * CHUNK),
                               (((1,), (1,)), ((), ())),
                               preferred_element_type=F32)

    def softmax_pv(sc, s, c, g):
        p_rows, sink_rows = [], []
        for j in range(N_PAIRS):
            sj = sc[j * CHUNK:(j + 1) * CHUNK]
            c0 = sj[:, 0:KV_W]
            c1 = sj[:, KV_W:2 * KV_W]
            c2 = sj[:, 2 * KV_W:]
            if mask_first_halo and c * CHUNK < WINDOW:
                n_bad = jnp.where(first, WINDOW - c * CHUNK, 0)
                bad = lane < n_bad
                c0 = jnp.where(bad, NEG, c0)
                c1 = jnp.where(bad, NEG, c1)
            sink_e = sinks_ref[g * GROUP + 2 * j] * LOG2_E
            sink_o = sinks_ref[g * GROUP + 2 * j + 1] * LOG2_E
            m_e = jnp.maximum(jnp.max(jnp.maximum(c0, jnp.where(lo, c2, NEG)),
                                      axis=-1, keepdims=True), sink_e)
            m_o = jnp.maximum(jnp.max(jnp.maximum(c1, jnp.where(lo, NEG, c2)),
                                      axis=-1, keepdims=True), sink_o)
            p_rows.append(jnp.concatenate(
                [jnp.exp2(c0 - m_e), jnp.exp2(c1 - m_o),
                 jnp.exp2(c2 - jnp.where(lo, m_e, m_o))], axis=1).astype(BF16))
            sink_rows.append(jnp.where(lo, jnp.exp2(sink_e - m_e), jnp.exp2(sink_o - m_o)))
        pmat = jnp.concatenate(p_rows, axis=0)
        ov = jnp.dot(pmat, placed(vmat_scr, s, g, c * CHUNK),
                     preferred_element_type=F32)
        q0 = s * seg + c * CHUNK
        for j in range(N_PAIRS):
            rows = slice(j * CHUNK, (j + 1) * CHUNK)
            c_lo = g * GROUP * HEAD_DIM + j * PAIR_W
            o_scr[q0:q0 + CHUNK, c_lo:c_lo + PAIR_W] = (
                ov[rows, :KV_W] / (ov[rows, KV_W:] + sink_rows[j]))

    all_units = [(s, c, g) for s in range(nseg) for c in range(n_chunks) for g in range(N_KV)]

    for n in range(n_col_blocks):
        q_block(0, n)
    for h in range(n_parts):
        units = [u for u in all_units if (u[0] * seg + u[1] * CHUNK) // part == h]
        fillers = []
        if h >= 1:
            fillers += [functools.partial(out_piece, h - 1, n) for n in range(n_col_blocks)]
        gates = [functools.partial(gate_piece, h, n, rh)
                 for n in range(n_col_blocks) for rh in range(row_split)]
        nxt = ([functools.partial(q_block, h + 1, n) for n in range(n_col_blocks)]
               if h + 1 < n_parts else [])
        while gates or nxt:
            fillers += gates[:2]
            gates = gates[2:]
            fillers += nxt[:1]
            nxt = nxt[1:]
        fillers.pop(0)()
        per_unit = -(-len(fillers) // len(units))
        pending = [scores(*u) for u in units[:LOOKAHEAD]]
        for k, (s, c, g) in enumerate(units):
            sc = pending.pop(0)
            if k + LOOKAHEAD < len(units):
                pending.append(scores(*units[k + LOOKAHEAD]))
            for _ in range(min(per_unit, len(fillers))):
                fillers.pop(0)()
            softmax_pv(sc, s, c, g)
        assert not fillers
        gate_multiply(h)
    for n in range(n_col_blocks):
        out_piece(n_parts - 1, n)


def _layer_b(x1, k_halo, v_halo, k_cur, v_cur, p, *, n_batch, t_len, nseg, seg, prompt):
    m = nseg * seg
    rows = n_batch * t_len
    const2 = lambda *_: (0, 0)
    if prompt:
        steps = t_len // m
        per_b = t_len // WINDOW
        grid = (n_batch, steps)
        row_map = lambda b, t: (b * steps + t, 0)
        halo_map = lambda b, t: (b * per_b + jnp.maximum(t * (m // WINDOW) - 1, 0), 0, 0)
    else:
        grid = (n_batch // nseg, 1)
        row_map = lambda i, t: (i, 0)
        halo_map = lambda i, t: (i, 0, 0)

    weights = [p["norm_b"], p["w_in_b"], p["qg"]]
    in_specs = [
        pl.BlockSpec((m, D_MODEL), row_map),
        pl.BlockSpec((nseg, WINDOW, KV_W), halo_map),
        pl.BlockSpec((m, KV_W), row_map),
        pl.BlockSpec((nseg, WINDOW, KV_W), halo_map),
        pl.BlockSpec((m, KV_W), row_map),
    ] + [pl.BlockSpec(w.shape, const2) for w in weights] + [
        pl.BlockSpec(memory_space=pltpu.SMEM),
        pl.BlockSpec(p["w_out_b"].shape, const2),
    ]
    kv_rows = WINDOW + seg
    scratch = [
        pltpu.VMEM((m, D_MODEL), BF16),
        pltpu.VMEM((m, D_MODEL), F32),
        pltpu.VMEM((m, D_MODEL), BF16),
        pltpu.VMEM((nseg, N_KV, 2, kv_rows, KV_W), BF16),
        pltpu.VMEM((nseg, N_KV, 2, kv_rows, 2 * KV_W), BF16),
        pltpu.VMEM((m, D_MODEL), F32),
        pltpu.VMEM((m, D_MODEL), BF16),
    ]
    return pl.pallas_call(
        functools.partial(_layer_b_kernel, nseg=nseg, seg=seg, mask_first_halo=prompt),
        grid=grid,
        in_specs=in_specs,
        out_specs=pl.BlockSpec((m, D_MODEL), row_map),
        out_shape=jax.ShapeDtypeStruct((rows, D_MODEL), F32),
        scratch_shapes=scratch,
        compiler_params=pltpu.CompilerParams(
            dimension_semantics=("arbitrary", "arbitrary"), vmem_limit_bytes=VMEM_LIMIT_BYTES),
        name="layer_b_prompt" if prompt else "layer_b_sample",
    )(x1, k_halo, k_cur, v_halo, v_cur, *weights, p["sinks"], p["w_out_b"])


def _prep_params(norm_a, w_in_a, conv_w, conv_b, w_gate_x, b_gate_x, w_gate_a, b_gate_a,
                 lru_lambda, w_out_a, norm_kv, w_kv, k_norm, norm_b, w_in_b, q_norm, sinks,
                 w_out_b):
    row = lambda v: v.reshape(1, -1).astype(F32)
    in_scale = jnp.concatenate([jnp.ones((D_MODEL,), F32), jnp.full((D_MODEL,), 0.5, F32)])
    return {
        "norm_a": row(norm_a[0]),
        "w_in_a": (w_in_a[0] * in_scale).astype(BF16),
        "conv_w": (0.5 * conv_w[0]).astype(F32),
        "conv_b": row(0.5 * conv_b[0]),
        "w_gate": jnp.concatenate([w_gate_x[0], w_gate_a[0]], axis=-1).astype(BF16),
        "b_gx": row(0.5 * b_gate_x[0]),
        "b_ga": row(0.5 * b_gate_a[0]),
        "lam": row(lru_lambda[0]),
        "w_out_a": w_out_a[0].astype(BF16),
        "norm_kv": row(norm_kv),
        "w_kv": w_kv.astype(BF16),
        "k_norm2": row(jnp.tile(k_norm, N_KV)),
        "norm_b": row(norm_b[0]),
        "w_in_b": (w_in_b[0] * in_scale).astype(BF16),
        "qg": row(jnp.tile(q_norm[0], N_HEADS) * (HEAD_DIM ** -0.5 * LOG2_E)),
        "sinks": sinks[0].astype(F32),
        "w_out_b": w_out_b[0].astype(BF16),
    }


def kernel(x_prompt, x_sample, state_conv, state_rglru, cache_k_win, cache_v_win, norm_a, w_in_a, conv_w, conv_b, w_gate_x, b_gate_x, w_gate_a, b_gate_a, lru_lambda, w_out_a, norm_kv, w_kv, k_norm, norm_b, w_in_b, q_norm, sinks, w_out_b):
    assert norm_a.shape[0] == 1 and norm_b.shape[0] == 1, "one recurrent + one attention layer"
    p = _prep_params(norm_a, w_in_a, conv_w, conv_b, w_gate_x, b_gate_x, w_gate_a, b_gate_a,
                     lru_lambda, w_out_a, norm_kv, w_kv, k_norm, norm_b, w_in_b, q_norm, sinks,
                     w_out_b)
    bp, tp, _ = x_prompt.shape
    bs, ts, _ = x_sample.shape

    x1p, kp, vp, conv_p, h_p = _layer_a(
        x_prompt, jnp.zeros((CONV_W - 1, bp, D_RNN), F32), jnp.zeros((bp, D_RNN), F32), p,
        "layer_a_prompt")
    tile_b = 1024
    kp2 = kp.reshape(bp * tp, KV_W)
    vp2 = vp.reshape(bp * tp, KV_W)
    kp3 = kp.reshape(bp * tp // WINDOW, WINDOW, KV_W)
    vp3 = vp.reshape(bp * tp // WINDOW, WINDOW, KV_W)
    y_p = _layer_b(x1p.reshape(bp * tp, D_MODEL), kp3, vp3, kp2, vp2, p, n_batch=bp, t_len=tp,
                   nseg=1, seg=tile_b, prompt=True)

    x1s, ks, vs, conv_s, h_s = _layer_a(
        x_sample, jnp.transpose(state_conv[0], (1, 0, 2)), state_rglru[0], p, "layer_a_sample")
    y_s = _layer_b(x1s.reshape(bs * ts, D_MODEL), cache_k_win.reshape(bs, WINDOW, KV_W),
                   cache_v_win.reshape(bs, WINDOW, KV_W), ks.reshape(bs * ts, KV_W),
                   vs.reshape(bs * ts, KV_W), p, n_batch=bs, t_len=ts, nseg=16, seg=ts,
                   prompt=False)

    kp4 = kp.reshape(bp, tp, N_KV, HEAD_DIM)[:, -WINDOW:]
    vp4 = vp.reshape(bp, tp, N_KV, HEAD_DIM)[:, -WINDOW:]
    ks4 = jnp.concatenate([cache_k_win, ks.reshape(bs, ts, N_KV, HEAD_DIM)], axis=1)[:, -WINDOW:]
    vs4 = jnp.concatenate([cache_v_win, vs.reshape(bs, ts, N_KV, HEAD_DIM)], axis=1)[:, -WINDOW:]
    return (y_p.reshape(bp, tp, D_MODEL), y_s.reshape(bs, ts, D_MODEL),
            jnp.transpose(conv_p, (1, 0, 2))[None], h_p[None], kp4, vp4,
            jnp.transpose(conv_s, (1, 0, 2))[None], h_s[None], ks4, vs4)
```

```python
import functools

import jax
import jax.numpy as jnp
from jax import lax
from jax.experimental import pallas as pl
from jax.experimental.pallas import tpu as pltpu

D_MODEL = 1024
D_RNN = 1024
GATE_BLOCK = 256
N_GATE_BLOCKS = D_RNN // GATE_BLOCK
CONV_W = 4
LRU_C = 8.0
HEAD_DIM = 64
N_HEADS = 16
N_KV = 2
GROUP = N_HEADS // N_KV
CHUNK = 64
WINDOW = 128
EPS = 1e-6
NEG = -1e30
LOG2_E = 1.4426950408889634
LN_2 = 0.6931471805599453

SUBLANES = 8
LANES = 128
BF16_SUBLANES = 16
MXU_N = 256
VMEM_LIMIT_BYTES = 56 * 1024 * 1024

F32 = jnp.float32
BF16 = jnp.bfloat16

KV_W = N_KV * HEAD_DIM
PAIR_W = 2 * HEAD_DIM
N_PAIRS = GROUP // 2
LOOKAHEAD = 3
B_PART_ROWS = 256

A_MAX_STEPS = 128
A_TILE_ROWS = 1024
OUT_K_SPLIT = D_RNN // 2


def _silu_of_half(h):
    return h * jnp.tanh(h) + h


def _norm_rows_to(x_ref, g_ref, dst_ref, n_rows, blk, r0=0):
    g = g_ref[...]
    for r in range(r0, r0 + n_rows, blk):
        x = x_ref[r:r + blk, :]
        rs = lax.rsqrt(jnp.mean(x * x, axis=-1, keepdims=True) + EPS)
        dst_ref[r:r + blk, :] = (x_ref[r:r + blk, :] * rs * g).astype(BF16)


def _layer_a_kernel(x_hbm, conv_in_ref, h_in_ref,
                    norm_a_ref, w_in_ref, conv_w_ref, conv_b_ref, w_gate_ref, b_gx_ref, b_ga_ref,
                    lam_ref, w_out_ref, norm_kv_ref, w_kv_ref, k_norm_ref,
                    x1_hbm, k_hbm, v_hbm, conv_out_ref, h_out_ref,
                    xin, x1o, ko, vo, sem_in, sem_out,
                    xn_scr, xbp_scr, gate_scr, xc_scr, xcb_scr, gx_scr, ga_scr, hy_scr, h_scr,
                    *, a_steps, nseq):
    a_rows = a_steps * nseq
    conv_pre = (CONV_W - 1) * nseq
    g = pl.program_id(0)
    t = pl.program_id(1)
    nt = pl.num_programs(1)
    total = pl.num_programs(0) * nt
    i = g * nt + t
    slot = lax.rem(i, 2)

    def in_copies(step, sl):
        gg = step // nt
        tt = lax.rem(step, nt)
        return [pltpu.make_async_copy(x_hbm.at[gg * nseq + j, pl.ds(tt * a_steps, a_steps), :],
                                      xin.at[sl, :, j, :], sem_in.at[sl]) for j in range(nseq)]

    def out_copies(step, sl):
        gg = step // nt
        tt = lax.rem(step, nt)
        cps = []
        for buf, dst in ((x1o, x1_hbm), (ko, k_hbm), (vo, v_hbm)):
            cps += [pltpu.make_async_copy(buf.at[sl, :, j, :],
                                          dst.at[gg * nseq + j, pl.ds(tt * a_steps, a_steps), :],
                                          sem_out.at[sl]) for j in range(nseq)]
        return cps

    @pl.when(i == 0)
    def _():
        for c in in_copies(i, slot):
            c.start()

    @pl.when(i + 1 < total)
    def _():
        for c in in_copies(i + 1, 1 - slot):
            c.start()

    @pl.when(t == 0)
    def _():
        xbp_scr[0:conv_pre, :] = conv_in_ref[...].reshape(conv_pre, D_RNN)
        h_scr[...] = h_in_ref[...]

    @pl.when(i >= 2)
    def _():
        for c in out_copies(i - 2, slot):
            c.wait()

    for c in in_copies(i, slot):
        c.wait()

    x_ref = xin.at[slot].reshape(a_rows, D_MODEL)
    x1_ref = x1o.at[slot].reshape(a_rows, D_MODEL)
    k_ref = ko.at[slot].reshape(a_rows, KV_W)
    v_ref = vo.at[slot].reshape(a_rows, KV_W)

    _norm_rows_to(x_ref, norm_a_ref, xn_scr, a_rows, 128)
    lam = lam_ref[...]
    log_sig_lam = jnp.minimum(lam, 0.0) - jnp.log1p(jnp.exp(-jnp.abs(lam)))
    kk = jnp.broadcast_to((0.5 * LRU_C * LOG2_E) * log_sig_lam, (nseq, D_RNN))
    b_gx = jnp.broadcast_to(b_gx_ref[...], (nseq, D_RNN))
    b_ga = jnp.broadcast_to(b_ga_ref[...], (nseq, D_RNN))

    cw = conv_w_ref[...]
    cb = conv_b_ref[...]
    conv_blk = 32

    def in_proj_x(n):
        cols = slice(n * GATE_BLOCK, (n + 1) * GATE_BLOCK)
        xbp_scr[conv_pre:, cols] = jnp.dot(xn_scr[...], w_in_ref[:, cols],
                                           preferred_element_type=F32)

    def in_proj_gate(n):
        cols = slice(n * GATE_BLOCK, (n + 1) * GATE_BLOCK)
        gate_scr[:, cols] = jnp.dot(
            xn_scr[...], w_in_ref[:, D_RNN + n * GATE_BLOCK:D_RNN + (n + 1) * GATE_BLOCK],
            preferred_element_type=F32)

    def conv_and_gates(n):
        cols = slice(n * GATE_BLOCK, (n + 1) * GATE_BLOCK)
        for r in range(0, a_rows, conv_blk):
            acc = cb[:, cols]
            for j in range(CONV_W):
                acc = acc + cw[j:j + 1, cols] * xbp_scr[r + j * nseq:r + j * nseq + conv_blk, cols]
            xc_scr[r:r + conv_blk, cols] = acc
            xcb_scr[r:r + conv_blk, cols] = acc.astype(BF16)
        tail = xbp_scr[a_rows:a_rows + conv_pre, cols]
        conv_out_ref[:, :, cols] = tail.reshape(CONV_W - 1, nseq, GATE_BLOCK)
        xbp_scr[0:conv_pre, cols] = tail

        res = jnp.dot(xcb_scr[:, cols], w_gate_ref[n], preferred_element_type=F32)
        gx_scr[:, cols] = res[:, :GATE_BLOCK]
        ga_scr[:, cols] = res[:, GATE_BLOCK:]

    def scan(n):
        cols = slice(n * GATE_BLOCK, (n + 1) * GATE_BLOCK)
        kk_n = kk[:, cols]
        bgx_n = b_gx[:, cols]
        bga_n = b_ga[:, cols]
        h = h_scr[:, cols]
        rows_per_store = max(BF16_SUBLANES, nseq)
        for r in range(0, a_rows, rows_per_store):
            hy = []
            for rr in range(r, r + rows_per_store, nseq):
                t_i = jnp.tanh(gx_scr[rr:rr + nseq, cols] + bgx_n)
                t_r = jnp.tanh(ga_scr[rr:rr + nseq, cols] + bga_n)
                log2_a = t_r * kk_n + kk_n
                a = jnp.exp2(log2_a)
                y = jnp.tanh(log2_a * (-LN_2)) * (a * a + 1.0)
                mult = jnp.where(y > 0.0, y * lax.rsqrt(y), 0.0)
                u = mult * xc_scr[rr:rr + nseq, cols]
                h = a * h + (u * t_i + u)
                hy.append(h * _silu_of_half(gate_scr[rr:rr + nseq, cols]))
            hy_scr[r:r + rows_per_store, cols] = jnp.concatenate(hy, axis=0).astype(BF16)
        h_scr[:, cols] = h
        h_out_ref[:, cols] = h

    def out_proj_first_part():
        for nn in range(D_MODEL // MXU_N):
            cols = slice(nn * MXU_N, (nn + 1) * MXU_N)
            x1_ref[:, cols] = x_ref[:, cols] + jnp.dot(
                hy_scr[:, :OUT_K_SPLIT], w_out_ref[:OUT_K_SPLIT, cols],
                preferred_element_type=F32)

    in_proj_x(0)
    in_proj_gate(0)
    in_proj_x(1)
    conv_and_gates(0)
    in_proj_gate(1)
    for n in range(N_GATE_BLOCKS):
        if n + 2 < N_GATE_BLOCKS:
            in_proj_x(n + 2)
        if n + 1 < N_GATE_BLOCKS:
            conv_and_gates(n + 1)
        if n + 2 < N_GATE_BLOCKS:
            in_proj_gate(n + 2)
        if (n + 1) * GATE_BLOCK == D_RNN - GATE_BLOCK:
            out_proj_first_part()
        scan(n)

    half_rows = a_rows // 2
    lo = lax.broadcasted_iota(jnp.int32, (half_rows, KV_W), 1) < HEAD_DIM
    for rh in range(2):
        rows = slice(rh * half_rows, (rh + 1) * half_rows)
        for n in range(D_MODEL // MXU_N):
            cols = slice(n * MXU_N, (n + 1) * MXU_N)
            x1_ref[rows, cols] = x1_ref[rows, cols] + jnp.dot(
                hy_scr[rows, OUT_K_SPLIT:], w_out_ref[OUT_K_SPLIT:, cols],
                preferred_element_type=F32)
        _norm_rows_to(x1_ref, norm_kv_ref, xn_scr, half_rows, 64, r0=rh * half_rows)
        kv = jnp.dot(xn_scr[rows, :], w_kv_ref[...], preferred_element_type=F32)
        k = kv[:, :KV_W]
        k2 = k * k
        ms_lo = jnp.sum(jnp.where(lo, k2, 0.0), axis=-1, keepdims=True) * (1.0 / HEAD_DIM)
        ms_hi = jnp.sum(jnp.where(lo, 0.0, k2), axis=-1, keepdims=True) * (1.0 / HEAD_DIM)
        rs = jnp.where(lo, lax.rsqrt(ms_lo + EPS), lax.rsqrt(ms_hi + EPS))
        k_ref[rows, :] = k * rs * k_norm_ref[...]
        v_ref[rows, :] = kv[:, KV_W:]

    for c in out_copies(i, slot):
        c.start()

    @pl.when(i == total - 1)
    def _():
        @pl.when(i >= 1)
        def _():
            for c in out_copies(i - 1, 1 - slot):
                c.wait()
        for c in out_copies(i, slot):
            c.wait()


def _layer_a(x, conv_in, h_in, p, name):
    n_seq, t_len, _ = x.shape
    a_steps = min(t_len, A_MAX_STEPS)
    nseq = min(n_seq, max(SUBLANES, A_TILE_ROWS // a_steps), BF16_SUBLANES)
    a_rows = a_steps * nseq
    conv_pre = (CONV_W - 1) * nseq
    assert nseq % SUBLANES == 0 and n_seq % nseq == 0 and t_len % a_steps == 0
    grid = (n_seq // nseq, t_len // a_steps)
    const2 = lambda g, t: (0, 0)
    const3 = lambda g, t: (0, 0, 0)
    any_spec = pl.BlockSpec(memory_space=pl.ANY)

    weights = [p["norm_a"], p["w_in_a"], p["conv_w"], p["conv_b"], p["w_gate"], p["b_gx"],
               p["b_ga"], p["lam"], p["w_out_a"], p["norm_kv"], p["w_kv"], p["k_norm2"]]
    w_specs = [pl.BlockSpec(w.shape, const3 if w.ndim == 3 else const2,
                            pipeline_mode=pl.Buffered(1)) for w in weights]
    conv_spec = pl.BlockSpec((CONV_W - 1, nseq, D_RNN), lambda g, t: (0, g, 0))
    h_spec = pl.BlockSpec((nseq, D_RNN), lambda g, t: (g, 0))

    out_shape = [
        jax.ShapeDtypeStruct((n_seq, t_len, D_MODEL), F32),
        jax.ShapeDtypeStruct((n_seq, t_len, KV_W), F32),
        jax.ShapeDtypeStruct((n_seq, t_len, KV_W), F32),
        jax.ShapeDtypeStruct((CONV_W - 1, n_seq, D_RNN), F32),
        jax.ShapeDtypeStruct((n_seq, D_RNN), F32),
    ]
    scratch = [
        pltpu.VMEM((2, a_steps, nseq, D_MODEL), F32),
        pltpu.VMEM((2, a_steps, nseq, D_MODEL), F32),
        pltpu.VMEM((2, a_steps, nseq, KV_W), F32),
        pltpu.VMEM((2, a_steps, nseq, KV_W), F32),
        pltpu.SemaphoreType.DMA((2,)),
        pltpu.SemaphoreType.DMA((2,)),
        pltpu.VMEM((a_rows, D_MODEL), BF16),
        pltpu.VMEM((conv_pre + a_rows, D_RNN), F32),
        pltpu.VMEM((a_rows, D_RNN), F32),
        pltpu.VMEM((a_rows, D_RNN), F32),
        pltpu.VMEM((a_rows, D_RNN), BF16),
        pltpu.VMEM((a_rows, D_RNN), F32),
        pltpu.VMEM((a_rows, D_RNN), F32),
        pltpu.VMEM((a_rows, D_RNN), BF16),
        pltpu.VMEM((nseq, D_RNN), F32),
    ]
    return pl.pallas_call(
        functools.partial(_layer_a_kernel, a_steps=a_steps, nseq=nseq),
        grid=grid,
        in_specs=[any_spec, conv_spec, h_spec] + w_specs,
        out_specs=[any_spec, any_spec, any_spec, conv_spec, h_spec],
        out_shape=out_shape,
        scratch_shapes=scratch,
        compiler_params=pltpu.CompilerParams(
            dimension_semantics=("arbitrary", "arbitrary"), vmem_limit_bytes=VMEM_LIMIT_BYTES),
        name=name,
    )(x, conv_in, h_in, *weights)


def _layer_b_kernel(x1_ref, kh_ref, kc_ref, vh_ref, vc_ref, norm_b_ref, w_in_ref,
                    qg_ref, sinks_ref, w_out_ref, y_ref,
                    xn_scr, gate_scr, qh_scr, kmat_scr, vmat_scr, o_scr, og_scr,
                    *, nseg, seg, mask_first_halo):
    m = nseg * seg
    n_chunks = seg // CHUNK
    part = min(m, B_PART_ROWS)
    assert m % part == 0 and (part % seg == 0 or seg % part == 0)
    n_parts = m // part
    n_col_blocks = D_MODEL // MXU_N

    _norm_rows_to(x1_ref, norm_b_ref, xn_scr, m, 128)

    lo_m = lax.broadcasted_iota(jnp.int32, (part, LANES), 1) < HEAD_DIM
    qg = qg_ref[...]

    def q_block(h, n):
        rows = slice(h * part, (h + 1) * part)
        qn = jnp.dot(xn_scr[rows, :], w_in_ref[:, n * MXU_N:(n + 1) * MXU_N],
                     preferred_element_type=F32)
        for cc in range(MXU_N // LANES):
            col = n * MXU_N + cc * LANES
            x = qn[:, cc * LANES:(cc + 1) * LANES]
            sq = x * x
            ms_lo = jnp.sum(jnp.where(lo_m, sq, 0.0), axis=-1, keepdims=True) * (1.0 / HEAD_DIM)
            ms_hi = jnp.sum(jnp.where(lo_m, 0.0, sq), axis=-1, keepdims=True) * (1.0 / HEAD_DIM)
            rs = jnp.where(lo_m, lax.rsqrt(ms_lo + EPS), lax.rsqrt(ms_hi + EPS))
            qh_scr[rows, col:col + LANES] = (x * rs * qg[:, col:col + LANES]).astype(BF16)

    row_split = 2 if part >= 2 * MXU_N else 1

    def gate_piece(h, n, rh):
        r0 = h * part + rh * (part // row_split)
        rows = slice(r0, r0 + part // row_split)
        gate_scr[rows, n * MXU_N:(n + 1) * MXU_N] = jnp.dot(
            xn_scr[rows, :], w_in_ref[:, D_MODEL + n * MXU_N:D_MODEL + (n + 1) * MXU_N],
            preferred_element_type=F32)

    def gate_multiply(h):
        blk = 32
        for r in range(h * part, (h + 1) * part, blk):
            og_scr[r:r + blk, :] = (
                o_scr[r:r + blk, :] * _silu_of_half(gate_scr[r:r + blk, :])).astype(BF16)

    def out_piece(h, n):
        rows = slice(h * part, (h + 1) * part)
        cols = slice(n * MXU_N, (n + 1) * MXU_N)
        y_ref[rows, cols] = x1_ref[rows, cols] + jnp.dot(
            og_scr[rows, :], w_out_ref[:, cols], preferred_element_type=F32)

    lane = lax.broadcasted_iota(jnp.int32, (CHUNK, KV_W), 1)
    lo = lane < HEAD_DIM
    first = pl.program_id(1) == 0 if mask_first_halo else None

    for s in range(nseg):
        for r0, nrows, kval, vval in ((0, WINDOW, kh_ref[s], vh_ref[s]),
                                      (WINDOW, seg, kc_ref[s * seg:(s + 1) * seg, :],
                                       vc_ref[s * seg:(s + 1) * seg, :])):
            is_lo = lax.broadcasted_iota(jnp.int32, kval.shape, 1) < HEAD_DIM
            ind_e = jnp.where(is_lo, 1.0, 0.0).astype(BF16)
            ind_o = jnp.where(is_lo, 0.0, 1.0).astype(BF16)
            rows = slice(r0, r0 + nrows)
            for val, scr in ((kval, kmat_scr), (vval, vmat_scr)):
                swapped = pltpu.roll(val, HEAD_DIM, 1)
                for g in range(N_KV):
                    low = val if g == 0 else swapped
                    high = swapped if g == 0 else val
                    scr[s, g, 0, rows, 0:KV_W] = jnp.where(is_lo, low, 0.0).astype(BF16)
                    scr[s, g, 1, rows, 0:KV_W] = jnp.where(is_lo, 0.0, high).astype(BF16)
            for g in range(N_KV):
                vmat_scr[s, g, 0, rows, KV_W:] = ind_e
                vmat_scr[s, g, 1, rows, KV_W:] = ind_o

    def placed(scr, s, g, k0):
        return jnp.concatenate(
            [scr[s, g, 0, k0:k0 + WINDOW, :], scr[s, g, 1, k0:k0 + WINDOW, :],
             scr[s, g, 0, k0 + WINDOW:k0 + WINDOW + CHUNK, :],
             scr[s, g, 1, k0 + WINDOW:k0 + WINDOW + CHUNK, :]], axis=0)

    def scores(s, c, g):
        q0 = s * seg + c * CHUNK
        qst = jnp.concatenate(
            [qh_scr[q0:q0 + CHUNK, g * GROUP * HEAD_DIM + j * PAIR_W:
                    g * GROUP * HEAD_DIM + (j + 1) * PAIR_W] for j in range(N_PAIRS)],
            axis=0)
        return lax.dot_general(qst, placed(kmat_scr, s, g, c * CHUNK),
                               (((1,), (1,)), ((), ())),
                               preferred_element_type=F32)

    def softmax_pv(sc, s, c, g):
        p_rows, sink_rows = [], []
        for j in range(N_PAIRS):
            sj = sc[j * CHUNK:(j + 1) * CHUNK]
            c0 = sj[:, 0:KV_W]
            c1 = sj[:, KV_W:2 * KV_W]
            c2 = sj[:, 2 * KV_W:]
            if mask_first_halo and c * CHUNK < WINDOW:
                n_bad = jnp.where(first, WINDOW - c * CHUNK, 0)
                bad = lane < n_bad
                c0 = jnp.where(bad, NEG, c0)
                c1 = jnp.where(bad, NEG, c1)
            sink_e = sinks_ref[g * GROUP + 2 * j] * LOG2_E
            sink_o = sinks_ref[g * GROUP + 2 * j + 1] * LOG2_E
            m_e = jnp.maximum(jnp.max(jnp.maximum(c0, jnp.where(lo, c2, NEG)),
                                      axis=-1, keepdims=True), sink_e)
            m_o = jnp.maximum(jnp.max(jnp.maximum(c1, jnp.where(lo, NEG, c2)),
                                      axis=-1, keepdims=True), sink_o)
            p_rows.append(jnp.concatenate(
                [jnp.exp2(c0 - m_e), jnp.exp2(c1 - m_o),
                 jnp.exp2(c2 - jnp.where(lo, m_e, m_o))], axis=1).astype(BF16))
            sink_rows.append(jnp.where(lo, jnp.exp2(sink_e - m_e), jnp.exp2(sink_o - m_o)))
        pmat = jnp.concatenate(p_rows, axis=0)
        ov = jnp.dot(pmat, placed(vmat_scr, s, g, c * CHUNK),
                     preferred_element_type=F32)
        q0 = s * seg + c * CHUNK
        for j in range(N_PAIRS):
            rows = slice(j * CHUNK, (j + 1) * CHUNK)
            c_lo = g * GROUP * HEAD_DIM + j * PAIR_W
            o_scr[q0:q0 + CHUNK, c_lo:c_lo + PAIR_W] = (
                ov[rows, :KV_W] / (ov[rows, KV_W:] + sink_rows[j]))

    all_units = [(s, c, g) for s in range(nseg) for c in range(n_chunks) for g in range(N_KV)]

    for n in range(n_col_blocks):
        q_block(0, n)
    for h in range(n_parts):
        units = [u for u in all_units if (u[0] * seg + u[1] * CHUNK) // part == h]
        fillers = []
        if h >= 1:
            fillers += [functools.partial(out_piece, h - 1, n) for n in range(n_col_blocks)]
        gates = [functools.partial(gate_piece, h, n, rh)
                 for n in range(n_col_blocks) for rh in range(row_split)]
        nxt = ([functools.partial(q_block, h + 1, n) for n in range(n_col_blocks)]
               if h + 1 < n_parts else [])
        while gates or nxt:
            fillers += gates[:2]
            gates = gates[2:]
            fillers += nxt[:1]
            nxt = nxt[1:]
        fillers.pop(0)()
        per_unit = -(-len(fillers) // len(units))
        pending = [scores(*u) for u in units[:LOOKAHEAD]]
        for k, (s, c, g) in enumerate(units):
            sc = pending.pop(0)
            if k + LOOKAHEAD < len(units):
                pending.append(scores(*units[k + LOOKAHEAD]))
            for _ in range(min(per_unit, len(fillers))):
                fillers.pop(0)()
            softmax_pv(sc, s, c, g)
        assert not fillers
        gate_multiply(h)
    for n in range(n_col_blocks):
        out_piece(n_parts - 1, n)


def _layer_b(x1, k_halo, v_halo, k_cur, v_cur, p, *, n_batch, t_len, nseg, seg, prompt):
    m = nseg * seg
    rows = n_batch * t_len
    const2 = lambda *_: (0, 0)
    if prompt:
        steps = t_len // m
        per_b = t_len // WINDOW
        grid = (n_batch, steps)
        row_map = lambda b, t: (b * steps + t, 0)
        halo_map = lambda b, t: (b * per_b + jnp.maximum(t * (m // WINDOW) - 1, 0), 0, 0)
    else:
        grid = (n_batch // nseg, 1)
        row_map = lambda i, t: (i, 0)
        halo_map = lambda i, t: (i, 0, 0)

    weights = [p["norm_b"], p["w_in_b"], p["qg"]]
    in_specs = [
        pl.BlockSpec((m, D_MODEL), row_map),
        pl.BlockSpec((nseg, WINDOW, KV_W), halo_map),
        pl.BlockSpec((m, KV_W), row_map),
        pl.BlockSpec((nseg, WINDOW, KV_W), halo_map),
        pl.BlockSpec((m, KV_W), row_map),
    ] + [pl.BlockSpec(w.shape, const2) for w in weights] + [
        pl.BlockSpec(memory_space=pltpu.SMEM),
        pl.BlockSpec(p["w_out_b"].shape, const2),
    ]
    kv_rows = WINDOW + seg
    scratch = [
        pltpu.VMEM((m, D_MODEL), BF16),
        pltpu.VMEM((m, D_MODEL), F32),
        pltpu.VMEM((m, D_MODEL), BF16),
        pltpu.VMEM((nseg, N_KV, 2, kv_rows, KV_W), BF16),
        pltpu.VMEM((nseg, N_KV, 2, kv_rows, 2 * KV_W), BF16),
        pltpu.VMEM((m, D_MODEL), F32),
        pltpu.VMEM((m, D_MODEL), BF16),
    ]
    return pl.pallas_call(
        functools.partial(_layer_b_kernel, nseg=nseg, seg=seg, mask_first_halo=prompt),
        grid=grid,
        in_specs=in_specs,
        out_specs=pl.BlockSpec((m, D_MODEL), row_map),
        out_shape=jax.ShapeDtypeStruct((rows, D_MODEL), F32),
        scratch_shapes=scratch,
        compiler_params=pltpu.CompilerParams(
            dimension_semantics=("arbitrary", "arbitrary"), vmem_limit_bytes=VMEM_LIMIT_BYTES),
        name="layer_b_prompt" if prompt else "layer_b_sample",
    )(x1, k_halo, k_cur, v_halo, v_cur, *weights, p["sinks"], p["w_out_b"])


def _prep_params(norm_a, w_in_a, conv_w, conv_b, w_gate_x, b_gate_x, w_gate_a, b_gate_a,
                 lru_lambda, w_out_a, norm_kv, w_kv, k_norm, norm_b, w_in_b, q_norm, sinks,
                 w_out_b):
    row = lambda v: v.reshape(1, -1).astype(F32)
    in_scale = jnp.concatenate([jnp.ones((D_MODEL,), F32), jnp.full((D_MODEL,), 0.5, F32)])
    return {
        "norm_a": row(norm_a[0]),
        "w_in_a": (w_in_a[0] * in_scale).astype(BF16),
        "conv_w": (0.5 * conv_w[0]).astype(F32),
        "conv_b": row(0.5 * conv_b[0]),
        "w_gate": jnp.concatenate([w_gate_x[0], w_gate_a[0]], axis=-1).astype(BF16),
        "b_gx": row(0.5 * b_gate_x[0]),
        "b_ga": row(0.5 * b_gate_a[0]),
        "lam": row(lru_lambda[0]),
        "w_out_a": w_out_a[0].astype(BF16),
        "norm_kv": row(norm_kv),
        "w_kv": w_kv.astype(BF16),
        "k_norm2": row(jnp.tile(k_norm, N_KV)),
        "norm_b": row(norm_b[0]),
        "w_in_b": (w_in_b[0] * in_scale).astype(BF16),
        "qg": row(jnp.tile(q_norm[0], N_HEADS) * (HEAD_DIM ** -0.5 * LOG2_E)),
        "sinks": sinks[0].astype(F32),
        "w_out_b": w_out_b[0].astype(BF16),
    }


def kernel(x_prompt, x_sample, state_conv, state_rglru, cache_k_win, cache_v_win, norm_a, w_in_a, conv_w, conv_b, w_gate_x, b_gate_x, w_gate_a, b_gate_a, lru_lambda, w_out_a, norm_kv, w_kv, k_norm, norm_b, w_in_b, q_norm, sinks, w_out_b):
    assert norm_a.shape[0] == 1 and norm_b.shape[0] == 1, "one recurrent + one attention layer"
    p = _prep_params(norm_a, w_in_a, conv_w, conv_b, w_gate_x, b_gate_x, w_gate_a, b_gate_a,
                     lru_lambda, w_out_a, norm_kv, w_kv, k_norm, norm_b, w_in_b, q_norm, sinks,
                     w_out_b)
    bp, tp, _ = x_prompt.shape
    bs, ts, _ = x_sample.shape

    x1p, kp, vp, conv_p, h_p = _layer_a(
        x_prompt, jnp.zeros((CONV_W - 1, bp, D_RNN), F32), jnp.zeros((bp, D_RNN), F32), p,
        "layer_a_prompt")
    tile_b = 1024
    kp2 = kp.reshape(bp * tp, KV_W)
    vp2 = vp.reshape(bp * tp, KV_W)
    kp3 = kp.reshape(bp * tp // WINDOW, WINDOW, KV_W)
    vp3 = vp.reshape(bp * tp // WINDOW, WINDOW, KV_W)
    y_p = _layer_b(x1p.reshape(bp * tp, D_MODEL), kp3, vp3, kp2, vp2, p, n_batch=bp, t_len=tp,
                   nseg=1, seg=tile_b, prompt=True)

    x1s, ks, vs, conv_s, h_s = _layer_a(
        x_sample, jnp.transpose(state_conv[0], (1, 0, 2)), state_rglru[0], p, "layer_a_sample")
    y_s = _layer_b(x1s.reshape(bs * ts, D_MODEL), cache_k_win.reshape(bs, WINDOW, KV_W),
                   cache_v_win.reshape(bs, WINDOW, KV_W), ks.reshape(bs * ts, KV_W),
                   vs.reshape(bs * ts, KV_W), p, n_batch=bs, t_len=ts, nseg=16, seg=ts,
                   prompt=False)

    kp4 = kp.reshape(bp, tp, N_KV, HEAD_DIM)[:, -WINDOW:]
    vp4 = vp.reshape(bp, tp, N_KV, HEAD_DIM)[:, -WINDOW:]
    ks4 = jnp.concatenate([cache_k_win, ks.reshape(bs, ts, N_KV, HEAD_DIM)], axis=1)[:, -WINDOW:]
    vs4 = jnp.concatenate([cache_v_win, vs.reshape(bs, ts, N_KV, HEAD_DIM)], axis=1)[:, -WINDOW:]
    return (y_p.reshape(bp, tp, D_MODEL), y_s.reshape(bs, ts, D_MODEL),
            jnp.transpose(conv_p, (1, 0, 2))[None], h_p[None], kp4, vp4,
            jnp.transpose(conv_s, (1, 0, 2))[None], h_s[None], ks4, vs4)
```

```python
import functools

import jax
import jax.numpy as jnp
from jax import lax
from jax.experimental import pallas as pl
from jax.experimental.pallas import tpu as pltpu

D_MODEL = 1024
D_RNN = 1024
GATE_BLOCK = 256
N_GATE_BLOCKS = D_RNN // GATE_BLOCK
CONV_W = 4
LRU_C = 8.0
HEAD_DIM = 64
N_HEADS = 16
N_KV = 2
GROUP = N_HEADS // N_KV
CHUNK = 64
WINDOW = 128
EPS = 1e-6
NEG = -1e30
LOG2_E = 1.4426950408889634
LN_2 = 0.6931471805599453

SUBLANES = 8
LANES = 128
BF16_SUBLANES = 16
MXU_N = 256
VMEM_LIMIT_BYTES = 56 * 1024 * 1024

F32 = jnp.float32
BF16 = jnp.bfloat16

KV_W = N_KV * HEAD_DIM
PAIR_W = 2 * HEAD_DIM
N_PAIRS = GROUP // 2
LOOKAHEAD = 2
B_TILE_ROWS = 1024
B_PART_ROWS = 256

PA_NORM_A, PA_CONV_B, PA_B_GX, PA_B_GA, PA_LAMBDA, PA_NORM_KV, PA_K_NORM = range(7)
PA_CONV_W = SUBLANES
PA_ROWS = 2 * SUBLANES
PB_NORM_B, PB_Q_GAIN = range(2)
PB_ROWS = SUBLANES

A_MAX_STEPS = 128
A_TILE_ROWS = 1024
OUT_K_SPLIT = D_RNN // 2


def _silu_of_half(h):
    return h * jnp.tanh(h) + h


def _norm_rows_to(x_ref, g, dst_ref, n_rows, blk, r0=0):
    for r in range(r0, r0 + n_rows, blk):
        x = x_ref[r:r + blk, :]
        rs = lax.rsqrt(jnp.mean(x * x, axis=-1, keepdims=True) + EPS)
        dst_ref[r:r + blk, :] = (x_ref[r:r + blk, :] * rs * g).astype(BF16)


def _layer_a_kernel(x_hbm, conv_in_ref, h_in_ref,
                    pa_ref, w_in_ref, w_gate_ref, w_out_ref, w_kv_ref,
                    x1_hbm, k_hbm, v_hbm, conv_out_ref, h_out_ref,
                    xin, x1o, ko, vo, sem_in, sem_out,
                    xn_scr, xbp_scr, gate_scr, xc_scr, xcb_scr, gx_scr, ga_scr, hy_scr, h_scr,
                    *, a_steps, nseq):
    a_rows = a_steps * nseq
    conv_pre = (CONV_W - 1) * nseq
    g = pl.program_id(0)
    t = pl.program_id(1)
    nt = pl.num_programs(1)
    total = pl.num_programs(0) * nt
    i = g * nt + t
    slot = lax.rem(i, 2)

    def in_copies(step, sl):
        gg = step // nt
        tt = lax.rem(step, nt)
        return [pltpu.make_async_copy(x_hbm.at[gg * nseq + j, pl.ds(tt * a_steps, a_steps), :],
                                      xin.at[sl, :, j, :], sem_in.at[sl]) for j in range(nseq)]

    def out_copies(step, sl):
        gg = step // nt
        tt = lax.rem(step, nt)
        cps = []
        for buf, dst in ((x1o, x1_hbm), (ko, k_hbm), (vo, v_hbm)):
            cps += [pltpu.make_async_copy(buf.at[sl, :, j, :],
                                          dst.at[gg * nseq + j, pl.ds(tt * a_steps, a_steps), :],
                                          sem_out.at[sl]) for j in range(nseq)]
        return cps

    @pl.when(i == 0)
    def _():
        for c in in_copies(i, slot):
            c.start()

    @pl.when(i + 1 < total)
    def _():
        for c in in_copies(i + 1, 1 - slot):
            c.start()

    @pl.when(t == 0)
    def _():
        xbp_scr[0:conv_pre, :] = conv_in_ref[...].reshape(conv_pre, D_RNN)
        h_scr[...] = h_in_ref[...]

    @pl.when(i >= 2)
    def _():
        for c in out_copies(i - 2, slot):
            c.wait()

    for c in in_copies(i, slot):
        c.wait()

    x_ref = xin.at[slot].reshape(a_rows, D_MODEL)
    x1_ref = x1o.at[slot].reshape(a_rows, D_MODEL)
    k_ref = ko.at[slot].reshape(a_rows, KV_W)
    v_ref = vo.at[slot].reshape(a_rows, KV_W)

    def pa_row(r, n=1):
        return pa_ref[r:r + n, :]

    _norm_rows_to(x_ref, pa_row(PA_NORM_A), xn_scr, a_rows, 128)
    lam = pa_row(PA_LAMBDA)
    log_sig_lam = jnp.minimum(lam, 0.0) - jnp.log1p(jnp.exp(-jnp.abs(lam)))
    kk = jnp.broadcast_to((0.5 * LRU_C * LOG2_E) * log_sig_lam, (nseq, D_RNN))
    b_gx = jnp.broadcast_to(pa_row(PA_B_GX), (nseq, D_RNN))
    b_ga = jnp.broadcast_to(pa_row(PA_B_GA), (nseq, D_RNN))

    cw = pa_row(PA_CONV_W, CONV_W)
    cb = pa_row(PA_CONV_B)
    conv_blk = 32

    def in_proj_x(n):
        cols = slice(n * GATE_BLOCK, (n + 1) * GATE_BLOCK)
        xbp_scr[conv_pre:, cols] = jnp.dot(xn_scr[...], w_in_ref[:, cols],
                                           preferred_element_type=F32)

    def in_proj_gate(n):
        cols = slice(n * GATE_BLOCK, (n + 1) * GATE_BLOCK)
        gate_scr[:, cols] = jnp.dot(
            xn_scr[...], w_in_ref[:, D_RNN + n * GATE_BLOCK:D_RNN + (n + 1) * GATE_BLOCK],
            preferred_element_type=F32)

    def conv_and_gates(n):
        cols = slice(n * GATE_BLOCK, (n + 1) * GATE_BLOCK)
        for r in range(0, a_rows, conv_blk):
            acc = cb[:, cols]
            for j in range(CONV_W):
                acc = acc + cw[j:j + 1, cols] * xbp_scr[r + j * nseq:r + j * nseq + conv_blk, cols]
            xc_scr[r:r + conv_blk, cols] = acc
            xcb_scr[r:r + conv_blk, cols] = acc.astype(BF16)
        tail = xbp_scr[a_rows:a_rows + conv_pre, cols]
        conv_out_ref[:, :, cols] = tail.reshape(CONV_W - 1, nseq, GATE_BLOCK)
        xbp_scr[0:conv_pre, cols] = tail

        res = jnp.dot(xcb_scr[:, cols], w_gate_ref[n], preferred_element_type=F32)
        gx_scr[:, cols] = res[:, :GATE_BLOCK]
        ga_scr[:, cols] = res[:, GATE_BLOCK:]

    def scan(n):
        cols = slice(n * GATE_BLOCK, (n + 1) * GATE_BLOCK)
        kk_n = kk[:, cols]
        bgx_n = b_gx[:, cols]
        bga_n = b_ga[:, cols]
        h = h_scr[:, cols]
        rows_per_store = max(BF16_SUBLANES, nseq)
        for r in range(0, a_rows, rows_per_store):
            hy = []
            for rr in range(r, r + rows_per_store, nseq):
                t_i = jnp.tanh(gx_scr[rr:rr + nseq, cols] + bgx_n)
                t_r = jnp.tanh(ga_scr[rr:rr + nseq, cols] + bga_n)
                log2_a = t_r * kk_n + kk_n
                a = jnp.exp2(log2_a)
                y = jnp.tanh(log2_a * (-LN_2)) * (a * a + 1.0)
                mult = jnp.where(y > 0.0, y * lax.rsqrt(y), 0.0)
                u = mult * xc_scr[rr:rr + nseq, cols]
                h = a * h + (u * t_i + u)
                hy.append(h * _silu_of_half(gate_scr[rr:rr + nseq, cols]))
            hy_scr[r:r + rows_per_store, cols] = jnp.concatenate(hy, axis=0).astype(BF16)
        h_scr[:, cols] = h
        h_out_ref[:, cols] = h

    def out_proj_first_part():
        for nn in range(D_MODEL // MXU_N):
            cols = slice(nn * MXU_N, (nn + 1) * MXU_N)
            x1_ref[:, cols] = x_ref[:, cols] + jnp.dot(
                hy_scr[:, :OUT_K_SPLIT], w_out_ref[:OUT_K_SPLIT, cols],
                preferred_element_type=F32)

    in_proj_x(0)
    in_proj_gate(0)
    in_proj_x(1)
    conv_and_gates(0)
    in_proj_gate(1)
    for n in range(N_GATE_BLOCKS):
        if n + 2 < N_GATE_BLOCKS:
            in_proj_x(n + 2)
        if n + 1 < N_GATE_BLOCKS:
            conv_and_gates(n + 1)
        if n + 2 < N_GATE_BLOCKS:
            in_proj_gate(n + 2)
        if (n + 1) * GATE_BLOCK == D_RNN - GATE_BLOCK:
            out_proj_first_part()
        scan(n)

    half_rows = a_rows // 2
    lo = lax.broadcasted_iota(jnp.int32, (half_rows, KV_W), 1) < HEAD_DIM
    for rh in range(2):
        rows = slice(rh * half_rows, (rh + 1) * half_rows)
        for n in range(D_MODEL // MXU_N):
            cols = slice(n * MXU_N, (n + 1) * MXU_N)
            x1_ref[rows, cols] = x1_ref[rows, cols] + jnp.dot(
                hy_scr[rows, OUT_K_SPLIT:], w_out_ref[OUT_K_SPLIT:, cols],
                preferred_element_type=F32)
        _norm_rows_to(x1_ref, pa_row(PA_NORM_KV), xn_scr, half_rows, 64, r0=rh * half_rows)
        kv = jnp.dot(xn_scr[rows, :], w_kv_ref[...], preferred_element_type=F32)
        k = kv[:, :KV_W]
        k2 = k * k
        ms_lo = jnp.sum(jnp.where(lo, k2, 0.0), axis=-1, keepdims=True) * (1.0 / HEAD_DIM)
        ms_hi = jnp.sum(jnp.where(lo, 0.0, k2), axis=-1, keepdims=True) * (1.0 / HEAD_DIM)
        rs = jnp.where(lo, lax.rsqrt(ms_lo + EPS), lax.rsqrt(ms_hi + EPS))
        k_ref[rows, :] = k * rs * pa_ref[PA_K_NORM:PA_K_NORM + 1, 0:KV_W]
        v_ref[rows, :] = kv[:, KV_W:]

    for c in out_copies(i, slot):
        c.start()

    @pl.when(i == total - 1)
    def _():
        @pl.when(i >= 1)
        def _():
            for c in out_copies(i - 1, 1 - slot):
                c.wait()
        for c in out_copies(i, slot):
            c.wait()


def _layer_a(x, conv_in, h_in, p, name):
    n_seq, t_len, _ = x.shape
    a_steps = min(t_len, A_MAX_STEPS)
    nseq = min(n_seq, max(SUBLANES, A_TILE_ROWS // a_steps), BF16_SUBLANES)
    a_rows = a_steps * nseq
    conv_pre = (CONV_W - 1) * nseq
    assert nseq % SUBLANES == 0 and n_seq % nseq == 0 and t_len % a_steps == 0
    grid = (n_seq // nseq, t_len // a_steps)
    const2 = lambda g, t: (0, 0)
    const3 = lambda g, t: (0, 0, 0)
    any_spec = pl.BlockSpec(memory_space=pl.ANY)

    weights = [p["pa"], p["w_in_a"], p["w_gate"], p["w_out_a"], p["w_kv"]]
    w_specs = [pl.BlockSpec(w.shape, const3 if w.ndim == 3 else const2,
                            pipeline_mode=pl.Buffered(1)) for w in weights]
    conv_spec = pl.BlockSpec((CONV_W - 1, nseq, D_RNN), lambda g, t: (0, g, 0))
    h_spec = pl.BlockSpec((nseq, D_RNN), lambda g, t: (g, 0))

    out_shape = [
        jax.ShapeDtypeStruct((n_seq, t_len, D_MODEL), F32),
        jax.ShapeDtypeStruct((n_seq, t_len, KV_W), F32),
        jax.ShapeDtypeStruct((n_seq, t_len, KV_W), F32),
        jax.ShapeDtypeStruct((CONV_W - 1, n_seq, D_RNN), F32),
        jax.ShapeDtypeStruct((n_seq, D_RNN), F32),
    ]
    scratch = [
        pltpu.VMEM((2, a_steps, nseq, D_MODEL), F32),
        pltpu.VMEM((2, a_steps, nseq, D_MODEL), F32),
        pltpu.VMEM((2, a_steps, nseq, KV_W), F32),
        pltpu.VMEM((2, a_steps, nseq, KV_W), F32),
        pltpu.SemaphoreType.DMA((2,)),
        pltpu.SemaphoreType.DMA((2,)),
        pltpu.VMEM((a_rows, D_MODEL), BF16),
        pltpu.VMEM((conv_pre + a_rows, D_RNN), F32),
        pltpu.VMEM((a_rows, D_RNN), F32),
        pltpu.VMEM((a_rows, D_RNN), F32),
        pltpu.VMEM((a_rows, D_RNN), BF16),
        pltpu.VMEM((a_rows, D_RNN), F32),
        pltpu.VMEM((a_rows, D_RNN), F32),
        pltpu.VMEM((a_rows, D_RNN), BF16),
        pltpu.VMEM((nseq, D_RNN), F32),
    ]
    return pl.pallas_call(
        functools.partial(_layer_a_kernel, a_steps=a_steps, nseq=nseq),
        grid=grid,
        in_specs=[any_spec, conv_spec, h_spec] + w_specs,
        out_specs=[any_spec, any_spec, any_spec, conv_spec, h_spec],
        out_shape=out_shape,
        scratch_shapes=scratch,
        compiler_params=pltpu.CompilerParams(
            dimension_semantics=("arbitrary", "arbitrary"), vmem_limit_bytes=VMEM_LIMIT_BYTES),
        name=name,
    )(x, conv_in, h_in, *weights)


def _layer_b_kernel(x1_ref, kh_ref, kc_ref, vh_ref, vc_ref, pb_ref, w_in_ref,
                    sinks_ref, w_out_ref, y_ref,
                    xn_scr, gate_scr, qh_scr, kmat_scr, vmat_scr, o_scr, og_scr,
                    *, nseg, seg, mask_first_halo):
    m = nseg * seg
    n_chunks = seg // CHUNK
    part = min(m, B_PART_ROWS)
    assert m % part == 0 and (part % seg == 0 or seg % part == 0)
    n_parts = m // part
    n_col_blocks = D_MODEL // MXU_N

    _norm_rows_to(x1_ref, pb_ref[PB_NORM_B:PB_NORM_B + 1, :], xn_scr, m, 128)

    lo_m = lax.broadcasted_iota(jnp.int32, (part, LANES), 1) < HEAD_DIM
    qg = pb_ref[PB_Q_GAIN:PB_Q_GAIN + 1, :]

    def q_block(h, n):
        rows = slice(h * part, (h + 1) * part)
        qn = jnp.dot(xn_scr[rows, :], w_in_ref[:, n * MXU_N:(n + 1) * MXU_N],
                     preferred_element_type=F32)
        for cc in range(MXU_N // LANES):
            col = n * MXU_N + cc * LANES
            x = qn[:, cc * LANES:(cc + 1) * LANES]
            sq = x * x
            ms_lo = jnp.sum(jnp.where(lo_m, sq, 0.0), axis=-1, keepdims=True) * (1.0 / HEAD_DIM)
            ms_hi = jnp.sum(jnp.where(lo_m, 0.0, sq), axis=-1, keepdims=True) * (1.0 / HEAD_DIM)
            rs = jnp.where(lo_m, lax.rsqrt(ms_lo + EPS), lax.rsqrt(ms_hi + EPS))
            qh_scr[rows, col:col + LANES] = (x * rs * qg[:, col:col + LANES]).astype(BF16)

    row_split = 2 if part >= 2 * MXU_N else 1

    def gate_piece(h, n, rh):
        r0 = h * part + rh * (part // row_split)
        rows = slice(r0, r0 + part // row_split)
        gate_scr[rows, n * MXU_N:(n + 1) * MXU_N] = jnp.dot(
            xn_scr[rows, :], w_in_ref[:, D_MODEL + n * MXU_N:D_MODEL + (n + 1) * MXU_N],
            preferred_element_type=F32)

    def gate_multiply(h):
        blk = 32
        for r in range(h * part, (h + 1) * part, blk):
            og_scr[r:r + blk, :] = (
                o_scr[r:r + blk, :] * _silu_of_half(gate_scr[r:r + blk, :])).astype(BF16)

    def out_piece(h, n):
        rows = slice(h * part, (h + 1) * part)
        cols = slice(n * MXU_N, (n + 1) * MXU_N)
        y_ref[rows, cols] = x1_ref[rows, cols] + jnp.dot(
            og_scr[rows, :], w_out_ref[:, cols], preferred_element_type=F32)

    lane = lax.broadcasted_iota(jnp.int32, (CHUNK, KV_W), 1)
    lo = lane < HEAD_DIM
    first = pl.program_id(1) == 0 if mask_first_halo else None

    for s in range(nseg):
        for r0, nrows, kval, vval in ((0, WINDOW, kh_ref[s], vh_ref[s]),
                                      (WINDOW, seg, kc_ref[s * seg:(s + 1) * seg, :],
                                       vc_ref[s * seg:(s + 1) * seg, :])):
            is_lo = lax.broadcasted_iota(jnp.int32, kval.shape, 1) < HEAD_DIM
            ind_e = jnp.where(is_lo, 1.0, 0.0).astype(BF16)
            ind_o = jnp.where(is_lo, 0.0, 1.0).astype(BF16)
            rows = slice(r0, r0 + nrows)
            for val, scr in ((kval, kmat_scr), (vval, vmat_scr)):
                swapped = pltpu.roll(val, HEAD_DIM, 1)
                for g in range(N_KV):
                    low = val if g == 0 else swapped
                    high = swapped if g == 0 else val
                    scr[s, g, 0, rows, 0:KV_W] = jnp.where(is_lo, low, 0.0).astype(BF16)
                    scr[s, g, 1, rows, 0:KV_W] = jnp.where(is_lo, 0.0, high).astype(BF16)
            for g in range(N_KV):
                vmat_scr[s, g, 0, rows, KV_W:] = ind_e
                vmat_scr[s, g, 1, rows, KV_W:] = ind_o

    def placed(scr, s, g, k0):
        return jnp.concatenate(
            [scr[s, g, 0, k0:k0 + WINDOW, :], scr[s, g, 1, k0:k0 + WINDOW, :],
             scr[s, g, 0, k0 + WINDOW:k0 + WINDOW + CHUNK, :],
             scr[s, g, 1, k0 + WINDOW:k0 + WINDOW + CHUNK, :]], axis=0)

    def scores(s, c, g):
        q0 = s * seg + c * CHUNK
        qst = jnp.concatenate(
            [qh_scr[q0:q0 + CHUNK, g * GROUP * HEAD_DIM + j * PAIR_W:
                    g * GROUP * HEAD_DIM + (j + 1) * PAIR_W] for j in range(N_PAIRS)],
            axis=0)
        return lax.dot_general(qst, placed(kmat_scr, s, g, c * CHUNK),
                               (((1,), (1,)), ((), ())),
                               preferred_element_type=F32)

    def softmax_pv(sc, s, c, g):
        p_rows, sink_rows = [], []
        for j in range(N_PAIRS):
            sj = sc[j * CHUNK:(j + 1) * CHUNK]
            c0 = sj[:, 0:KV_W]
            c1 = sj[:, KV_W:2 * KV_W]
            c2 = sj[:, 2 * KV_W:]
            if mask_first_halo and c * CHUNK < WINDOW:
                n_bad = jnp.where(first, WINDOW - c * CHUNK, 0)
                bad = lane < n_bad
                c0 = jnp.where(bad, NEG, c0)
                c1 = jnp.where(bad, NEG, c1)
            sink_e = sinks_ref[g * GROUP + 2 * j] * LOG2_E
            sink_o = sinks_ref[g * GROUP + 2 * j + 1] * LOG2_E
            m_e = jnp.maximum(jnp.max(jnp.maximum(c0, jnp.where(lo, c2, NEG)),
                                      axis=-1, keepdims=True), sink_e)
            m_o = jnp.maximum(jnp.max(jnp.maximum(c1, jnp.where(lo, NEG, c2)),
                                      axis=-1, keepdims=True), sink_o)
            p_rows.append(jnp.concatenate(
                [jnp.exp2(c0 - m_e), jnp.exp2(c1 - m_o),
                 jnp.exp2(c2 - jnp.where(lo, m_e, m_o))], axis=1).astype(BF16))
            sink_rows.append(jnp.where(lo, jnp.exp2(sink_e - m_e), jnp.exp2(sink_o - m_o)))
        pmat = jnp.concatenate(p_rows, axis=0)
        ov = jnp.dot(pmat, placed(vmat_scr, s, g, c * CHUNK),
                     preferred_element_type=F32)
        q0 = s * seg + c * CHUNK
        for j in range(N_PAIRS):
            rows = slice(j * CHUNK, (j + 1) * CHUNK)
            c_lo = g * GROUP * HEAD_DIM + j * PAIR_W
            o_scr[q0:q0 + CHUNK, c_lo:c_lo + PAIR_W] = (
                ov[rows, :KV_W] / (ov[rows, KV_W:] + sink_rows[j]))

    all_units = [(s, c, g) for s in range(nseg) for c in range(n_chunks) for g in range(N_KV)]

    for n in range(n_col_blocks):
        q_block(0, n)
    for h in range(n_parts):
        units = [u for u in all_units if (u[0] * seg + u[1] * CHUNK) // part == h]
        fillers = []
        if h >= 1:
            fillers += [functools.partial(out_piece, h - 1, n) for n in range(n_col_blocks)]
        gates = [functools.partial(gate_piece, h, n, rh)
                 for n in range(n_col_blocks) for rh in range(row_split)]
        nxt = ([functools.partial(q_block, h + 1, n) for n in range(n_col_blocks)]
               if h + 1 < n_parts else [])
        while gates or nxt:
            fillers += gates[:2]
            gates = gates[2:]
            fillers += nxt[:1]
            nxt = nxt[1:]
        fillers.pop(0)()
        per_unit = -(-len(fillers) // len(units))
        pending = [scores(*u) for u in units[:LOOKAHEAD]]
        for k, (s, c, g) in enumerate(units):
            sc = pending.pop(0)
            if k + LOOKAHEAD < len(units):
                pending.append(scores(*units[k + LOOKAHEAD]))
            for _ in range(min(per_unit, len(fillers))):
                fillers.pop(0)()
            softmax_pv(sc, s, c, g)
        assert not fillers
        gate_multiply(h)
    for n in range(n_col_blocks):
        out_piece(n_parts - 1, n)


def _layer_b(x1, k_halo, v_halo, k_cur, v_cur, p, *, n_batch, t_len, nseg, seg, prompt):
    m = nseg * seg
    rows = n_batch * t_len
    const2 = lambda *_: (0, 0)
    if prompt:
        steps = t_len // m
        per_b = t_len // WINDOW
        grid = (n_batch, steps)
        row_map = lambda b, t: (b * steps + t, 0)
        halo_map = lambda b, t: (b * per_b + jnp.maximum(t * (m // WINDOW) - 1, 0), 0, 0)
    else:
        grid = (n_batch // nseg, 1)
        row_map = lambda i, t: (i, 0)
        halo_map = lambda i, t: (i, 0, 0)

    weights = [p["pb"], p["w_in_b"]]
    in_specs = [
        pl.BlockSpec((m, D_MODEL), row_map),
        pl.BlockSpec((nseg, WINDOW, KV_W), halo_map),
        pl.BlockSpec((m, KV_W), row_map),
        pl.BlockSpec((nseg, WINDOW, KV_W), halo_map),
        pl.BlockSpec((m, KV_W), row_map),
    ] + [pl.BlockSpec(w.shape, const2) for w in weights] + [
        pl.BlockSpec(memory_space=pltpu.SMEM),
        pl.BlockSpec(p["w_out_b"].shape, const2),
    ]
    kv_rows = WINDOW + seg
    scratch = [
        pltpu.VMEM((m, D_MODEL), BF16),
        pltpu.VMEM((m, D_MODEL), F32),
        pltpu.VMEM((m, D_MODEL), BF16),
        pltpu.VMEM((nseg, N_KV, 2, kv_rows, KV_W), BF16),
        pltpu.VMEM((nseg, N_KV, 2, kv_rows, 2 * KV_W), BF16),
        pltpu.VMEM((m, D_MODEL), F32),
        pltpu.VMEM((m, D_MODEL), BF16),
    ]
    return pl.pallas_call(
        functools.partial(_layer_b_kernel, nseg=nseg, seg=seg, mask_first_halo=prompt),
        grid=grid,
        in_specs=in_specs,
        out_specs=pl.BlockSpec((m, D_MODEL), row_map),
        out_shape=jax.ShapeDtypeStruct((rows, D_MODEL), F32),
        scratch_shapes=scratch,
        compiler_params=pltpu.CompilerParams(
            dimension_semantics=("arbitrary", "arbitrary"), vmem_limit_bytes=VMEM_LIMIT_BYTES),
        name="layer_b_prompt" if prompt else "layer_b_sample",
    )(x1, k_halo, k_cur, v_halo, v_cur, *weights, p["sinks"], p["w_out_b"])


def _prep_params(norm_a, w_in_a, conv_w, conv_b, w_gate_x, b_gate_x, w_gate_a, b_gate_a,
                 lru_lambda, w_out_a, norm_kv, w_kv, k_norm, norm_b, w_in_b, q_norm, sinks,
                 w_out_b):
    row = lambda v: v.reshape(1, -1).astype(F32)
    in_scale = jnp.concatenate([jnp.ones((D_MODEL,), F32), jnp.full((D_MODEL,), 0.5, F32)])
    k_norm_row = jnp.pad(row(jnp.tile(k_norm, N_KV)), ((0, 0), (0, D_MODEL - KV_W)))
    pa_rows = {PA_NORM_A: row(norm_a[0]), PA_CONV_B: row(0.5 * conv_b[0]),
               PA_B_GX: row(0.5 * b_gate_x[0]), PA_B_GA: row(0.5 * b_gate_a[0]),
               PA_LAMBDA: row(lru_lambda[0]), PA_NORM_KV: row(norm_kv), PA_K_NORM: k_norm_row,
               PA_CONV_W: (0.5 * conv_w[0]).astype(F32)}
    zero_row = jnp.zeros((1, D_MODEL), F32)
    pa, r = [], 0
    while r < PA_ROWS:
        piece = pa_rows.get(r, zero_row)
        pa.append(piece)
        r += piece.shape[0]
    qg = row(jnp.tile(q_norm[0], N_HEADS) * (HEAD_DIM ** -0.5 * LOG2_E))
    pb = [row(norm_b[0]), qg] + [zero_row] * (PB_ROWS - 2)
    return {
        "pa": jnp.concatenate(pa, axis=0),
        "w_in_a": (w_in_a[0] * in_scale).astype(BF16),
        "w_gate": jnp.concatenate([w_gate_x[0], w_gate_a[0]], axis=-1).astype(BF16),
        "w_out_a": w_out_a[0].astype(BF16),
        "w_kv": w_kv.astype(BF16),
        "pb": jnp.concatenate(pb, axis=0),
        "w_in_b": (w_in_b[0] * in_scale).astype(BF16),
        "sinks": sinks[0].astype(F32),
        "w_out_b": w_out_b[0].astype(BF16),
    }


def kernel(x_prompt, x_sample, state_conv, state_rglru, cache_k_win, cache_v_win, norm_a, w_in_a, conv_w, conv_b, w_gate_x, b_gate_x, w_gate_a, b_gate_a, lru_lambda, w_out_a, norm_kv, w_kv, k_norm, norm_b, w_in_b, q_norm, sinks, w_out_b):
    assert norm_a.shape[0] == 1 and norm_b.shape[0] == 1, "one recurrent + one attention layer"
    p = _prep_params(norm_a, w_in_a, conv_w, conv_b, w_gate_x, b_gate_x, w_gate_a, b_gate_a,
                     lru_lambda, w_out_a, norm_kv, w_kv, k_norm, norm_b, w_in_b, q_norm, sinks,
                     w_out_b)
    bp, tp, _ = x_prompt.shape
    bs, ts, _ = x_sample.shape

    x1p, kp, vp, conv_p, h_p = _layer_a(
        x_prompt, jnp.zeros((CONV_W - 1, bp, D_RNN), F32), jnp.zeros((bp, D_RNN), F32), p,
        "layer_a_prompt")
    tile_b = min(tp, B_TILE_ROWS)
    kp2 = kp.reshape(bp * tp, KV_W)
    vp2 = vp.reshape(bp * tp, KV_W)
    kp3 = kp.reshape(bp * tp // WINDOW, WINDOW, KV_W)
    vp3 = vp.reshape(bp * tp // WINDOW, WINDOW, KV_W)
    y_p = _layer_b(x1p.reshape(bp * tp, D_MODEL), kp3, vp3, kp2, vp2, p, n_batch=bp, t_len=tp,
                   nseg=1, seg=tile_b, prompt=True)

    x1s, ks, vs, conv_s, h_s = _layer_a(
        x_sample, jnp.transpose(state_conv[0], (1, 0, 2)), state_rglru[0], p, "layer_a_sample")
    y_s = _layer_b(x1s.reshape(bs * ts, D_MODEL), cache_k_win.reshape(bs, WINDOW, KV_W),
                   cache_v_win.reshape(bs, WINDOW, KV_W), ks.reshape(bs * ts, KV_W),
                   vs.reshape(bs * ts, KV_W), p, n_batch=bs, t_len=ts,
                   nseg=min(bs, B_TILE_ROWS // ts), seg=ts, prompt=False)

    kp4 = kp.reshape(bp, tp, N_KV, HEAD_DIM)[:, -WINDOW:]
    vp4 = vp.reshape(bp, tp, N_KV, HEAD_DIM)[:, -WINDOW:]
    ks4 = jnp.concatenate([cache_k_win, ks.reshape(bs, ts, N_KV, HEAD_DIM)], axis=1)[:, -WINDOW:]
    vs4 = jnp.concatenate([cache_v_win, vs.reshape(bs, ts, N_KV, HEAD_DIM)], axis=1)[:, -WINDOW:]
    return (y_p.reshape(bp, tp, D_MODEL), y_s.reshape(bs, ts, D_MODEL),
            jnp.transpose(conv_p, (1, 0, 2))[None], h_p[None], kp4, vp4,
            jnp.transpose(conv_s, (1, 0, 2))[None], h_s[None], ks4, vs4)
```

```python
import functools

import jax
import jax.numpy as jnp
from jax import lax
from jax.experimental import pallas as pl
from jax.experimental.pallas import tpu as pltpu

D_MODEL = 1024
D_RNN = 1024
GATE_BLOCK = 256
N_GATE_BLOCKS = D_RNN // GATE_BLOCK
CONV_W = 4
LRU_C = 8.0
HEAD_DIM = 64
N_HEADS = 16
N_KV = 2
GROUP = N_HEADS // N_KV
CHUNK = 64
WINDOW = 128
EPS = 1e-6
NEG = -1e30
LOG2_E = 1.4426950408889634
LN_2 = 0.6931471805599453

SUBLANES = 8
LANES = 128
BF16_SUBLANES = 16
MXU_N = 256
VMEM_LIMIT_BYTES = 56 * 1024 * 1024

F32 = jnp.float32
BF16 = jnp.bfloat16

KV_W = N_KV * HEAD_DIM
PAIR_W = 2 * HEAD_DIM
N_PAIRS = GROUP // 2
LOOKAHEAD = 2
B_TILE_ROWS = 1024
B_PART_ROWS = 256

PA_NORM_A, PA_CONV_B, PA_B_GX, PA_B_GA, PA_LAMBDA, PA_NORM_KV, PA_K_NORM = range(7)
PA_CONV_W = SUBLANES
PA_ROWS = 2 * SUBLANES
PB_NORM_B, PB_Q_GAIN = range(2)
PB_ROWS = SUBLANES

A_MAX_STEPS = 128
A_TILE_ROWS = 1024
OUT_K_SPLIT = D_RNN // 2


def _silu_of_half(h):
    return h * jnp.tanh(h) + h


def _norm_rows_to(x_ref, g, dst_ref, n_rows, blk, r0=0):
    for r in range(r0, r0 + n_rows, blk):
        x = x_ref[r:r + blk, :]
        rs = lax.rsqrt(jnp.mean(x * x, axis=-1, keepdims=True) + EPS)
        dst_ref[r:r + blk, :] = (x_ref[r:r + blk, :] * rs * g).astype(BF16)


def _layer_a_kernel(x_hbm, conv_in_ref, h_in_ref,
                    pa_ref, w_in_ref, w_gate_ref, w_out_ref, w_kv_ref,
                    x1_hbm, k_hbm, v_hbm, conv_out_ref, h_out_ref,
                    xin, x1o, ko, vo, sem_in, sem_out,
                    xn_scr, xbp_scr, gate_scr, xc_scr, xcb_scr, gx_scr, ga_scr, hy_scr, h_scr,
                    *, a_steps, nseq):
    a_rows = a_steps * nseq
    conv_pre = (CONV_W - 1) * nseq
    g = pl.program_id(0)
    t = pl.program_id(1)
    nt = pl.num_programs(1)
    total = pl.num_programs(0) * nt
    i = g * nt + t
    slot = lax.rem(i, 2)

    def in_copies(step, sl):
        gg = step // nt
        tt = lax.rem(step, nt)
        return [pltpu.make_async_copy(x_hbm.at[gg * nseq + j, pl.ds(tt * a_steps, a_steps), :],
                                      xin.at[sl, :, j, :], sem_in.at[sl]) for j in range(nseq)]

    def out_copies(step, sl):
        gg = step // nt
        tt = lax.rem(step, nt)
        cps = []
        for buf, dst in ((x1o, x1_hbm), (ko, k_hbm), (vo, v_hbm)):
            cps += [pltpu.make_async_copy(buf.at[sl, :, j, :],
                                          dst.at[gg * nseq + j, pl.ds(tt * a_steps, a_steps), :],
                                          sem_out.at[sl]) for j in range(nseq)]
        return cps

    @pl.when(i == 0)
    def _():
        for c in in_copies(i, slot):
            c.start()

    @pl.when(i + 1 < total)
    def _():
        for c in in_copies(i + 1, 1 - slot):
            c.start()

    @pl.when(t == 0)
    def _():
        xbp_scr[0:conv_pre, :] = conv_in_ref[...].reshape(conv_pre, D_RNN)
        h_scr[...] = h_in_ref[...]

    @pl.when(i >= 2)
    def _():
        for c in out_copies(i - 2, slot):
            c.wait()

    for c in in_copies(i, slot):
        c.wait()

    x_ref = xin.at[slot].reshape(a_rows, D_MODEL)
    x1_ref = x1o.at[slot].reshape(a_rows, D_MODEL)
    k_ref = ko.at[slot].reshape(a_rows, KV_W)
    v_ref = vo.at[slot].reshape(a_rows, KV_W)

    def pa_row(r, n=1):
        return pa_ref[r:r + n, :]

    _norm_rows_to(x_ref, pa_row(PA_NORM_A), xn_scr, a_rows, 128)
    lam = pa_row(PA_LAMBDA)
    log_sig_lam = jnp.minimum(lam, 0.0) - jnp.log1p(jnp.exp(-jnp.abs(lam)))
    kk = jnp.broadcast_to((0.5 * LRU_C * LOG2_E) * log_sig_lam, (nseq, D_RNN))
    b_gx = jnp.broadcast_to(pa_row(PA_B_GX), (nseq, D_RNN))
    b_ga = jnp.broadcast_to(pa_row(PA_B_GA), (nseq, D_RNN))

    cw = pa_row(PA_CONV_W, CONV_W)
    cb = pa_row(PA_CONV_B)
    conv_blk = 32

    def in_proj_x(n):
        cols = slice(n * GATE_BLOCK, (n + 1) * GATE_BLOCK)
        xbp_scr[conv_pre:, cols] = jnp.dot(xn_scr[...], w_in_ref[:, cols],
                                           preferred_element_type=F32)

    def in_proj_gate(n):
        cols = slice(n * GATE_BLOCK, (n + 1) * GATE_BLOCK)
        gate_scr[:, cols] = jnp.dot(
            xn_scr[...], w_in_ref[:, D_RNN + n * GATE_BLOCK:D_RNN + (n + 1) * GATE_BLOCK],
            preferred_element_type=F32)

    def conv_and_gates(n):
        cols = slice(n * GATE_BLOCK, (n + 1) * GATE_BLOCK)
        for r in range(0, a_rows, conv_blk):
            acc = cb[:, cols]
            for j in range(CONV_W):
                acc = acc + cw[j:j + 1, cols] * xbp_scr[r + j * nseq:r + j * nseq + conv_blk, cols]
            xc_scr[r:r + conv_blk, cols] = acc
            xcb_scr[r:r + conv_blk, cols] = acc.astype(BF16)
        tail = xbp_scr[a_rows:a_rows + conv_pre, cols]
        conv_out_ref[:, :, cols] = tail.reshape(CONV_W - 1, nseq, GATE_BLOCK)
        xbp_scr[0:conv_pre, cols] = tail

        res = jnp.dot(xcb_scr[:, cols], w_gate_ref[n], preferred_element_type=F32)
        gx_scr[:, cols] = res[:, :GATE_BLOCK]
        ga_scr[:, cols] = res[:, GATE_BLOCK:]

    def scan(n):
        cols = slice(n * GATE_BLOCK, (n + 1) * GATE_BLOCK)
        kk_n = kk[:, cols]
        bgx_n = b_gx[:, cols]
        bga_n = b_ga[:, cols]
        h = h_scr[:, cols]
        rows_per_store = max(BF16_SUBLANES, nseq)
        for r in range(0, a_rows, rows_per_store):
            hy = []
            for rr in range(r, r + rows_per_store, nseq):
                t_i = jnp.tanh(gx_scr[rr:rr + nseq, cols] + bgx_n)
                t_r = jnp.tanh(ga_scr[rr:rr + nseq, cols] + bga_n)
                log2_a = t_r * kk_n + kk_n
                a = jnp.exp2(log2_a)
                y = jnp.tanh(log2_a * (-LN_2)) * (a * a + 1.0)
                mult = jnp.where(y > 0.0, y * lax.rsqrt(y), 0.0)
                u = mult * xc_scr[rr:rr + nseq, cols]
                h = a * h + (u * t_i + u)
                hy.append(h * _silu_of_half(gate_scr[rr:rr + nseq, cols]))
            hy_scr[r:r + rows_per_store, cols] = jnp.concatenate(hy, axis=0).astype(BF16)
        h_scr[:, cols] = h
        h_out_ref[:, cols] = h

    def out_proj_first_part():
        for nn in range(D_MODEL // MXU_N):
            cols = slice(nn * MXU_N, (nn + 1) * MXU_N)
            x1_ref[:, cols] = x_ref[:, cols] + jnp.dot(
                hy_scr[:, :OUT_K_SPLIT], w_out_ref[:OUT_K_SPLIT, cols],
                preferred_element_type=F32)

    in_proj_x(0)
    in_proj_gate(0)
    in_proj_x(1)
    conv_and_gates(0)
    in_proj_gate(1)
    for n in range(N_GATE_BLOCKS):
        if n + 2 < N_GATE_BLOCKS:
            in_proj_x(n + 2)
        if n + 1 < N_GATE_BLOCKS:
            conv_and_gates(n + 1)
        if n + 2 < N_GATE_BLOCKS:
            in_proj_gate(n + 2)
        if (n + 1) * GATE_BLOCK == D_RNN - GATE_BLOCK:
            out_proj_first_part()
        scan(n)

    half_rows = a_rows // 2
    lo = lax.broadcasted_iota(jnp.int32, (half_rows, KV_W), 1) < HEAD_DIM
    for rh in range(2):
        rows = slice(rh * half_rows, (rh + 1) * half_rows)
        for n in range(D_MODEL // MXU_N):
            cols = slice(n * MXU_N, (n + 1) * MXU_N)
            x1_ref[rows, cols] = x1_ref[rows, cols] + jnp.dot(
                hy_scr[rows, OUT_K_SPLIT:], w_out_ref[OUT_K_SPLIT:, cols],
                preferred_element_type=F32)
        _norm_rows_to(x1_ref, pa_row(PA_NORM_KV), xn_scr, half_rows, 64, r0=rh * half_rows)
        kv = jnp.dot(xn_scr[rows, :], w_kv_ref[...], preferred_element_type=F32)
        k = kv[:, :KV_W]
        k2 = k * k
        ms_lo = jnp.sum(jnp.where(lo, k2, 0.0), axis=-1, keepdims=True) * (1.0 / HEAD_DIM)
        ms_hi = jnp.sum(jnp.where(lo, 0.0, k2), axis=-1, keepdims=True) * (1.0 / HEAD_DIM)
        rs = jnp.where(lo, lax.rsqrt(ms_lo + EPS), lax.rsqrt(ms_hi + EPS))
        k_ref[rows, :] = k * rs * pa_ref[PA_K_NORM:PA_K_NORM + 1, 0:KV_W]
        v_ref[rows, :] = kv[:, KV_W:]

    for c in out_copies(i, slot):
        c.start()

    @pl.when(i == total - 1)
    def _():
        @pl.when(i >= 1)
        def _():
            for c in out_copies(i - 1, 1 - slot):
                c.wait()
        for c in out_copies(i, slot):
            c.wait()


def _layer_a(x, conv_in, h_in, p, name):
    n_seq, t_len, _ = x.shape
    a_steps = min(t_len, A_MAX_STEPS)
    nseq = min(n_seq, max(SUBLANES, A_TILE_ROWS // a_steps), BF16_SUBLANES)
    a_rows = a_steps * nseq
    conv_pre = (CONV_W - 1) * nseq
    assert nseq % SUBLANES == 0 and n_seq % nseq == 0 and t_len % a_steps == 0
    grid = (n_seq // nseq, t_len // a_steps)
    const2 = lambda g, t: (0, 0)
    const3 = lambda g, t: (0, 0, 0)
    any_spec = pl.BlockSpec(memory_space=pl.ANY)

    weights = [p["pa"], p["w_in_a"], p["w_gate"], p["w_out_a"], p["w_kv"]]
    w_specs = [pl.BlockSpec(w.shape, const3 if w.ndim == 3 else const2,
                            pipeline_mode=pl.Buffered(1)) for w in weights]
    conv_spec = pl.BlockSpec((CONV_W - 1, nseq, D_RNN), lambda g, t: (0, g, 0))
    h_spec = pl.BlockSpec((nseq, D_RNN), lambda g, t: (g, 0))

    out_shape = [
        jax.ShapeDtypeStruct((n_seq, t_len, D_MODEL), F32),
        jax.ShapeDtypeStruct((n_seq, t_len, KV_W), F32),
        jax.ShapeDtypeStruct((n_seq, t_len, KV_W), F32),
        jax.ShapeDtypeStruct((CONV_W - 1, n_seq, D_RNN), F32),
        jax.ShapeDtypeStruct((n_seq, D_RNN), F32),
    ]
    scratch = [
        pltpu.VMEM((2, a_steps, nseq, D_MODEL), F32),
        pltpu.VMEM((2, a_steps, nseq, D_MODEL), F32),
        pltpu.VMEM((2, a_steps, nseq, KV_W), F32),
        pltpu.VMEM((2, a_steps, nseq, KV_W), F32),
        pltpu.SemaphoreType.DMA((2,)),
        pltpu.SemaphoreType.DMA((2,)),
        pltpu.VMEM((a_rows, D_MODEL), BF16),
        pltpu.VMEM((conv_pre + a_rows, D_RNN), F32),
        pltpu.VMEM((a_rows, D_RNN), F32),
        pltpu.VMEM((a_rows, D_RNN), F32),
        pltpu.VMEM((a_rows, D_RNN), BF16),
        pltpu.VMEM((a_rows, D_RNN), F32),
        pltpu.VMEM((a_rows, D_RNN), F32),
        pltpu.VMEM((a_rows, D_RNN), BF16),
        pltpu.VMEM((nseq, D_RNN), F32),
    ]
    return pl.pallas_call(
        functools.partial(_layer_a_kernel, a_steps=a_steps, nseq=nseq),
        grid=grid,
        in_specs=[any_spec, conv_spec, h_spec] + w_specs,
        out_specs=[any_spec, any_spec, any_spec, conv_spec, h_spec],
        out_shape=out_shape,
        scratch_shapes=scratch,
        compiler_params=pltpu.CompilerParams(
            dimension_semantics=("arbitrary", "arbitrary"), vmem_limit_bytes=VMEM_LIMIT_BYTES),
        name=name,
    )(x, conv_in, h_in, *weights)


def _layer_b_kernel(x1_ref, kh_ref, kc_ref, vh_ref, vc_ref, pb_ref, w_in_ref,
                    sinks_ref, w_out_ref, y_ref, kt_ref, vt_ref,
                    xn_scr, gate_scr, qh_scr, kmat_scr, vmat_scr, o_scr, og_scr,
                    *, nseg, seg, mask_first_halo):
    m = nseg * seg
    n_chunks = seg // CHUNK
    part = min(m, B_PART_ROWS)
    assert m % part == 0 and (part % seg == 0 or seg % part == 0)
    n_parts = m // part
    n_col_blocks = D_MODEL // MXU_N

    _norm_rows_to(x1_ref, pb_ref[PB_NORM_B:PB_NORM_B + 1, :], xn_scr, m, 128)

    lo_m = lax.broadcasted_iota(jnp.int32, (part, LANES), 1) < HEAD_DIM
    qg = pb_ref[PB_Q_GAIN:PB_Q_GAIN + 1, :]

    def q_block(h, n):
        rows = slice(h * part, (h + 1) * part)
        qn = jnp.dot(xn_scr[rows, :], w_in_ref[:, n * MXU_N:(n + 1) * MXU_N],
                     preferred_element_type=F32)
        for cc in range(MXU_N // LANES):
            col = n * MXU_N + cc * LANES
            x = qn[:, cc * LANES:(cc + 1) * LANES]
            sq = x * x
            ms_lo = jnp.sum(jnp.where(lo_m, sq, 0.0), axis=-1, keepdims=True) * (1.0 / HEAD_DIM)
            ms_hi = jnp.sum(jnp.where(lo_m, 0.0, sq), axis=-1, keepdims=True) * (1.0 / HEAD_DIM)
            rs = jnp.where(lo_m, lax.rsqrt(ms_lo + EPS), lax.rsqrt(ms_hi + EPS))
            qh_scr[rows, col:col + LANES] = (x * rs * qg[:, col:col + LANES]).astype(BF16)

    row_split = 2 if part >= 2 * MXU_N else 1

    def gate_piece(h, n, rh):
        r0 = h * part + rh * (part // row_split)
        rows = slice(r0, r0 + part // row_split)
        gate_scr[rows, n * MXU_N:(n + 1) * MXU_N] = jnp.dot(
            xn_scr[rows, :], w_in_ref[:, D_MODEL + n * MXU_N:D_MODEL + (n + 1) * MXU_N],
            preferred_element_type=F32)

    def gate_multiply(h):
        blk = 32
        for r in range(h * part, (h + 1) * part, blk):
            og_scr[r:r + blk, :] = (
                o_scr[r:r + blk, :] * _silu_of_half(gate_scr[r:r + blk, :])).astype(BF16)

    def out_piece(h, n):
        rows = slice(h * part, (h + 1) * part)
        cols = slice(n * MXU_N, (n + 1) * MXU_N)
        y_ref[rows, cols] = x1_ref[rows, cols] + jnp.dot(
            og_scr[rows, :], w_out_ref[:, cols], preferred_element_type=F32)

    lane = lax.broadcasted_iota(jnp.int32, (CHUNK, KV_W), 1)
    lo = lane < HEAD_DIM
    first = pl.program_id(1) == 0 if mask_first_halo else None

    for s in range(nseg):
        for r0, nrows, kval, vval in ((0, WINDOW, kh_ref[s], vh_ref[s]),
                                      (WINDOW, seg, kc_ref[s * seg:(s + 1) * seg, :],
                                       vc_ref[s * seg:(s + 1) * seg, :])):
            is_lo = lax.broadcasted_iota(jnp.int32, kval.shape, 1) < HEAD_DIM
            ind_e = jnp.where(is_lo, 1.0, 0.0).astype(BF16)
            ind_o = jnp.where(is_lo, 0.0, 1.0).astype(BF16)
            rows = slice(r0, r0 + nrows)
            for val, scr in ((kval, kmat_scr), (vval, vmat_scr)):
                swapped = pltpu.roll(val, HEAD_DIM, 1)
                for g in range(N_KV):
                    low = val if g == 0 else swapped
                    high = swapped if g == 0 else val
                    scr[s, g, 0, rows, 0:KV_W] = jnp.where(is_lo, low, 0.0).astype(BF16)
                    scr[s, g, 1, rows, 0:KV_W] = jnp.where(is_lo, 0.0, high).astype(BF16)
            for g in range(N_KV):
                vmat_scr[s, g, 0, rows, KV_W:] = ind_e
                vmat_scr[s, g, 1, rows, KV_W:] = ind_o

    def placed(scr, s, g, k0):
        return jnp.concatenate(
            [scr[s, g, 0, k0:k0 + WINDOW, :], scr[s, g, 1, k0:k0 + WINDOW, :],
             scr[s, g, 0, k0 + WINDOW:k0 + WINDOW + CHUNK, :],
             scr[s, g, 1, k0 + WINDOW:k0 + WINDOW + CHUNK, :]], axis=0)

    def scores(s, c, g):
        q0 = s * seg + c * CHUNK
        qst = jnp.concatenate(
            [qh_scr[q0:q0 + CHUNK, g * GROUP * HEAD_DIM + j * PAIR_W:
                    g * GROUP * HEAD_DIM + (j + 1) * PAIR_W] for j in range(N_PAIRS)],
            axis=0)
        return lax.dot_general(qst, placed(kmat_scr, s, g, c * CHUNK),
                               (((1,), (1,)), ((), ())),
                               preferred_element_type=F32)

    def softmax_pv(sc, s, c, g):
        p_rows, sink_rows = [], []
        for j in range(N_PAIRS):
            sj = sc[j * CHUNK:(j + 1) * CHUNK]
            c0 = sj[:, 0:KV_W]
            c1 = sj[:, KV_W:2 * KV_W]
            c2 = sj[:, 2 * KV_W:]
            if mask_first_halo and c * CHUNK < WINDOW:
                n_bad = jnp.where(first, WINDOW - c * CHUNK, 0)
                bad = lane < n_bad
                c0 = jnp.where(bad, NEG, c0)
                c1 = jnp.where(bad, NEG, c1)
            sink_e = sinks_ref[g * GROUP + 2 * j] * LOG2_E
            sink_o = sinks_ref[g * GROUP + 2 * j + 1] * LOG2_E
            m_e = jnp.maximum(jnp.max(jnp.maximum(c0, jnp.where(lo, c2, NEG)),
                                      axis=-1, keepdims=True), sink_e)
            m_o = jnp.maximum(jnp.max(jnp.maximum(c1, jnp.where(lo, NEG, c2)),
                                      axis=-1, keepdims=True), sink_o)
            p_rows.append(jnp.concatenate(
                [jnp.exp2(c0 - m_e), jnp.exp2(c1 - m_o),
                 jnp.exp2(c2 - jnp.where(lo, m_e, m_o))], axis=1).astype(BF16))
            sink_rows.append(jnp.where(lo, jnp.exp2(sink_e - m_e), jnp.exp2(sink_o - m_o)))
        pmat = jnp.concatenate(p_rows, axis=0)
        ov = jnp.dot(pmat, placed(vmat_scr, s, g, c * CHUNK),
                     preferred_element_type=F32)
        q0 = s * seg + c * CHUNK
        for j in range(N_PAIRS):
            rows = slice(j * CHUNK, (j + 1) * CHUNK)
            c_lo = g * GROUP * HEAD_DIM + j * PAIR_W
            o_scr[q0:q0 + CHUNK, c_lo:c_lo + PAIR_W] = (
                ov[rows, :KV_W] / (ov[rows, KV_W:] + sink_rows[j]))

    all_units = [(s, c, g) for s in range(nseg) for c in range(n_chunks) for g in range(N_KV)]

    for n in range(n_col_blocks):
        q_block(0, n)
    for h in range(n_parts):
        units = [u for u in all_units if (u[0] * seg + u[1] * CHUNK) // part == h]
        fillers = []
        if h >= 1:
            fillers += [functools.partial(out_piece, h - 1, n) for n in range(n_col_blocks)]
        gates = [functools.partial(gate_piece, h, n, rh)
                 for n in range(n_col_blocks) for rh in range(row_split)]
        nxt = ([functools.partial(q_block, h + 1, n) for n in range(n_col_blocks)]
               if h + 1 < n_parts else [])
        while gates or nxt:
            fillers += gates[:2]
            gates = gates[2:]
            fillers += nxt[:1]
            nxt = nxt[1:]
        fillers.pop(0)()
        per_unit = -(-len(fillers) // len(units))
        pending = [scores(*u) for u in units[:LOOKAHEAD]]
        for k, (s, c, g) in enumerate(units):
            sc = pending.pop(0)
            if k + LOOKAHEAD < len(units):
                pending.append(scores(*units[k + LOOKAHEAD]))
            for _ in range(min(per_unit, len(fillers))):
                fillers.pop(0)()
            softmax_pv(sc, s, c, g)
        assert not fillers
        gate_multiply(h)
    for n in range(n_col_blocks):
        out_piece(n_parts - 1, n)

    def write_tails():
        for s in range(nseg):
            for h_ref, c_ref, t_ref in ((kh_ref, kc_ref, kt_ref), (vh_ref, vc_ref, vt_ref)):
                if seg >= WINDOW:
                    t_ref[s] = c_ref[(s + 1) * seg - WINDOW:(s + 1) * seg, :]
                else:
                    t_ref[s, 0:WINDOW - seg, :] = h_ref[s, seg:, :]
                    t_ref[s, WINDOW - seg:, :] = c_ref[s * seg:(s + 1) * seg, :]

    if mask_first_halo:
        pl.when(pl.program_id(1) == pl.num_programs(1) - 1)(write_tails)
    else:
        write_tails()


def _layer_b(x1, k_halo, v_halo, k_cur, v_cur, p, *, n_batch, t_len, nseg, seg, prompt):
    m = nseg * seg
    rows = n_batch * t_len
    const2 = lambda *_: (0, 0)
    if prompt:
        steps = t_len // m
        per_b = t_len // WINDOW
        grid = (n_batch, steps)
        row_map = lambda b, t: (b * steps + t, 0)
        halo_map = lambda b, t: (b * per_b + jnp.maximum(t * (m // WINDOW) - 1, 0), 0, 0)
    else:
        grid = (n_batch // nseg, 1)
        row_map = lambda i, t: (i, 0)
        halo_map = lambda i, t: (i, 0, 0)

    weights = [p["pb"], p["w_in_b"]]
    in_specs = [
        pl.BlockSpec((m, D_MODEL), row_map),
        pl.BlockSpec((nseg, WINDOW, KV_W), halo_map),
        pl.BlockSpec((m, KV_W), row_map),
        pl.BlockSpec((nseg, WINDOW, KV_W), halo_map),
        pl.BlockSpec((m, KV_W), row_map),
    ] + [pl.BlockSpec(w.shape, const2) for w in weights] + [
        pl.BlockSpec(memory_space=pltpu.SMEM),
        pl.BlockSpec(p["w_out_b"].shape, const2),
    ]
    tail_spec = pl.BlockSpec((nseg, WINDOW, KV_W), lambda a, t: (a, 0, 0))
    tail_shape = jax.ShapeDtypeStruct((n_batch, WINDOW, KV_W), F32)
    kv_rows = WINDOW + seg
    scratch = [
        pltpu.VMEM((m, D_MODEL), BF16),
        pltpu.VMEM((m, D_MODEL), F32),
        pltpu.VMEM((m, D_MODEL), BF16),
        pltpu.VMEM((nseg, N_KV, 2, kv_rows, KV_W), BF16),
        pltpu.VMEM((nseg, N_KV, 2, kv_rows, 2 * KV_W), BF16),
        pltpu.VMEM((m, D_MODEL), F32),
        pltpu.VMEM((m, D_MODEL), BF16),
    ]
    return pl.pallas_call(
        functools.partial(_layer_b_kernel, nseg=nseg, seg=seg, mask_first_halo=prompt),
        grid=grid,
        in_specs=in_specs,
        out_specs=[pl.BlockSpec((m, D_MODEL), row_map), tail_spec, tail_spec],
        out_shape=[jax.ShapeDtypeStruct((rows, D_MODEL), F32), tail_shape, tail_shape],
        scratch_shapes=scratch,
        compiler_params=pltpu.CompilerParams(
            dimension_semantics=("arbitrary", "arbitrary"), vmem_limit_bytes=VMEM_LIMIT_BYTES),
        name="layer_b_prompt" if prompt else "layer_b_sample",
    )(x1, k_halo, k_cur, v_halo, v_cur, *weights, p["sinks"], p["w_out_b"])


def _prep_params(norm_a, w_in_a, conv_w, conv_b, w_gate_x, b_gate_x, w_gate_a, b_gate_a,
                 lru_lambda, w_out_a, norm_kv, w_kv, k_norm, norm_b, w_in_b, q_norm, sinks,
                 w_out_b):
    row = lambda v: v.reshape(1, -1).astype(F32)
    in_scale = jnp.concatenate([jnp.ones((D_MODEL,), F32), jnp.full((D_MODEL,), 0.5, F32)])
    k_norm_row = jnp.pad(row(jnp.tile(k_norm, N_KV)), ((0, 0), (0, D_MODEL - KV_W)))
    pa_rows = {PA_NORM_A: row(norm_a[0]), PA_CONV_B: row(0.5 * conv_b[0]),
               PA_B_GX: row(0.5 * b_gate_x[0]), PA_B_GA: row(0.5 * b_gate_a[0]),
               PA_LAMBDA: row(lru_lambda[0]), PA_NORM_KV: row(norm_kv), PA_K_NORM: k_norm_row,
               PA_CONV_W: (0.5 * conv_w[0]).astype(F32)}
    zero_row = jnp.zeros((1, D_MODEL), F32)
    pa, r = [], 0
    while r < PA_ROWS:
        piece = pa_rows.get(r, zero_row)
        pa.append(piece)
        r += piece.shape[0]
    qg = row(jnp.tile(q_norm[0], N_HEADS) * (HEAD_DIM ** -0.5 * LOG2_E))
    pb = [row(norm_b[0]), qg] + [zero_row] * (PB_ROWS - 2)
    return {
        "pa": jnp.concatenate(pa, axis=0),
        "w_in_a": (w_in_a[0] * in_scale).astype(BF16),
        "w_gate": jnp.concatenate([w_gate_x[0], w_gate_a[0]], axis=-1).astype(BF16),
        "w_out_a": w_out_a[0].astype(BF16),
        "w_kv": w_kv.astype(BF16),
        "pb": jnp.concatenate(pb, axis=0),
        "w_in_b": (w_in_b[0] * in_scale).astype(BF16),
        "sinks": sinks[0].astype(F32),
        "w_out_b": w_out_b[0].astype(BF16),
    }


def kernel(x_prompt, x_sample, state_conv, state_rglru, cache_k_win, cache_v_win, norm_a, w_in_a, conv_w, conv_b, w_gate_x, b_gate_x, w_gate_a, b_gate_a, lru_lambda, w_out_a, norm_kv, w_kv, k_norm, norm_b, w_in_b, q_norm, sinks, w_out_b):
    assert norm_a.shape[0] == 1 and norm_b.shape[0] == 1, "one recurrent + one attention layer"
    p = _prep_params(norm_a, w_in_a, conv_w, conv_b, w_gate_x, b_gate_x, w_gate_a, b_gate_a,
                     lru_lambda, w_out_a, norm_kv, w_kv, k_norm, norm_b, w_in_b, q_norm, sinks,
                     w_out_b)
    bp, tp, _ = x_prompt.shape
    bs, ts, _ = x_sample.shape

    x1p, kp, vp, conv_p, h_p = _layer_a(
        x_prompt, jnp.zeros((CONV_W - 1, bp, D_RNN), F32), jnp.zeros((bp, D_RNN), F32), p,
        "layer_a_prompt")
    tile_b = min(tp, B_TILE_ROWS)
    kp2 = kp.reshape(bp * tp, KV_W)
    vp2 = vp.reshape(bp * tp, KV_W)
    kp3 = kp.reshape(bp * tp // WINDOW, WINDOW, KV_W)
    vp3 = vp.reshape(bp * tp // WINDOW, WINDOW, KV_W)
    y_p, kt_p, vt_p = _layer_b(x1p.reshape(bp * tp, D_MODEL), kp3, vp3, kp2, vp2, p,
                               n_batch=bp, t_len=tp, nseg=1, seg=tile_b, prompt=True)

    x1s, ks, vs, conv_s, h_s = _layer_a(
        x_sample, jnp.transpose(state_conv[0], (1, 0, 2)), state_rglru[0], p, "layer_a_sample")
    y_s, kt_s, vt_s = _layer_b(
        x1s.reshape(bs * ts, D_MODEL), cache_k_win.reshape(bs, WINDOW, KV_W),
        cache_v_win.reshape(bs, WINDOW, KV_W), ks.reshape(bs * ts, KV_W),
        vs.reshape(bs * ts, KV_W), p, n_batch=bs, t_len=ts,
        nseg=min(bs, B_TILE_ROWS // ts), seg=ts, prompt=False)

    heads = lambda a: a.reshape(a.shape[0], WINDOW, N_KV, HEAD_DIM)
    return (y_p.reshape(bp, tp, D_MODEL), y_s.reshape(bs, ts, D_MODEL),
            jnp.transpose(conv_p, (1, 0, 2))[None], h_p[None], heads(kt_p), heads(vt_p),
            jnp.transpose(conv_s, (1, 0, 2))[None], h_s[None], heads(kt_s), heads(vt_s))
```

```python
import functools

import jax
import jax.numpy as jnp
from jax import lax
from jax.experimental import pallas as pl
from jax.experimental.pallas import tpu as pltpu

D_MODEL = 1024
D_RNN = 1024
GATE_BLOCK = 256
N_GATE_BLOCKS = D_RNN // GATE_BLOCK
CONV_W = 4
LRU_C = 8.0
HEAD_DIM = 64
N_HEADS = 16
N_KV = 2
GROUP = N_HEADS // N_KV
CHUNK = 64
WINDOW = 128
EPS = 1e-6
NEG = -1e30
LOG2_E = 1.4426950408889634
LN_2 = 0.6931471805599453

SUBLANES = 8
LANES = 128
BF16_SUBLANES = 16
MXU_N = 256
VMEM_LIMIT_BYTES = 56 * 1024 * 1024

F32 = jnp.float32
BF16 = jnp.bfloat16

KV_W = N_KV * HEAD_DIM
PAIR_W = 2 * HEAD_DIM
N_PAIRS = GROUP // 2
LOOKAHEAD = 2
B_TILE_ROWS = 1024
B_PART_ROWS = 256

PA_NORM_A, PA_CONV_B, PA_B_GX, PA_B_GA, PA_LAMBDA, PA_NORM_KV, PA_K_NORM = range(7)
PA_CONV_W = SUBLANES
PA_ROWS = 2 * SUBLANES
PB_NORM_B, PB_Q_GAIN = range(2)
PB_ROWS = SUBLANES

A_MAX_STEPS = 128
A_TILE_ROWS = 1024
OUT_K_SPLIT = D_RNN // 2


def _silu_of_half(h):
    return h * jnp.tanh(h) + h


def _norm_rows_to(x_ref, g, dst_ref, n_rows, blk, r0=0):
    for r in range(r0, r0 + n_rows, blk):
        x = x_ref[r:r + blk, :]
        rs = lax.rsqrt(jnp.mean(x * x, axis=-1, keepdims=True) + EPS)
        dst_ref[r:r + blk, :] = (x_ref[r:r + blk, :] * rs * g).astype(BF16)


def _layer_a_kernel(x_hbm, conv_in_ref, h_in_ref,
                    pa_ref, w_in_ref, w_gate_ref, w_out_ref, w_kv_ref,
                    x1_hbm, k_hbm, v_hbm, conv_out_ref, h_out_ref,
                    xin, x1o, ko, vo, sem_in, sem_out,
                    xn_scr, xbp_scr, gate_scr, xc_scr, xcb_scr, gx_scr, ga_scr, hy_scr, h_scr,
                    *, a_steps, nseq):
    a_rows = a_steps * nseq
    conv_pre = (CONV_W - 1) * nseq
    g = pl.program_id(0)
    t = pl.program_id(1)
    nt = pl.num_programs(1)
    total = pl.num_programs(0) * nt
    i = g * nt + t
    slot = lax.rem(i, 2)

    def in_copies(step, sl):
        gg = step // nt
        tt = lax.rem(step, nt)
        return [pltpu.make_async_copy(x_hbm.at[gg * nseq + j, pl.ds(tt * a_steps, a_steps), :],
                                      xin.at[sl, :, j, :], sem_in.at[sl]) for j in range(nseq)]

    def out_copies(step, sl):
        gg = step // nt
        tt = lax.rem(step, nt)
        cps = []
        for buf, dst in ((x1o, x1_hbm), (ko, k_hbm), (vo, v_hbm)):
            cps += [pltpu.make_async_copy(buf.at[sl, :, j, :],
                                          dst.at[gg * nseq + j, pl.ds(tt * a_steps, a_steps), :],
                                          sem_out.at[sl]) for j in range(nseq)]
        return cps

    @pl.when(i == 0)
    def _():
        for c in in_copies(i, slot):
            c.start()

    @pl.when(i + 1 < total)
    def _():
        for c in in_copies(i + 1, 1 - slot):
            c.start()

    @pl.when(t == 0)
    def _():
        for k in range(CONV_W - 1):
            xbp_scr[k * nseq:(k + 1) * nseq, :] = conv_in_ref[:, k, :]
        h_scr[...] = h_in_ref[...]

    @pl.when(i >= 2)
    def _():
        for c in out_copies(i - 2, slot):
            c.wait()

    for c in in_copies(i, slot):
        c.wait()

    x_ref = xin.at[slot].reshape(a_rows, D_MODEL)
    x1_ref = x1o.at[slot].reshape(a_rows, D_MODEL)
    k_ref = ko.at[slot].reshape(a_rows, KV_W)
    v_ref = vo.at[slot].reshape(a_rows, KV_W)

    def pa_row(r, n=1):
        return pa_ref[r:r + n, :]

    _norm_rows_to(x_ref, pa_row(PA_NORM_A), xn_scr, a_rows, 128)
    lam = pa_row(PA_LAMBDA)
    log_sig_lam = jnp.minimum(lam, 0.0) - jnp.log1p(jnp.exp(-jnp.abs(lam)))
    kk = jnp.broadcast_to((0.5 * LRU_C * LOG2_E) * log_sig_lam, (nseq, D_RNN))
    b_gx = jnp.broadcast_to(pa_row(PA_B_GX), (nseq, D_RNN))
    b_ga = jnp.broadcast_to(pa_row(PA_B_GA), (nseq, D_RNN))

    cw = pa_row(PA_CONV_W, CONV_W)
    cb = pa_row(PA_CONV_B)
    conv_blk = 32

    def in_proj_x(n):
        cols = slice(n * GATE_BLOCK, (n + 1) * GATE_BLOCK)
        xbp_scr[conv_pre:, cols] = jnp.dot(xn_scr[...], w_in_ref[:, cols],
                                           preferred_element_type=F32)

    def in_proj_gate(n):
        cols = slice(n * GATE_BLOCK, (n + 1) * GATE_BLOCK)
        gate_scr[:, cols] = jnp.dot(
            xn_scr[...], w_in_ref[:, D_RNN + n * GATE_BLOCK:D_RNN + (n + 1) * GATE_BLOCK],
            preferred_element_type=F32)

    def conv_and_gates(n):
        cols = slice(n * GATE_BLOCK, (n + 1) * GATE_BLOCK)
        for r in range(0, a_rows, conv_blk):
            acc = cb[:, cols]
            for j in range(CONV_W):
                acc = acc + cw[j:j + 1, cols] * xbp_scr[r + j * nseq:r + j * nseq + conv_blk, cols]
            xc_scr[r:r + conv_blk, cols] = acc
            xcb_scr[r:r + conv_blk, cols] = acc.astype(BF16)
        tail = xbp_scr[a_rows:a_rows + conv_pre, cols]
        for k in range(CONV_W - 1):
            conv_out_ref[:, k, cols] = tail[k * nseq:(k + 1) * nseq, :]
        xbp_scr[0:conv_pre, cols] = tail

        res = jnp.dot(xcb_scr[:, cols], w_gate_ref[n], preferred_element_type=F32)
        gx_scr[:, cols] = res[:, :GATE_BLOCK]
        ga_scr[:, cols] = res[:, GATE_BLOCK:]

    def scan(n):
        cols = slice(n * GATE_BLOCK, (n + 1) * GATE_BLOCK)
        kk_n = kk[:, cols]
        bgx_n = b_gx[:, cols]
        bga_n = b_ga[:, cols]
        h = h_scr[:, cols]
        rows_per_store = max(BF16_SUBLANES, nseq)
        for r in range(0, a_rows, rows_per_store):
            hy = []
            for rr in range(r, r + rows_per_store, nseq):
                t_i = jnp.tanh(gx_scr[rr:rr + nseq, cols] + bgx_n)
                t_r = jnp.tanh(ga_scr[rr:rr + nseq, cols] + bga_n)
                log2_a = t_r * kk_n + kk_n
                a = jnp.exp2(log2_a)
                y = jnp.tanh(log2_a * (-LN_2)) * (a * a + 1.0)
                mult = jnp.where(y > 0.0, y * lax.rsqrt(y), 0.0)
                u = mult * xc_scr[rr:rr + nseq, cols]
                h = a * h + (u * t_i + u)
                hy.append(h * _silu_of_half(gate_scr[rr:rr + nseq, cols]))
            hy_scr[r:r + rows_per_store, cols] = jnp.concatenate(hy, axis=0).astype(BF16)
        h_scr[:, cols] = h
        h_out_ref[:, cols] = h

    def out_proj_first_part():
        for nn in range(D_MODEL // MXU_N):
            cols = slice(nn * MXU_N, (nn + 1) * MXU_N)
            x1_ref[:, cols] = x_ref[:, cols] + jnp.dot(
                hy_scr[:, :OUT_K_SPLIT], w_out_ref[:OUT_K_SPLIT, cols],
                preferred_element_type=F32)

    in_proj_x(0)
    in_proj_gate(0)
    in_proj_x(1)
    conv_and_gates(0)
    in_proj_gate(1)
    for n in range(N_GATE_BLOCKS):
        if n + 2 < N_GATE_BLOCKS:
            in_proj_x(n + 2)
        if n + 1 < N_GATE_BLOCKS:
            conv_and_gates(n + 1)
        if n + 2 < N_GATE_BLOCKS:
            in_proj_gate(n + 2)
        if (n + 1) * GATE_BLOCK == D_RNN - GATE_BLOCK:
            out_proj_first_part()
        scan(n)

    half_rows = a_rows // 2
    lo = lax.broadcasted_iota(jnp.int32, (half_rows, KV_W), 1) < HEAD_DIM
    for rh in range(2):
        rows = slice(rh * half_rows, (rh + 1) * half_rows)
        for n in range(D_MODEL // MXU_N):
            cols = slice(n * MXU_N, (n + 1) * MXU_N)
            x1_ref[rows, cols] = x1_ref[rows, cols] + jnp.dot(
                hy_scr[rows, OUT_K_SPLIT:], w_out_ref[OUT_K_SPLIT:, cols],
                preferred_element_type=F32)
        _norm_rows_to(x1_ref, pa_row(PA_NORM_KV), xn_scr, half_rows, 64, r0=rh * half_rows)
        kv = jnp.dot(xn_scr[rows, :], w_kv_ref[...], preferred_element_type=F32)
        k = kv[:, :KV_W]
        k2 = k * k
        ms_lo = jnp.sum(jnp.where(lo, k2, 0.0), axis=-1, keepdims=True) * (1.0 / HEAD_DIM)
        ms_hi = jnp.sum(jnp.where(lo, 0.0, k2), axis=-1, keepdims=True) * (1.0 / HEAD_DIM)
        rs = jnp.where(lo, lax.rsqrt(ms_lo + EPS), lax.rsqrt(ms_hi + EPS))
        k_ref[rows, :] = k * rs * pa_ref[PA_K_NORM:PA_K_NORM + 1, 0:KV_W]
        v_ref[rows, :] = kv[:, KV_W:]

    for c in out_copies(i, slot):
        c.start()

    @pl.when(i == total - 1)
    def _():
        @pl.when(i >= 1)
        def _():
            for c in out_copies(i - 1, 1 - slot):
                c.wait()
        for c in out_copies(i, slot):
            c.wait()


def _layer_a(x, conv_in, h_in, p, name):
    n_seq, t_len, _ = x.shape
    a_steps = min(t_len, A_MAX_STEPS)
    nseq = min(n_seq, max(SUBLANES, A_TILE_ROWS // a_steps), BF16_SUBLANES)
    a_rows = a_steps * nseq
    conv_pre = (CONV_W - 1) * nseq
    assert nseq % SUBLANES == 0 and n_seq % nseq == 0 and t_len % a_steps == 0
    grid = (n_seq // nseq, t_len // a_steps)
    const2 = lambda g, t: (0, 0)
    const3 = lambda g, t: (0, 0, 0)
    any_spec = pl.BlockSpec(memory_space=pl.ANY)

    weights = [p["pa"], p["w_in_a"], p["w_gate"], p["w_out_a"], p["w_kv"]]
    w_specs = [pl.BlockSpec(w.shape, const3 if w.ndim == 3 else const2,
                            pipeline_mode=pl.Buffered(1)) for w in weights]
    conv_spec = pl.BlockSpec((nseq, CONV_W - 1, D_RNN), lambda g, t: (g, 0, 0))
    h_spec = pl.BlockSpec((nseq, D_RNN), lambda g, t: (g, 0))

    out_shape = [
        jax.ShapeDtypeStruct((n_seq, t_len, D_MODEL), F32),
        jax.ShapeDtypeStruct((n_seq, t_len, KV_W), F32),
        jax.ShapeDtypeStruct((n_seq, t_len, KV_W), F32),
        jax.ShapeDtypeStruct((n_seq, CONV_W - 1, D_RNN), F32),
        jax.ShapeDtypeStruct((n_seq, D_RNN), F32),
    ]
    scratch = [
        pltpu.VMEM((2, a_steps, nseq, D_MODEL), F32),
        pltpu.VMEM((2, a_steps, nseq, D_MODEL), F32),
        pltpu.VMEM((2, a_steps, nseq, KV_W), F32),
        pltpu.VMEM((2, a_steps, nseq, KV_W), F32),
        pltpu.SemaphoreType.DMA((2,)),
        pltpu.SemaphoreType.DMA((2,)),
        pltpu.VMEM((a_rows, D_MODEL), BF16),
        pltpu.VMEM((conv_pre + a_rows, D_RNN), F32),
        pltpu.VMEM((a_rows, D_RNN), F32),
        pltpu.VMEM((a_rows, D_RNN), F32),
        pltpu.VMEM((a_rows, D_RNN), BF16),
        pltpu.VMEM((a_rows, D_RNN), F32),
        pltpu.VMEM((a_rows, D_RNN), F32),
        pltpu.VMEM((a_rows, D_RNN), BF16),
        pltpu.VMEM((nseq, D_RNN), F32),
    ]
    return pl.pallas_call(
        functools.partial(_layer_a_kernel, a_steps=a_steps, nseq=nseq),
        grid=grid,
        in_specs=[any_spec, conv_spec, h_spec] + w_specs,
        out_specs=[any_spec, any_spec, any_spec, conv_spec, h_spec],
        out_shape=out_shape,
        scratch_shapes=scratch,
        compiler_params=pltpu.CompilerParams(
            dimension_semantics=("arbitrary", "arbitrary"), vmem_limit_bytes=VMEM_LIMIT_BYTES),
        name=name,
    )(x, conv_in, h_in, *weights)


def _layer_b_kernel(x1_ref, kh_ref, kc_ref, vh_ref, vc_ref, pb_ref, w_in_ref,
                    sinks_ref, w_out_ref, y_ref, kt_ref, vt_ref,
                    xn_scr, gate_scr, qh_scr, kmat_scr, vmat_scr, o_scr, og_scr,
                    *, nseg, seg, mask_first_halo):
    m = nseg * seg
    n_chunks = seg // CHUNK
    part = min(m, B_PART_ROWS)
    assert m % part == 0 and (part % seg == 0 or seg % part == 0)
    n_parts = m // part
    n_col_blocks = D_MODEL // MXU_N

    _norm_rows_to(x1_ref, pb_ref[PB_NORM_B:PB_NORM_B + 1, :], xn_scr, m, 128)

    lo_m = lax.broadcasted_iota(jnp.int32, (part, LANES), 1) < HEAD_DIM
    qg = pb_ref[PB_Q_GAIN:PB_Q_GAIN + 1, :]

    def q_block(h, n):
        rows = slice(h * part, (h + 1) * part)
        qn = jnp.dot(xn_scr[rows, :], w_in_ref[:, n * MXU_N:(n + 1) * MXU_N],
                     preferred_element_type=F32)
        for cc in range(MXU_N // LANES):
            col = n * MXU_N + cc * LANES
            x = qn[:, cc * LANES:(cc + 1) * LANES]
            sq = x * x
            ms_lo = jnp.sum(jnp.where(lo_m, sq, 0.0), axis=-1, keepdims=True) * (1.0 / HEAD_DIM)
            ms_hi = jnp.sum(jnp.where(lo_m, 0.0, sq), axis=-1, keepdims=True) * (1.0 / HEAD_DIM)
            rs = jnp.where(lo_m, lax.rsqrt(ms_lo + EPS), lax.rsqrt(ms_hi + EPS))
            qh_scr[rows, col:col + LANES] = (x * rs * qg[:, col:col + LANES]).astype(BF16)

    row_split = 2 if part >= 2 * MXU_N else 1

    def gate_piece(h, n, rh):
        r0 = h * part + rh * (part // row_split)
        rows = slice(r0, r0 + part // row_split)
        gate_scr[rows, n * MXU_N:(n + 1) * MXU_N] = jnp.dot(
            xn_scr[rows, :], w_in_ref[:, D_MODEL + n * MXU_N:D_MODEL + (n + 1) * MXU_N],
            preferred_element_type=F32)

    def gate_multiply(h):
        blk = 32
        for r in range(h * part, (h + 1) * part, blk):
            og_scr[r:r + blk, :] = (
                o_scr[r:r + blk, :] * _silu_of_half(gate_scr[r:r + blk, :])).astype(BF16)

    def out_piece(h, n):
        rows = slice(h * part, (h + 1) * part)
        cols = slice(n * MXU_N, (n + 1) * MXU_N)
        y_ref[rows, cols] = x1_ref[rows, cols] + jnp.dot(
            og_scr[rows, :], w_out_ref[:, cols], preferred_element_type=F32)

    lane = lax.broadcasted_iota(jnp.int32, (CHUNK, KV_W), 1)
    lo = lane < HEAD_DIM
    first = pl.program_id(1) == 0 if mask_first_halo else None

    for s in range(nseg):
        for r0, nrows, kval, vval in ((0, WINDOW, kh_ref[s], vh_ref[s]),
                                      (WINDOW, seg, kc_ref[s * seg:(s + 1) * seg, :],
                                       vc_ref[s * seg:(s + 1) * seg, :])):
            is_lo = lax.broadcasted_iota(jnp.int32, kval.shape, 1) < HEAD_DIM
            ind_e = jnp.where(is_lo, 1.0, 0.0).astype(BF16)
            ind_o = jnp.where(is_lo, 0.0, 1.0).astype(BF16)
            rows = slice(r0, r0 + nrows)
            for val, scr in ((kval, kmat_scr), (vval, vmat_scr)):
                swapped = pltpu.roll(val, HEAD_DIM, 1)
                for g in range(N_KV):
                    low = val if g == 0 else swapped
                    high = swapped if g == 0 else val
                    scr[s, g, 0, rows, 0:KV_W] = jnp.where(is_lo, low, 0.0).astype(BF16)
                    scr[s, g, 1, rows, 0:KV_W] = jnp.where(is_lo, 0.0, high).astype(BF16)
            for g in range(N_KV):
                vmat_scr[s, g, 0, rows, KV_W:] = ind_e
                vmat_scr[s, g, 1, rows, KV_W:] = ind_o

    def placed(scr, s, g, k0):
        return jnp.concatenate(
            [scr[s, g, 0, k0:k0 + WINDOW, :], scr[s, g, 1, k0:k0 + WINDOW, :],
             scr[s, g, 0, k0 + WINDOW:k0 + WINDOW + CHUNK, :],
             scr[s, g, 1, k0 + WINDOW:k0 + WINDOW + CHUNK, :]], axis=0)

    def scores(s, c, g):
        q0 = s * seg + c * CHUNK
        qst = jnp.concatenate(
            [qh_scr[q0:q0 + CHUNK, g * GROUP * HEAD_DIM + j * PAIR_W:
                    g * GROUP * HEAD_DIM + (j + 1) * PAIR_W] for j in range(N_PAIRS)],
            axis=0)
        return lax.dot_general(qst, placed(kmat_scr, s, g, c * CHUNK),
                               (((1,), (1,)), ((), ())),
                               preferred_element_type=F32)

    def softmax_pv(sc, s, c, g):
        p_rows, sink_rows = [], []
        for j in range(N_PAIRS):
            sj = sc[j * CHUNK:(j + 1) * CHUNK]
            c0 = sj[:, 0:KV_W]
            c1 = sj[:, KV_W:2 * KV_W]
            c2 = sj[:, 2 * KV_W:]
            if mask_first_halo and c * CHUNK < WINDOW:
                n_bad = jnp.where(first, WINDOW - c * CHUNK, 0)
                bad = lane < n_bad
                c0 = jnp.where(bad, NEG, c0)
                c1 = jnp.where(bad, NEG, c1)
            sink_e = sinks_ref[g * GROUP + 2 * j] * LOG2_E
            sink_o = sinks_ref[g * GROUP + 2 * j + 1] * LOG2_E
            m_e = jnp.maximum(jnp.max(jnp.maximum(c0, jnp.where(lo, c2, NEG)),
                                      axis=-1, keepdims=True), sink_e)
            m_o = jnp.maximum(jnp.max(jnp.maximum(c1, jnp.where(lo, NEG, c2)),
                                      axis=-1, keepdims=True), sink_o)
            p_rows.append(jnp.concatenate(
                [jnp.exp2(c0 - m_e), jnp.exp2(c1 - m_o),
                 jnp.exp2(c2 - jnp.where(lo, m_e, m_o))], axis=1).astype(BF16))
            sink_rows.append(jnp.where(lo, jnp.exp2(sink_e - m_e), jnp.exp2(sink_o - m_o)))
        pmat = jnp.concatenate(p_rows, axis=0)
        ov = jnp.dot(pmat, placed(vmat_scr, s, g, c * CHUNK),
                     preferred_element_type=F32)
        q0 = s * seg + c * CHUNK
        for j in range(N_PAIRS):
            rows = slice(j * CHUNK, (j + 1) * CHUNK)
            c_lo = g * GROUP * HEAD_DIM + j * PAIR_W
            o_scr[q0:q0 + CHUNK, c_lo:c_lo + PAIR_W] = (
                ov[rows, :KV_W] / (ov[rows, KV_W:] + sink_rows[j]))

    all_units = [(s, c, g) for s in range(nseg) for c in range(n_chunks) for g in range(N_KV)]

    for n in range(n_col_blocks):
        q_block(0, n)
    for h in range(n_parts):
        units = [u for u in all_units if (u[0] * seg + u[1] * CHUNK) // part == h]
        fillers = []
        if h >= 1:
            fillers += [functools.partial(out_piece, h - 1, n) for n in range(n_col_blocks)]
        gates = [functools.partial(gate_piece, h, n, rh)
                 for n in range(n_col_blocks) for rh in range(row_split)]
        nxt = ([functools.partial(q_block, h + 1, n) for n in range(n_col_blocks)]
               if h + 1 < n_parts else [])
        while gates or nxt:
            fillers += gates[:2]
            gates = gates[2:]
            fillers += nxt[:1]
            nxt = nxt[1:]
        fillers.pop(0)()
        per_unit = -(-len(fillers) // len(units))
        pending = [scores(*u) for u in units[:LOOKAHEAD]]
        for k, (s, c, g) in enumerate(units):
            sc = pending.pop(0)
            if k + LOOKAHEAD < len(units):
                pending.append(scores(*units[k + LOOKAHEAD]))
            for _ in range(min(per_unit, len(fillers))):
                fillers.pop(0)()
            softmax_pv(sc, s, c, g)
        assert not fillers
        gate_multiply(h)
    for n in range(n_col_blocks):
        out_piece(n_parts - 1, n)

    def write_tails():
        for s in range(nseg):
            for h_ref, c_ref, t_ref in ((kh_ref, kc_ref, kt_ref), (vh_ref, vc_ref, vt_ref)):
                if seg >= WINDOW:
                    t_ref[s] = c_ref[(s + 1) * seg - WINDOW:(s + 1) * seg, :]
                else:
                    t_ref[s, 0:WINDOW - seg, :] = h_ref[s, seg:, :]
                    t_ref[s, WINDOW - seg:, :] = c_ref[s * seg:(s + 1) * seg, :]

    if mask_first_halo:
        pl.when(pl.program_id(1) == pl.num_programs(1) - 1)(write_tails)
    else:
        write_tails()


def _layer_b(x1, k_halo, v_halo, k_cur, v_cur, p, *, n_batch, t_len, nseg, seg, prompt):
    m = nseg * seg
    rows = n_batch * t_len
    const2 = lambda *_: (0, 0)
    if prompt:
        steps = t_len // m
        per_b = t_len // WINDOW
        grid = (n_batch, steps)
        row_map = lambda b, t: (b * steps + t, 0)
        halo_map = lambda b, t: (b * per_b + jnp.maximum(t * (m // WINDOW) - 1, 0), 0, 0)
    else:
        grid = (n_batch // nseg, 1)
        row_map = lambda i, t: (i, 0)
        halo_map = lambda i, t: (i, 0, 0)

    weights = [p["pb"], p["w_in_b"]]
    in_specs = [
        pl.BlockSpec((m, D_MODEL), row_map),
        pl.BlockSpec((nseg, WINDOW, KV_W), halo_map),
        pl.BlockSpec((m, KV_W), row_map),
        pl.BlockSpec((nseg, WINDOW, KV_W), halo_map),
        pl.BlockSpec((m, KV_W), row_map),
    ] + [pl.BlockSpec(w.shape, const2) for w in weights] + [
        pl.BlockSpec(memory_space=pltpu.SMEM),
        pl.BlockSpec(p["w_out_b"].shape, const2),
    ]
    tail_spec = pl.BlockSpec((nseg, WINDOW, KV_W), lambda a, t: (a, 0, 0))
    tail_shape = jax.ShapeDtypeStruct((n_batch, WINDOW, KV_W), F32)
    kv_rows = WINDOW + seg
    scratch = [
        pltpu.VMEM((m, D_MODEL), BF16),
        pltpu.VMEM((m, D_MODEL), F32),
        pltpu.VMEM((m, D_MODEL), BF16),
        pltpu.VMEM((nseg, N_KV, 2, kv_rows, KV_W), BF16),
        pltpu.VMEM((nseg, N_KV, 2, kv_rows, 2 * KV_W), BF16),
        pltpu.VMEM((m, D_MODEL), F32),
        pltpu.VMEM((m, D_MODEL), BF16),
    ]
    return pl.pallas_call(
        functools.partial(_layer_b_kernel, nseg=nseg, seg=seg, mask_first_halo=prompt),
        grid=grid,
        in_specs=in_specs,
        out_specs=[pl.BlockSpec((m, D_MODEL), row_map), tail_spec, tail_spec],
        out_shape=[jax.ShapeDtypeStruct((rows, D_MODEL), F32), tail_shape, tail_shape],
        scratch_shapes=scratch,
        compiler_params=pltpu.CompilerParams(
            dimension_semantics=("arbitrary", "arbitrary"), vmem_limit_bytes=VMEM_LIMIT_BYTES),
        name="layer_b_prompt" if prompt else "layer_b_sample",
    )(x1, k_halo, k_cur, v_halo, v_cur, *weights, p["sinks"], p["w_out_b"])


def _prep_params(norm_a, w_in_a, conv_w, conv_b, w_gate_x, b_gate_x, w_gate_a, b_gate_a,
                 lru_lambda, w_out_a, norm_kv, w_kv, k_norm, norm_b, w_in_b, q_norm, sinks,
                 w_out_b):
    row = lambda v: v.reshape(1, -1).astype(F32)
    in_scale = jnp.concatenate([jnp.ones((D_MODEL,), F32), jnp.full((D_MODEL,), 0.5, F32)])
    k_norm_row = jnp.pad(row(jnp.tile(k_norm, N_KV)), ((0, 0), (0, D_MODEL - KV_W)))
    pa_rows = {PA_NORM_A: row(norm_a[0]), PA_CONV_B: row(0.5 * conv_b[0]),
               PA_B_GX: row(0.5 * b_gate_x[0]), PA_B_GA: row(0.5 * b_gate_a[0]),
               PA_LAMBDA: row(lru_lambda[0]), PA_NORM_KV: row(norm_kv), PA_K_NORM: k_norm_row,
               PA_CONV_W: (0.5 * conv_w[0]).astype(F32)}
    zero_row = jnp.zeros((1, D_MODEL), F32)
    pa, r = [], 0
    while r < PA_ROWS:
        piece = pa_rows.get(r, zero_row)
        pa.append(piece)
        r += piece.shape[0]
    qg = row(jnp.tile(q_norm[0], N_HEADS) * (HEAD_DIM ** -0.5 * LOG2_E))
    pb = [row(norm_b[0]), qg] + [zero_row] * (PB_ROWS - 2)
    return {
        "pa": jnp.concatenate(pa, axis=0),
        "w_in_a": (w_in_a[0] * in_scale).astype(BF16),
        "w_gate": jnp.concatenate([w_gate_x[0], w_gate_a[0]], axis=-1).astype(BF16),
        "w_out_a": w_out_a[0].astype(BF16),
        "w_kv": w_kv.astype(BF16),
        "pb": jnp.concatenate(pb, axis=0),
        "w_in_b": (w_in_b[0] * in_scale).astype(BF16),
        "sinks": sinks[0].astype(F32),
        "w_out_b": w_out_b[0].astype(BF16),
    }


def kernel(x_prompt, x_sample, state_conv, state_rglru, cache_k_win, cache_v_win, norm_a, w_in_a, conv_w, conv_b, w_gate_x, b_gate_x, w_gate_a, b_gate_a, lru_lambda, w_out_a, norm_kv, w_kv, k_norm, norm_b, w_in_b, q_norm, sinks, w_out_b):
    assert norm_a.shape[0] == 1 and norm_b.shape[0] == 1, "one recurrent + one attention layer"
    p = _prep_params(norm_a, w_in_a, conv_w, conv_b, w_gate_x, b_gate_x, w_gate_a, b_gate_a,
                     lru_lambda, w_out_a, norm_kv, w_kv, k_norm, norm_b, w_in_b, q_norm, sinks,
                     w_out_b)
    bp, tp, _ = x_prompt.shape
    bs, ts, _ = x_sample.shape

    x1p, kp, vp, conv_p, h_p = _layer_a(
        x_prompt, jnp.zeros((bp, CONV_W - 1, D_RNN), F32), jnp.zeros((bp, D_RNN), F32), p,
        "layer_a_prompt")
    tile_b = min(tp, B_TILE_ROWS)
    kp2 = kp.reshape(bp * tp, KV_W)
    vp2 = vp.reshape(bp * tp, KV_W)
    kp3 = kp.reshape(bp * tp // WINDOW, WINDOW, KV_W)
    vp3 = vp.reshape(bp * tp // WINDOW, WINDOW, KV_W)
    y_p, kt_p, vt_p = _layer_b(x1p.reshape(bp * tp, D_MODEL), kp3, vp3, kp2, vp2, p,
                               n_batch=bp, t_len=tp, nseg=1, seg=tile_b, prompt=True)

    x1s, ks, vs, conv_s, h_s = _layer_a(
        x_sample, state_conv[0], state_rglru[0], p, "layer_a_sample")
    y_s, kt_s, vt_s = _layer_b(
        x1s.reshape(bs * ts, D_MODEL), cache_k_win.reshape(bs, WINDOW, KV_W),
        cache_v_win.reshape(bs, WINDOW, KV_W), ks.reshape(bs * ts, KV_W),
        vs.reshape(bs * ts, KV_W), p, n_batch=bs, t_len=ts,
        nseg=min(bs, B_TILE_ROWS // ts), seg=ts, prompt=False)

    heads = lambda a: a.reshape(a.shape[0], WINDOW, N_KV, HEAD_DIM)
    return (y_p.reshape(bp, tp, D_MODEL), y_s.reshape(bs, ts, D_MODEL),
            conv_p[None], h_p[None], heads(kt_p), heads(vt_p),
            conv_s[None], h_s[None], heads(kt_s), heads(vt_s))
```

```python
import functools

import jax
import jax.numpy as jnp
from jax import lax
from jax.experimental import pallas as pl
from jax.experimental.pallas import tpu as pltpu

D_MODEL = 1024
D_RNN = 1024
GATE_BLOCK = 256
N_GATE_BLOCKS = D_RNN // GATE_BLOCK
CONV_W = 4
LRU_C = 8.0
HEAD_DIM = 64
N_HEADS = 16
N_KV = 2
GROUP = N_HEADS // N_KV
CHUNK = 64
WINDOW = 128
EPS = 1e-6
NEG = -1e30
LOG2_E = 1.4426950408889634
LN_2 = 0.6931471805599453

SUBLANES = 8
LANES = 128
BF16_SUBLANES = 16
MXU_N = 256
VMEM_LIMIT_BYTES = 56 * 1024 * 1024

F32 = jnp.float32
BF16 = jnp.bfloat16

KV_W = N_KV * HEAD_DIM
PAIR_W = 2 * HEAD_DIM
N_PAIRS = GROUP // 2
LOOKAHEAD = 2
B_TILE_ROWS = 1024
B_PART_ROWS = 256

PA_NORM_A, PA_CONV_B, PA_B_GX, PA_B_GA, PA_LAMBDA, PA_NORM_KV, PA_K_NORM = range(7)
PA_CONV_W = SUBLANES
PA_ROWS = 2 * SUBLANES
PB_NORM_B, PB_Q_GAIN = range(2)
PB_ROWS = SUBLANES

A_MAX_STEPS = 128
A_TILE_ROWS = 1024
OUT_K_SPLIT = D_RNN // 2


def _silu_of_half(h):
    return h * jnp.tanh(h) + h


def _norm_rows_to(x_ref, g, dst_ref, n_rows, blk, r0=0):
    for r in range(r0, r0 + n_rows, blk):
        x = x_ref[r:r + blk, :]
        rs = lax.rsqrt(jnp.mean(x * x, axis=-1, keepdims=True) + EPS)
        dst_ref[r:r + blk, :] = (x_ref[r:r + blk, :] * rs * g).astype(BF16)


def _layer_a_kernel(x_hbm, conv_in_ref, h_in_ref,
                    pa_ref, w_in_ref, w_gate_ref, w_out_ref, w_kv_ref,
                    x1_hbm, k_hbm, v_hbm, conv_out_ref, h_out_ref,
                    xin, x1o, ko, vo, sem_in, sem_out,
                    xn_scr, xbp_scr, gate_scr, xc_scr, xcb_scr, gx_scr, ga_scr, hy_scr, h_scr,
                    *, a_steps, nseq):
    a_rows = a_steps * nseq
    conv_pre = (CONV_W - 1) * nseq
    g = pl.program_id(0)
    t = pl.program_id(1)
    nt = pl.num_programs(1)
    total = pl.num_programs(0) * nt
    i = g * nt + t
    slot = lax.rem(i, 2)

    def in_copies(step, sl):
        gg = step // nt
        tt = lax.rem(step, nt)
        return [pltpu.make_async_copy(x_hbm.at[gg * nseq + j, pl.ds(tt * a_steps, a_steps), :],
                                      xin.at[sl, :, j, :], sem_in.at[sl]) for j in range(nseq)]

    def out_copies(step, sl):
        gg = step // nt
        tt = lax.rem(step, nt)
        cps = []
        for buf, dst in ((x1o, x1_hbm), (ko, k_hbm), (vo, v_hbm)):
            cps += [pltpu.make_async_copy(buf.at[sl, :, j, :],
                                          dst.at[gg * nseq + j, pl.ds(tt * a_steps, a_steps), :],
                                          sem_out.at[sl]) for j in range(nseq)]
        return cps

    @pl.when(i == 0)
    def _():
        for n, c in enumerate(in_copies(i, slot)):
            c.start(priority=n % 2)

    @pl.when(i + 1 < total)
    def _():
        for n, c in enumerate(in_copies(i + 1, 1 - slot)):
            c.start(priority=n % 2)

    @pl.when(t == 0)
    def _():
        xbp_scr[0:conv_pre, :] = conv_in_ref[...].reshape(conv_pre, D_RNN)
        h_scr[...] = h_in_ref[...]

    @pl.when(i >= 2)
    def _():
        for c in out_copies(i - 2, slot):
            c.wait()

    for c in in_copies(i, slot):
        c.wait()

    x_ref = xin.at[slot].reshape(a_rows, D_MODEL)
    x1_ref = x1o.at[slot].reshape(a_rows, D_MODEL)
    k_ref = ko.at[slot].reshape(a_rows, KV_W)
    v_ref = vo.at[slot].reshape(a_rows, KV_W)

    def pa_row(r, n=1):
        return pa_ref[r:r + n, :]

    _norm_rows_to(x_ref, pa_row(PA_NORM_A), xn_scr, a_rows, 128)
    lam = pa_row(PA_LAMBDA)
    log_sig_lam = jnp.minimum(lam, 0.0) - jnp.log1p(jnp.exp(-jnp.abs(lam)))
    kk = jnp.broadcast_to((0.5 * LRU_C * LOG2_E) * log_sig_lam, (nseq, D_RNN))
    b_gx = jnp.broadcast_to(pa_row(PA_B_GX), (nseq, D_RNN))
    b_ga = jnp.broadcast_to(pa_row(PA_B_GA), (nseq, D_RNN))

    cw = pa_row(PA_CONV_W, CONV_W)
    cb = pa_row(PA_CONV_B)
    conv_blk = 32

    def in_proj_x(n):
        cols = slice(n * GATE_BLOCK, (n + 1) * GATE_BLOCK)
        xbp_scr[conv_pre:, cols] = jnp.dot(xn_scr[...], w_in_ref[:, cols],
                                           preferred_element_type=F32)

    def in_proj_gate(n):
        cols = slice(n * GATE_BLOCK, (n + 1) * GATE_BLOCK)
        gate_scr[:, cols] = jnp.dot(
            xn_scr[...], w_in_ref[:, D_RNN + n * GATE_BLOCK:D_RNN + (n + 1) * GATE_BLOCK],
            preferred_element_type=F32)

    def conv_and_gates(n):
        cols = slice(n * GATE_BLOCK, (n + 1) * GATE_BLOCK)
        for r in range(0, a_rows, conv_blk):
            acc = cb[:, cols]
            for j in range(CONV_W):
                acc = acc + cw[j:j + 1, cols] * xbp_scr[r + j * nseq:r + j * nseq + conv_blk, cols]
            xc_scr[r:r + conv_blk, cols] = acc
            xcb_scr[r:r + conv_blk, cols] = acc.astype(BF16)
        tail = xbp_scr[a_rows:a_rows + conv_pre, cols]
        conv_out_ref[:, :, cols] = tail.reshape(CONV_W - 1, nseq, GATE_BLOCK)
        xbp_scr[0:conv_pre, cols] = tail

        res = jnp.dot(xcb_scr[:, cols], w_gate_ref[n], preferred_element_type=F32)
        gx_scr[:, cols] = res[:, :GATE_BLOCK]
        ga_scr[:, cols] = res[:, GATE_BLOCK:]

    def scan(n):
        cols = slice(n * GATE_BLOCK, (n + 1) * GATE_BLOCK)
        kk_n = kk[:, cols]
        bgx_n = b_gx[:, cols]
        bga_n = b_ga[:, cols]
        h = h_scr[:, cols]
        rows_per_store = max(BF16_SUBLANES, nseq)
        for r in range(0, a_rows, rows_per_store):
            hy = []
            for rr in range(r, r + rows_per_store, nseq):
                t_i = jnp.tanh(gx_scr[rr:rr + nseq, cols] + bgx_n)
                t_r = jnp.tanh(ga_scr[rr:rr + nseq, cols] + bga_n)
                log2_a = t_r * kk_n + kk_n
                a = jnp.exp2(log2_a)
                y = jnp.tanh(log2_a * (-LN_2)) * (a * a + 1.0)
                mult = jnp.where(y > 0.0, y * lax.rsqrt(y), 0.0)
                u = mult * xc_scr[rr:rr + nseq, cols]
                h = a * h + (u * t_i + u)
                hy.append(h * _silu_of_half(gate_scr[rr:rr + nseq, cols]))
            hy_scr[r:r + rows_per_store, cols] = jnp.concatenate(hy, axis=0).astype(BF16)
        h_scr[:, cols] = h
        h_out_ref[:, cols] = h

    def out_proj_first_part():
        for nn in range(D_MODEL // MXU_N):
            cols = slice(nn * MXU_N, (nn + 1) * MXU_N)
            x1_ref[:, cols] = x_ref[:, cols] + jnp.dot(
                hy_scr[:, :OUT_K_SPLIT], w_out_ref[:OUT_K_SPLIT, cols],
                preferred_element_type=F32)

    in_proj_x(0)
    in_proj_gate(0)
    in_proj_x(1)
    conv_and_gates(0)
    in_proj_gate(1)
    for n in range(N_GATE_BLOCKS):
        if n + 2 < N_GATE_BLOCKS:
            in_proj_x(n + 2)
        if n + 1 < N_GATE_BLOCKS:
            conv_and_gates(n + 1)
        if n + 2 < N_GATE_BLOCKS:
            in_proj_gate(n + 2)
        if (n + 1) * GATE_BLOCK == D_RNN - GATE_BLOCK:
            out_proj_first_part()
        scan(n)

    half_rows = a_rows // 2
    lo = lax.broadcasted_iota(jnp.int32, (half_rows, KV_W), 1) < HEAD_DIM
    for rh in range(2):
        rows = slice(rh * half_rows, (rh + 1) * half_rows)
        for n in range(D_MODEL // MXU_N):
            cols = slice(n * MXU_N, (n + 1) * MXU_N)
            x1_ref[rows, cols] = x1_ref[rows, cols] + jnp.dot(
                hy_scr[rows, OUT_K_SPLIT:], w_out_ref[OUT_K_SPLIT:, cols],
                preferred_element_type=F32)
        _norm_rows_to(x1_ref, pa_row(PA_NORM_KV), xn_scr, half_rows, 64, r0=rh * half_rows)
        kv = jnp.dot(xn_scr[rows, :], w_kv_ref[...], preferred_element_type=F32)
        k = kv[:, :KV_W]
        k2 = k * k
        ms_lo = jnp.sum(jnp.where(lo, k2, 0.0), axis=-1, keepdims=True) * (1.0 / HEAD_DIM)
        ms_hi = jnp.sum(jnp.where(lo, 0.0, k2), axis=-1, keepdims=True) * (1.0 / HEAD_DIM)
        rs = jnp.where(lo, lax.rsqrt(ms_lo + EPS), lax.rsqrt(ms_hi + EPS))
        k_ref[rows, :] = k * rs * pa_ref[PA_K_NORM:PA_K_NORM + 1, 0:KV_W]
        v_ref[rows, :] = kv[:, KV_W:]

    for n, c in enumerate(out_copies(i, slot)):
        c.start(priority=n % 2)

    @pl.when(i == total - 1)
    def _():
        @pl.when(i >= 1)
        def _():
            for c in out_copies(i - 1, 1 - slot):
                c.wait()
        for c in out_copies(i, slot):
            c.wait()


def _layer_a(x, conv_in, h_in, p, name):
    n_seq, t_len, _ = x.shape
    a_steps = min(t_len, A_MAX_STEPS)
    nseq = min(n_seq, max(SUBLANES, A_TILE_ROWS // a_steps), BF16_SUBLANES)
    a_rows = a_steps * nseq
    conv_pre = (CONV_W - 1) * nseq
    assert nseq % SUBLANES == 0 and n_seq % nseq == 0 and t_len % a_steps == 0
    grid = (n_seq // nseq, t_len // a_steps)
    const2 = lambda g, t: (0, 0)
    const3 = lambda g, t: (0, 0, 0)
    any_spec = pl.BlockSpec(memory_space=pl.ANY)

    weights = [p["pa"], p["w_in_a"], p["w_gate"], p["w_out_a"], p["w_kv"]]
    w_specs = [pl.BlockSpec(w.shape, const3 if w.ndim == 3 else const2,
                            pipeline_mode=pl.Buffered(1)) for w in weights]
    conv_spec = pl.BlockSpec((CONV_W - 1, nseq, D_RNN), lambda g, t: (0, g, 0))
    h_spec = pl.BlockSpec((nseq, D_RNN), lambda g, t: (g, 0))

    out_shape = [
        jax.ShapeDtypeStruct((n_seq, t_len, D_MODEL), F32),
        jax.ShapeDtypeStruct((n_seq, t_len, KV_W), F32),
        jax.ShapeDtypeStruct((n_seq, t_len, KV_W), F32),
        jax.ShapeDtypeStruct((CONV_W - 1, n_seq, D_RNN), F32),
        jax.ShapeDtypeStruct((n_seq, D_RNN), F32),
    ]
    scratch = [
        pltpu.VMEM((2, a_steps, nseq, D_MODEL), F32),
        pltpu.VMEM((2, a_steps, nseq, D_MODEL), F32),
        pltpu.VMEM((2, a_steps, nseq, KV_W), F32),
        pltpu.VMEM((2, a_steps, nseq, KV_W), F32),
        pltpu.SemaphoreType.DMA((2,)),
        pltpu.SemaphoreType.DMA((2,)),
        pltpu.VMEM((a_rows, D_MODEL), BF16),
        pltpu.VMEM((conv_pre + a_rows, D_RNN), F32),
        pltpu.VMEM((a_rows, D_RNN), F32),
        pltpu.VMEM((a_rows, D_RNN), F32),
        pltpu.VMEM((a_rows, D_RNN), BF16),
        pltpu.VMEM((a_rows, D_RNN), F32),
        pltpu.VMEM((a_rows, D_RNN), F32),
        pltpu.VMEM((a_rows, D_RNN), BF16),
        pltpu.VMEM((nseq, D_RNN), F32),
    ]
    return pl.pallas_call(
        functools.partial(_layer_a_kernel, a_steps=a_steps, nseq=nseq),
        grid=grid,
        in_specs=[any_spec, conv_spec, h_spec] + w_specs,
        out_specs=[any_spec, any_spec, any_spec, conv_spec, h_spec],
        out_shape=out_shape,
        scratch_shapes=scratch,
        compiler_params=pltpu.CompilerParams(
            dimension_semantics=("arbitrary", "arbitrary"), vmem_limit_bytes=VMEM_LIMIT_BYTES),
        name=name,
    )(x, conv_in, h_in, *weights)


def _layer_b_kernel(x1_ref, kh_ref, kc_ref, vh_ref, vc_ref, pb_ref, w_in_ref,
                    sinks_ref, w_out_ref, y_ref, kt_ref, vt_ref,
                    xn_scr, gate_scr, qh_scr, kmat_scr, vmat_scr, o_scr, og_scr,
                    *, nseg, seg, mask_first_halo):
    m = nseg * seg
    n_chunks = seg // CHUNK
    part = min(m, B_PART_ROWS)
    assert m % part == 0 and (part % seg == 0 or seg % part == 0)
    n_parts = m // part
    n_col_blocks = D_MODEL // MXU_N

    _norm_rows_to(x1_ref, pb_ref[PB_NORM_B:PB_NORM_B + 1, :], xn_scr, m, 128)

    lo_m = lax.broadcasted_iota(jnp.int32, (part, LANES), 1) < HEAD_DIM
    qg = pb_ref[PB_Q_GAIN:PB_Q_GAIN + 1, :]

    def q_block(h, n):
        rows = slice(h * part, (h + 1) * part)
        qn = jnp.dot(xn_scr[rows, :], w_in_ref[:, n * MXU_N:(n + 1) * MXU_N],
                     preferred_element_type=F32)
        for cc in range(MXU_N // LANES):
            col = n * MXU_N + cc * LANES
            x = qn[:, cc * LANES:(cc + 1) * LANES]
            sq = x * x
            ms_lo = jnp.sum(jnp.where(lo_m, sq, 0.0), axis=-1, keepdims=True) * (1.0 / HEAD_DIM)
            ms_hi = jnp.sum(jnp.where(lo_m, 0.0, sq), axis=-1, keepdims=True) * (1.0 / HEAD_DIM)
            rs = jnp.where(lo_m, lax.rsqrt(ms_lo + EPS), lax.rsqrt(ms_hi + EPS))
            qh_scr[rows, col:col + LANES] = (x * rs * qg[:, col:col + LANES]).astype(BF16)

    row_split = 2 if part >= 2 * MXU_N else 1

    def gate_piece(h, n, rh):
        r0 = h * part + rh * (part // row_split)
        rows = slice(r0, r0 + part // row_split)
        gate_scr[rows, n * MXU_N:(n + 1) * MXU_N] = jnp.dot(
            xn_scr[rows, :], w_in_ref[:, D_MODEL + n * MXU_N:D_MODEL + (n + 1) * MXU_N],
            preferred_element_type=F32)

    def gate_multiply(h):
        blk = 32
        for r in range(h * part, (h + 1) * part, blk):
            og_scr[r:r + blk, :] = (
                o_scr[r:r + blk, :] * _silu_of_half(gate_scr[r:r + blk, :])).astype(BF16)

    def out_piece(h, n):
        rows = slice(h * part, (h + 1) * part)
        cols = slice(n * MXU_N, (n + 1) * MXU_N)
        y_ref[rows, cols] = x1_ref[rows, cols] + jnp.dot(
            og_scr[rows, :], w_out_ref[:, cols], preferred_element_type=F32)

    lane = lax.broadcasted_iota(jnp.int32, (CHUNK, KV_W), 1)
    lo = lane < HEAD_DIM
    first = pl.program_id(1) == 0 if mask_first_halo else None

    for s in range(nseg):
        for r0, nrows, kval, vval in ((0, WINDOW, kh_ref[s], vh_ref[s]),
                                      (WINDOW, seg, kc_ref[s * seg:(s + 1) * seg, :],
                                       vc_ref[s * seg:(s + 1) * seg, :])):
            is_lo = lax.broadcasted_iota(jnp.int32, kval.shape, 1) < HEAD_DIM
            ind_e = jnp.where(is_lo, 1.0, 0.0).astype(BF16)
            ind_o = jnp.where(is_lo, 0.0, 1.0).astype(BF16)
            rows = slice(r0, r0 + nrows)
            for val, scr in ((kval, kmat_scr), (vval, vmat_scr)):
                swapped = pltpu.roll(val, HEAD_DIM, 1)
                for g in range(N_KV):
                    low = val if g == 0 else swapped
                    high = swapped if g == 0 else val
                    scr[s, g, 0, rows, 0:KV_W] = jnp.where(is_lo, low, 0.0).astype(BF16)
                    scr[s, g, 1, rows, 0:KV_W] = jnp.where(is_lo, 0.0, high).astype(BF16)
            for g in range(N_KV):
                vmat_scr[s, g, 0, rows, KV_W:] = ind_e
                vmat_scr[s, g, 1, rows, KV_W:] = ind_o

    def placed(scr, s, g, k0):
        return jnp.concatenate(
            [scr[s, g, 0, k0:k0 + WINDOW, :], scr[s, g, 1, k0:k0 + WINDOW, :],
             scr[s, g, 0, k0 + WINDOW:k0 + WINDOW + CHUNK, :],
             scr[s, g, 1, k0 + WINDOW:k0 + WINDOW + CHUNK, :]], axis=0)

    def scores(s, c, g):
        q0 = s * seg + c * CHUNK
        qst = jnp.concatenate(
            [qh_scr[q0:q0 + CHUNK, g * GROUP * HEAD_DIM + j * PAIR_W:
                    g * GROUP * HEAD_DIM + (j + 1) * PAIR_W] for j in range(N_PAIRS)],
            axis=0)
        return lax.dot_general(qst, placed(kmat_scr, s, g, c * CHUNK),
                               (((1,), (1,)), ((), ())),
                               preferred_element_type=F32)

    def softmax_pv(sc, s, c, g):
        p_rows, sink_rows = [], []
        for j in range(N_PAIRS):
            sj = sc[j * CHUNK:(j + 1) * CHUNK]
            c0 = sj[:, 0:KV_W]
            c1 = sj[:, KV_W:2 * KV_W]
            c2 = sj[:, 2 * KV_W:]
            if mask_first_halo and c * CHUNK < WINDOW:
                n_bad = jnp.where(first, WINDOW - c * CHUNK, 0)
                bad = lane < n_bad
                c0 = jnp.where(bad, NEG, c0)
                c1 = jnp.where(bad, NEG, c1)
            sink_e = sinks_ref[g * GROUP + 2 * j] * LOG2_E
            sink_o = sinks_ref[g * GROUP + 2 * j + 1] * LOG2_E
            m_e = jnp.maximum(jnp.max(jnp.maximum(c0, jnp.where(lo, c2, NEG)),
                                      axis=-1, keepdims=True), sink_e)
            m_o = jnp.maximum(jnp.max(jnp.maximum(c1, jnp.where(lo, NEG, c2)),
                                      axis=-1, keepdims=True), sink_o)
            p_rows.append(jnp.concatenate(
                [jnp.exp2(c0 - m_e), jnp.exp2(c1 - m_o),
                 jnp.exp2(c2 - jnp.where(lo, m_e, m_o))], axis=1).astype(BF16))
            sink_rows.append(jnp.where(lo, jnp.exp2(sink_e - m_e), jnp.exp2(sink_o - m_o)))
        pmat = jnp.concatenate(p_rows, axis=0)
        ov = jnp.dot(pmat, placed(vmat_scr, s, g, c * CHUNK),
                     preferred_element_type=F32)
        q0 = s * seg + c * CHUNK
        for j in range(N_PAIRS):
            rows = slice(j * CHUNK, (j + 1) * CHUNK)
            c_lo = g * GROUP * HEAD_DIM + j * PAIR_W
            o_scr[q0:q0 + CHUNK, c_lo:c_lo + PAIR_W] = (
                ov[rows, :KV_W] / (ov[rows, KV_W:] + sink_rows[j]))

    all_units = [(s, c, g) for s in range(nseg) for c in range(n_chunks) for g in range(N_KV)]

    for n in range(n_col_blocks):
        q_block(0, n)
    for h in range(n_parts):
        units = [u for u in all_units if (u[0] * seg + u[1] * CHUNK) // part == h]
        fillers = []
        if h >= 1:
            fillers += [functools.partial(out_piece, h - 1, n) for n in range(n_col_blocks)]
        gates = [functools.partial(gate_piece, h, n, rh)
                 for n in range(n_col_blocks) for rh in range(row_split)]
        nxt = ([functools.partial(q_block, h + 1, n) for n in range(n_col_blocks)]
               if h + 1 < n_parts else [])
        while gates or nxt:
            fillers += gates[:2]
            gates = gates[2:]
            fillers += nxt[:1]
            nxt = nxt[1:]
        fillers.pop(0)()
        per_unit = -(-len(fillers) // len(units))
        pending = [scores(*u) for u in units[:LOOKAHEAD]]
        for k, (s, c, g) in enumerate(units):
            sc = pending.pop(0)
            if k + LOOKAHEAD < len(units):
                pending.append(scores(*units[k + LOOKAHEAD]))
            for _ in range(min(per_unit, len(fillers))):
                fillers.pop(0)()
            softmax_pv(sc, s, c, g)
        assert not fillers
        gate_multiply(h)
    for n in range(n_col_blocks):
        out_piece(n_parts - 1, n)

    def write_tails():
        for s in range(nseg):
            for h_ref, c_ref, t_ref in ((kh_ref, kc_ref, kt_ref), (vh_ref, vc_ref, vt_ref)):
                if seg >= WINDOW:
                    t_ref[s] = c_ref[(s + 1) * seg - WINDOW:(s + 1) * seg, :]
                else:
                    t_ref[s, 0:WINDOW - seg, :] = h_ref[s, seg:, :]
                    t_ref[s, WINDOW - seg:, :] = c_ref[s * seg:(s + 1) * seg, :]

    if mask_first_halo:
        pl.when(pl.program_id(1) == pl.num_programs(1) - 1)(write_tails)
    else:
        write_tails()


def _layer_b(x1, k_halo, v_halo, k_cur, v_cur, p, *, n_batch, t_len, nseg, seg, prompt):
    m = nseg * seg
    rows = n_batch * t_len
    const2 = lambda *_: (0, 0)
    if prompt:
        steps = t_len // m
        per_b = t_len // WINDOW
        grid = (n_batch, steps)
        row_map = lambda b, t: (b * steps + t, 0)
        halo_map = lambda b, t: (b * per_b + jnp.maximum(t * (m // WINDOW) - 1, 0), 0, 0)
    else:
        grid = (n_batch // nseg, 1)
        row_map = lambda i, t: (i, 0)
        halo_map = lambda i, t: (i, 0, 0)

    weights = [p["pb"], p["w_in_b"]]
    in_specs = [
        pl.BlockSpec((m, D_MODEL), row_map),
        pl.BlockSpec((nseg, WINDOW, KV_W), halo_map),
        pl.BlockSpec((m, KV_W), row_map),
        pl.BlockSpec((nseg, WINDOW, KV_W), halo_map),
        pl.BlockSpec((m, KV_W), row_map),
    ] + [pl.BlockSpec(w.shape, const2) for w in weights] + [
        pl.BlockSpec(memory_space=pltpu.SMEM),
        pl.BlockSpec(p["w_out_b"].shape, const2),
    ]
    tail_spec = pl.BlockSpec((nseg, WINDOW, KV_W), lambda a, t: (a, 0, 0))
    tail_shape = jax.ShapeDtypeStruct((n_batch, WINDOW, KV_W), F32)
    kv_rows = WINDOW + seg
    scratch = [
        pltpu.VMEM((m, D_MODEL), BF16),
        pltpu.VMEM((m, D_MODEL), F32),
        pltpu.VMEM((m, D_MODEL), BF16),
        pltpu.VMEM((nseg, N_KV, 2, kv_rows, KV_W), BF16),
        pltpu.VMEM((nseg, N_KV, 2, kv_rows, 2 * KV_W), BF16),
        pltpu.VMEM((m, D_MODEL), F32),
        pltpu.VMEM((m, D_MODEL), BF16),
    ]
    return pl.pallas_call(
        functools.partial(_layer_b_kernel, nseg=nseg, seg=seg, mask_first_halo=prompt),
        grid=grid,
        in_specs=in_specs,
        out_specs=[pl.BlockSpec((m, D_MODEL), row_map), tail_spec, tail_spec],
        out_shape=[jax.ShapeDtypeStruct((rows, D_MODEL), F32), tail_shape, tail_shape],
        scratch_shapes=scratch,
        compiler_params=pltpu.CompilerParams(
            dimension_semantics=("arbitrary", "arbitrary"), vmem_limit_bytes=VMEM_LIMIT_BYTES),
        name="layer_b_prompt" if prompt else "layer_b_sample",
    )(x1, k_halo, k_cur, v_halo, v_cur, *weights, p["sinks"], p["w_out_b"])


def _prep_params(norm_a, w_in_a, conv_w, conv_b, w_gate_x, b_gate_x, w_gate_a, b_gate_a,
                 lru_lambda, w_out_a, norm_kv, w_kv, k_norm, norm_b, w_in_b, q_norm, sinks,
                 w_out_b):
    row = lambda v: v.reshape(1, -1).astype(F32)
    in_scale = jnp.concatenate([jnp.ones((D_MODEL,), F32), jnp.full((D_MODEL,), 0.5, F32)])
    k_norm_row = jnp.pad(row(jnp.tile(k_norm, N_KV)), ((0, 0), (0, D_MODEL - KV_W)))
    pa_rows = {PA_NORM_A: row(norm_a[0]), PA_CONV_B: row(0.5 * conv_b[0]),
               PA_B_GX: row(0.5 * b_gate_x[0]), PA_B_GA: row(0.5 * b_gate_a[0]),
               PA_LAMBDA: row(lru_lambda[0]), PA_NORM_KV: row(norm_kv), PA_K_NORM: k_norm_row,
               PA_CONV_W: (0.5 * conv_w[0]).astype(F32)}
    zero_row = jnp.zeros((1, D_MODEL), F32)
    pa, r = [], 0
    while r < PA_ROWS:
        piece = pa_rows.get(r, zero_row)
        pa.append(piece)
        r += piece.shape[0]
    qg = row(jnp.tile(q_norm[0], N_HEADS) * (HEAD_DIM ** -0.5 * LOG2_E))
    pb = [row(norm_b[0]), qg] + [zero_row] * (PB_ROWS - 2)
    return {
        "pa": jnp.concatenate(pa, axis=0),
        "w_in_a": (w_in_a[0] * in_scale).astype(BF16),
        "w_gate": jnp.concatenate([w_gate_x[0], w_gate_a[0]], axis=-1).astype(BF16),
        "w_out_a": w_out_a[0].astype(BF16),
        "w_kv": w_kv.astype(BF16),
        "pb": jnp.concatenate(pb, axis=0),
        "w_in_b": (w_in_b[0] * in_scale).astype(BF16),
        "sinks": sinks[0].astype(F32),
        "w_out_b": w_out_b[0].astype(BF16),
    }


def kernel(x_prompt, x_sample, state_conv, state_rglru, cache_k_win, cache_v_win, norm_a, w_in_a, conv_w, conv_b, w_gate_x, b_gate_x, w_gate_a, b_gate_a, lru_lambda, w_out_a, norm_kv, w_kv, k_norm, norm_b, w_in_b, q_norm, sinks, w_out_b):
    assert norm_a.shape[0] == 1 and norm_b.shape[0] == 1, "one recurrent + one attention layer"
    p = _prep_params(norm_a, w_in_a, conv_w, conv_b, w_gate_x, b_gate_x, w_gate_a, b_gate_a,
                     lru_lambda, w_out_a, norm_kv, w_kv, k_norm, norm_b, w_in_b, q_norm, sinks,
                     w_out_b)
    bp, tp, _ = x_prompt.shape
    bs, ts, _ = x_sample.shape

    x1p, kp, vp, conv_p, h_p = _layer_a(
        x_prompt, jnp.zeros((CONV_W - 1, bp, D_RNN), F32), jnp.zeros((bp, D_RNN), F32), p,
        "layer_a_prompt")
    tile_b = min(tp, B_TILE_ROWS)
    kp2 = kp.reshape(bp * tp, KV_W)
    vp2 = vp.reshape(bp * tp, KV_W)
    kp3 = kp.reshape(bp * tp // WINDOW, WINDOW, KV_W)
    vp3 = vp.reshape(bp * tp // WINDOW, WINDOW, KV_W)
    y_p, kt_p, vt_p = _layer_b(x1p.reshape(bp * tp, D_MODEL), kp3, vp3, kp2, vp2, p,
                               n_batch=bp, t_len=tp, nseg=1, seg=tile_b, prompt=True)

    x1s, ks, vs, conv_s, h_s = _layer_a(
        x_sample, jnp.transpose(state_conv[0], (1, 0, 2)), state_rglru[0], p, "layer_a_sample")
    y_s, kt_s, vt_s = _layer_b(
        x1s.reshape(bs * ts, D_MODEL), cache_k_win.reshape(bs, WINDOW, KV_W),
        cache_v_win.reshape(bs, WINDOW, KV_W), ks.reshape(bs * ts, KV_W),
        vs.reshape(bs * ts, KV_W), p, n_batch=bs, t_len=ts,
        nseg=min(bs, B_TILE_ROWS // ts), seg=ts, prompt=False)

    heads = lambda a: a.reshape(a.shape[0], WINDOW, N_KV, HEAD_DIM)
    return (y_p.reshape(bp, tp, D_MODEL), y_s.reshape(bs, ts, D_MODEL),
            jnp.transpose(conv_p, (1, 0, 2))[None], h_p[None], heads(kt_p), heads(vt_p),
            jnp.transpose(conv_s, (1, 0, 2))[None], h_s[None], heads(kt_s), heads(vt_s))
```
